```python
import jax
import jax.numpy as jnp
from jax import lax
import numpy as np

D_MODEL = 2048
BATCH = 1
SEQ = 16384
DEPTH = 2
DEC_BATCH = 32
DEC_SEQ = 64
PAST_LEN = 2048

CHUNK = 64
HEAD_DIM = 64
RET_WIDTH = D_MODEL // 4
RET_HEADS = RET_WIDTH // HEAD_DIM
FOX_WIDTH = D_MODEL // 2
FOX_HEADS = FOX_WIDTH // HEAD_DIM
CONV_CH = D_MODEL - RET_WIDTH - FOX_WIDTH
CONV_WIDTH = 31
Q_BLOCK = 128
OFF_FOX = 4 * RET_WIDTH
OFF_FGT = OFF_FOX + 3 * FOX_WIDTH
OFF_CONV = OFF_FGT + FOX_HEADS
N_IN = OFF_CONV + 2 * CONV_CH
N_GROUPS = 4
EXPERTS_PER_GROUP = 8
N_EXPERTS = N_GROUPS * EXPERTS_PER_GROUP
TOP_K_INNER = 2
D_EXPERT = D_MODEL // 4
ALPHA = (2 * DEPTH) ** 0.25
BETA = (8 * DEPTH) ** -0.25
ROPE_BASE = 10000.0
LN_EPS = 1e-5

kernel_name = 'hybrid_streaming_encoder_step'


def _layer_norm(x, g, b):
    xf = x.astype(jnp.float32)
    mu = jnp.mean(xf, -1, keepdims=True)
    xc = xf - mu
    var = jnp.mean(xc * xc, -1, keepdims=True)
    y = xc * lax.rsqrt(var + LN_EPS) * g.astype(jnp.float32) + b.astype(jnp.float32)
    return y.astype(x.dtype)


def _head_norm(o):
    mu = jnp.mean(o, -1, keepdims=True)
    oc = o - mu
    var = jnp.mean(oc * oc, -1, keepdims=True)
    return oc * lax.rsqrt(var + LN_EPS)


def _rotary(t, pos):
    half = t.shape[-1] // 2
    inv = ROPE_BASE ** (-jnp.arange(half, dtype=jnp.float32) / half)
    ang = pos.astype(jnp.float32)[:, None] * inv[None, :]
    cos = jnp.cos(ang)[None, :, None, :]
    sin = jnp.sin(ang)[None, :, None, :]
    tf = t.astype(jnp.float32)
    t1, t2 = tf[..., :half], tf[..., half:]
    return jnp.concatenate([t1 * cos - t2 * sin, t1 * sin + t2 * cos], -1).astype(t.dtype)


def _retention(q, k, v, s0):
    B, L, H, d = q.shape
    c = min(CHUNK, L)
    n = L // c
    f32 = jnp.float32
    lg = jnp.log1p(-jnp.exp2(-5.0 - jnp.arange(H, dtype=f32)))
    idx = jnp.arange(c, dtype=f32)
    diff = idx[:, None] - idx[None, :]
    causal = diff >= 0
    inner = jnp.where(causal[None], jnp.exp(jnp.where(causal, diff, 0.0)[None] * lg[:, None, None]), 0.0)
    q_dec = jnp.exp((idx + 1.0)[:, None] * lg[None, :])
    k_dec = jnp.exp((c - 1.0 - idx)[:, None] * lg[None, :])
    c_dec = jnp.exp(c * lg)

    def blocks(t):
        return t.astype(f32).reshape(B, n, c, H, d).transpose(1, 0, 2, 3, 4)

    qs, ks, vs = blocks(q), blocks(k * d ** -0.5), blocks(v)

    def step(S, xs):
        qc, kc, vc = xs
        att = jnp.einsum('bihd,bjhd->bhij', qc, kc) * inner[None]
        o = (jnp.einsum('bhij,bjhe->bihe', att, vc)
             + jnp.einsum('bihd,bhde->bihe', qc, S) * q_dec[None, :, :, None])
        S = S * c_dec[None, :, None, None] + jnp.einsum('bjhd,bjhe->bhde', kc * k_dec[None, :, :, None], vc)
        return S, o

    S, o = lax.scan(step, s0.astype(f32), (qs, ks, vs))
    return o.transpose(1, 0, 2, 3, 4).reshape(B, L, H, d), S


def _fox_block(q, cq, pos_q, k, v, ck, pos_k):
    s = jnp.einsum('bqhd,bkhd->bhqk', q, k, preferred_element_type=jnp.float32) * HEAD_DIM ** -0.5
    s = s + cq[..., :, None] - ck[..., None, :]
    s = jnp.where(pos_k[None, :] <= pos_q[:, None], s, -jnp.inf)
    p = jax.nn.softmax(s, axis=-1)
    return jnp.einsum('bhqk,bkhd->bqhd', p.astype(v.dtype), v)


def _fox_prompt(q, k, v, logf):
    B, L, H, d = q.shape
    ct = jnp.cumsum(logf, axis=1).transpose(0, 2, 1)
    nb = L // Q_BLOCK
    qb = q.reshape(B, nb, Q_BLOCK, H, d).transpose(1, 0, 2, 3, 4)
    cqb = ct.reshape(B, H, nb, Q_BLOCK).transpose(2, 0, 1, 3)
    pos_k = jnp.arange(L)

    def blk(a):
        i, qi, ci = a
        return _fox_block(qi, ci, i * Q_BLOCK + jnp.arange(Q_BLOCK), k, v, ct, pos_k)

    o = lax.map(blk, (jnp.arange(nb), qb, cqb))
    return o.transpose(1, 0, 2, 3, 4).reshape(B, L, H, d)


def _fox_extend(q, k, v, logf, pk, pv, plogf):
    P = pk.shape[1]
    L = q.shape[1]
    k_all = jnp.concatenate([pk, k], axis=1)
    v_all = jnp.concatenate([pv, v], axis=1)
    ct = jnp.cumsum(jnp.concatenate([plogf.astype(jnp.float32), logf], axis=1), axis=1).transpose(0, 2, 1)
    return _fox_block(q, ct[:, :, P:], P + jnp.arange(L), k_all, v_all, ct, jnp.arange(P + L))


def _conformer_conv(u, prev, w, b, g, beta):
    xp = jnp.concatenate([prev.astype(u.dtype), u], axis=1)
    y = lax.conv_general_dilated(xp, w[:, None, :].astype(u.dtype), (1,), 'VALID',
                                 dimension_numbers=('NWC', 'WIO', 'NWC'),
                                 feature_group_count=u.shape[-1]) + b
    y = jax.nn.silu(_layer_norm(y, g, beta))
    return y, xp[:, -(CONV_WIDTH - 1):]


def _hier_moe(x, rg_w, rg_b, re_w, re_b, w_gate, w_up, w_down):
    B, L, D = x.shape
    xt = x.reshape(B * L, D)
    pg = jax.nn.softmax((xt @ rg_w + rg_b).astype(jnp.float32), axis=-1)
    p_grp, g_idx = lax.top_k(pg, 1)
    le = (xt @ re_w + re_b).astype(jnp.float32).reshape(B * L, N_GROUPS, EXPERTS_PER_GROUP)
    le_sel = jnp.einsum('tg,tge->te', jax.nn.one_hot(g_idx[:, 0], N_GROUPS, dtype=jnp.float32), le)
    p_in, e_idx = lax.top_k(jax.nn.softmax(le_sel, axis=-1), TOP_K_INNER)
    gate = p_grp * p_in / jnp.sum(p_in, -1, keepdims=True)
    eid = g_idx * EXPERTS_PER_GROUP + e_idx
    combine = jnp.sum(jax.nn.one_hot(eid, N_EXPERTS, dtype=jnp.float32) * gate[..., None], axis=1).astype(x.dtype)
    y = jnp.zeros_like(xt)
    for e in range(N_EXPERTS):
        h = jax.nn.silu(xt @ w_gate[e]) * (xt @ w_up[e])
        y = y + combine[:, e:e + 1] * (h @ w_down[e])
    return y.reshape(B, L, D)


def setup_inputs(seed: int = 0) -> dict:
    k = jax.random.split(jax.random.key(seed), 27)
    f32 = jnp.float32

    def nrm(i, shape, s=1.0):
        return s * jax.random.normal(k[i], shape, f32)

    fox_bias = jnp.linspace(1.0, 6.0, FOX_HEADS, dtype=f32)
    col = np.ones((N_IN,), np.float32)
    col[2 * RET_WIDTH:3 * RET_WIDTH] = BETA
    col[OFF_FOX + 2 * FOX_WIDTH:OFF_FGT] = BETA
    return {
        'x_prompt': nrm(0, (BATCH, SEQ, D_MODEL)),
        'x_sample': nrm(1, (DEC_BATCH, DEC_SEQ, D_MODEL)),
        'cache_fox_k': nrm(2, (DEPTH, DEC_BATCH, PAST_LEN, FOX_HEADS, HEAD_DIM)),
        'cache_fox_v': nrm(3, (DEPTH, DEC_BATCH, PAST_LEN, FOX_HEADS, HEAD_DIM), BETA),
        'cache_fox_logf': jax.nn.log_sigmoid(fox_bias + nrm(4, (DEPTH, DEC_BATCH, PAST_LEN, FOX_HEADS))),
        'state_ret': nrm(5, (DEPTH, DEC_BATCH, RET_HEADS, HEAD_DIM, HEAD_DIM), 0.5),
        'state_conv': nrm(6, (DEPTH, DEC_BATCH, CONV_WIDTH - 1, CONV_CH), 0.5),
        'ln_in_g': 1.0 + nrm(7, (D_MODEL,), 0.02),
        'ln_in_b': nrm(8, (D_MODEL,), 0.02),
        'w_in': nrm(9, (DEPTH, D_MODEL, N_IN), D_MODEL ** -0.5) * jnp.asarray(col),
        'b_forget': fox_bias + nrm(10, (DEPTH, FOX_HEADS), 0.1),
        'conv_w': nrm(11, (DEPTH, CONV_WIDTH, CONV_CH), CONV_WIDTH ** -0.5),
        'conv_b': nrm(12, (DEPTH, CONV_CH), 0.01),
        'conv_ln_g': 1.0 + nrm(13, (DEPTH, CONV_CH), 0.02),
        'conv_ln_b': nrm(14, (DEPTH, CONV_CH), 0.02),
        'w_out': nrm(15, (DEPTH, D_MODEL, D_MODEL), BETA * D_MODEL ** -0.5),
        'ln1_g': 1.0 + nrm(16, (DEPTH, D_MODEL), 0.02),
        'ln1_b': nrm(17, (DEPTH, D_MODEL), 0.02),
        'router_group_w': nrm(18, (DEPTH, D_MODEL, N_GROUPS), D_MODEL ** -0.5),
        'router_group_b': nrm(19, (DEPTH, N_GROUPS), 0.01),
        'router_expert_w': nrm(20, (DEPTH, D_MODEL, N_EXPERTS), D_MODEL ** -0.5),
        'router_expert_b': nrm(21, (DEPTH, N_EXPERTS), 0.01),
        'w_gate': nrm(22, (DEPTH, N_EXPERTS, D_MODEL, D_EXPERT), D_MODEL ** -0.5),
        'w_up': nrm(23, (DEPTH, N_EXPERTS, D_MODEL, D_EXPERT), D_MODEL ** -0.5),
        'w_down': nrm(24, (DEPTH, N_EXPERTS, D_EXPERT, D_MODEL), BETA * D_EXPERT ** -0.5),
        'ln2_g': 1.0 + nrm(25, (DEPTH, D_MODEL), 0.02),
        'ln2_b': nrm(26, (DEPTH, D_MODEL), 0.02),
    }


def reference(x_prompt, x_sample, cache_fox_k, cache_fox_v, cache_fox_logf, state_ret, state_conv,
              ln_in_g, ln_in_b, w_in, b_forget, conv_w, conv_b, conv_ln_g, conv_ln_b, w_out,
              ln1_g, ln1_b, router_group_w, router_group_b, router_expert_w, router_expert_b,
              w_gate, w_up, w_down, ln2_g, ln2_b):

    def mixers(x, pos0, l, ret_s0, conv_prev, fox_past):
        B, L, _ = x.shape
        proj = x @ w_in[l]
        pos = pos0 + jnp.arange(L)

        def heads(lo, hi, h):
            return proj[..., lo:hi].reshape(B, L, h, HEAD_DIM)

        rq = _rotary(heads(0, RET_WIDTH, RET_HEADS), pos)
        rk = _rotary(heads(RET_WIDTH, 2 * RET_WIDTH, RET_HEADS), pos)
        rv = heads(2 * RET_WIDTH, 3 * RET_WIDTH, RET_HEADS)
        rg = proj[..., 3 * RET_WIDTH:OFF_FOX]
        o_r, s_new = _retention(rq, rk, rv, ret_s0)
        o_r = (_head_norm(o_r).reshape(B, L, RET_WIDTH) * jax.nn.silu(rg.astype(jnp.float32))).astype(x.dtype)
        fq = heads(OFF_FOX, OFF_FOX + FOX_WIDTH, FOX_HEADS)
        fk = heads(OFF_FOX + FOX_WIDTH, OFF_FOX + 2 * FOX_WIDTH, FOX_HEADS)
        fv = heads(OFF_FOX + 2 * FOX_WIDTH, OFF_FGT, FOX_HEADS)
        logf = jax.nn.log_sigmoid((proj[..., OFF_FGT:OFF_CONV] + b_forget[l]).astype(jnp.float32))
        if fox_past is None:
            o_f = _fox_prompt(fq, fk, fv, logf)
        else:
            o_f = _fox_extend(fq, fk, fv, logf, fox_past[0], fox_past[1], fox_past[2])
        u = proj[..., OFF_CONV:OFF_CONV + CONV_CH] * jax.nn.sigmoid(proj[..., OFF_CONV + CONV_CH:N_IN])
        o_c, conv_new = _conformer_conv(u, conv_prev, conv_w[l], conv_b[l], conv_ln_g[l], conv_ln_b[l])
        mixed = jnp.concatenate([o_r, o_f.reshape(B, L, FOX_WIDTH), o_c], axis=-1)
        return mixed @ w_out[l], (fk, fv, logf.astype(x.dtype), s_new.astype(x.dtype), conv_new)

    def run(x, pos0, has_past):
        B = x.shape[0]
        x = _layer_norm(x, ln_in_g, ln_in_b)
        ks, vs, lfs, rs, cs = [], [], [], [], []
        for l in range(DEPTH):
            if has_past:
                ret_s0 = state_ret[l]
                conv_prev = state_conv[l]
                fox_past = (cache_fox_k[l], cache_fox_v[l], cache_fox_logf[l])
            else:
                ret_s0 = jnp.zeros((B, RET_HEADS, HEAD_DIM, HEAD_DIM), jnp.float32)
                conv_prev = jnp.zeros((B, CONV_WIDTH - 1, CONV_CH), x.dtype)
                fox_past = None
            m, (k_new, v_new, lf_new, s_new, c_new) = mixers(x, pos0, l, ret_s0, conv_prev, fox_past)
            x = _layer_norm(ALPHA * x + m, ln1_g[l], ln1_b[l])
            f = _hier_moe(x, router_group_w[l], router_group_b[l], router_expert_w[l], router_expert_b[l],
                          w_gate[l], w_up[l], w_down[l])
            x = _layer_norm(ALPHA * x + f, ln2_g[l], ln2_b[l])
            ks.append(k_new)
            vs.append(v_new)
            lfs.append(lf_new)
            rs.append(s_new)
            cs.append(c_new)
        return x, jnp.stack(ks), jnp.stack(vs), jnp.stack(lfs), jnp.stack(rs), jnp.stack(cs)

    y_prompt, pk, pv, plf, prs, pcs = run(x_prompt, 0, False)
    y_sample, sk, sv, slf, srs, scs = run(x_sample, cache_fox_k.shape[2], True)
    return (y_prompt, y_sample, pk, pv, plf, prs, pcs, sk, sv, slf, srs, scs)
```

```python
import functools

import numpy as np
import jax
import jax.numpy as jnp
from jax import lax
from jax.experimental import pallas as pl
from jax.experimental.pallas import tpu as pltpu

F32 = jnp.float32
BF16 = jnp.bfloat16

HEAD_DIM = 64
CHUNK = 64
CONV_WIDTH = 31
N_GROUPS = 4
EXPERTS_PER_GROUP = 8
N_EXPERTS = N_GROUPS * EXPERTS_PER_GROUP
ROPE_BASE = 10000.0
LN_EPS = 1e-5
NEG_BIG = -1e30

LANES = 128
MOE_BLOCK_ROWS = 256
VMEM_LIMIT = 56 * 1024 * 1024


def _cparams(*sem):
    return pltpu.CompilerParams(dimension_semantics=sem, vmem_limit_bytes=VMEM_LIMIT)


def _row_tile(n, pref):
    t = min(n, pref)
    while n % t:
        t //= 2
    return t


def _ln_rows(y, g, b):
    mu = jnp.mean(y, axis=-1, keepdims=True)
    yc = y - mu
    var = jnp.mean(yc * yc, axis=-1, keepdims=True)
    return yc * lax.rsqrt(var + LN_EPS) * g + b


def _ln_kernel(x_ref, g_ref, b_ref, o32_ref, o16_ref):
    y = _ln_rows(x_ref[...], g_ref[...], b_ref[...])
    o32_ref[...] = y
    o16_ref[...] = y.astype(BF16)


def _layer_norm_in(x, g, b):
    T, D = x.shape
    tm = _row_tile(T, 256)
    row = pl.BlockSpec((tm, D), lambda i: (i, 0))
    vec = pl.BlockSpec((1, D), lambda i: (0, 0))
    return pl.pallas_call(
        _ln_kernel,
        grid=(T // tm,),
        in_specs=[row, vec, vec],
        out_specs=[row, row],
        out_shape=[jax.ShapeDtypeStruct((T, D), F32), jax.ShapeDtypeStruct((T, D), BF16)],
        compiler_params=_cparams("parallel"),
        name="ln_in",
    )(x, g.reshape(1, D), b.reshape(1, D))


def _res_ln_kernel(alpha, x_ref, f0_ref, f1_ref, g_ref, b_ref, o32_ref, o16_ref):
    y = _ln_rows(alpha * x_ref[...] + (f0_ref[...] + f1_ref[...]), g_ref[...], b_ref[...])
    o32_ref[...] = y
    o16_ref[...] = y.astype(BF16)


def _residual_ln(x, f0, f1, g, b, alpha):
    T, D = x.shape
    tm = _row_tile(T, 256)
    row = pl.BlockSpec((tm, D), lambda i: (i, 0))
    vec = pl.BlockSpec((1, D), lambda i: (0, 0))
    return pl.pallas_call(
        functools.partial(_res_ln_kernel, alpha),
        grid=(T // tm,),
        in_specs=[row, row, row, vec, vec],
        out_specs=[row, row],
        out_shape=[jax.ShapeDtypeStruct((T, D), F32), jax.ShapeDtypeStruct((T, D), BF16)],
        compiler_params=_cparams("parallel"),
        name="res_ln",
    )(x, f0, f1, g.reshape(1, D), b.reshape(1, D))


def _mm_kernel(x_ref, w_ref, o_ref):
    o_ref[...] = jnp.dot(x_ref[...], w_ref[...], preferred_element_type=F32).astype(o_ref.dtype)


def _matmul(x, w, out_dtype, name):
    T, K = x.shape
    N = w.shape[1]
    tm = _row_tile(T, 1024)
    tn = _row_tile(N, 1024)
    return pl.pallas_call(
        _mm_kernel,
        grid=(T // tm, N // tn),
        in_specs=[pl.BlockSpec((tm, K), lambda i, j: (i, 0)),
                  pl.BlockSpec((K, tn), lambda i, j: (0, j))],
        out_specs=pl.BlockSpec((tm, tn), lambda i, j: (i, j)),
        out_shape=jax.ShapeDtypeStruct((T, N), out_dtype),
        compiler_params=_cparams("parallel", "parallel"),
        name=name,
    )(x, w)


def _fox_proj_kernel(x_ref, wq_ref, wk_ref, wv_ref, q16_ref, k32_ref, v32_ref, k16_ref, v16_ref):
    x = x_ref[...]
    q16_ref[...] = jnp.dot(x, wq_ref[...], preferred_element_type=F32).astype(BF16)
    k = jnp.dot(x, wk_ref[...], preferred_element_type=F32)
    k32_ref[...] = k
    k16_ref[...] = k.astype(BF16)
    v = jnp.dot(x, wv_ref[...], preferred_element_type=F32)
    v32_ref[...] = v
    v16_ref[...] = v.astype(BF16)


def _fox_proj(x16, w_fox):
    T, D = x16.shape
    W = w_fox.shape[1] // 3
    tm = _row_tile(T, 512)
    tn = _row_tile(W, 512)
    nj = W // tn
    out = pl.BlockSpec((tm, tn), lambda i, j: (i, j))
    return pl.pallas_call(
        _fox_proj_kernel,
        grid=(T // tm, nj),
        in_specs=[pl.BlockSpec((tm, D), lambda i, j: (i, 0)),
                  pl.BlockSpec((D, tn), lambda i, j: (0, j)),
                  pl.BlockSpec((D, tn), lambda i, j: (0, nj + j)),
                  pl.BlockSpec((D, tn), lambda i, j: (0, 2 * nj + j))],
        out_specs=[out, out, out, out, out],
        out_shape=[jax.ShapeDtypeStruct((T, W), BF16), jax.ShapeDtypeStruct((T, W), F32),
                   jax.ShapeDtypeStruct((T, W), F32), jax.ShapeDtypeStruct((T, W), BF16),
                   jax.ShapeDtypeStruct((T, W), BF16)],
        compiler_params=_cparams("parallel", "parallel"),
        name="fox_proj",
    )(x16, w_fox, w_fox, w_fox)


def _log_sigmoid(z):
    return -(jnp.maximum(-z, 0.0) + jnp.log1p(jnp.exp(-jnp.abs(z))))


def _conv_fgt_proj_kernel(cc, nh, x_ref, w_ref, bf_ref, u_ref, lf_ref):
    x = x_ref[...]
    a = jnp.dot(x, w_ref[:, 0:cc], preferred_element_type=F32)
    gt = jnp.dot(x, w_ref[:, cc:2 * cc], preferred_element_type=F32)
    u_ref[...] = a * jax.nn.sigmoid(gt)
    z = jnp.dot(x, w_ref[:, 2 * cc:2 * cc + LANES], preferred_element_type=F32) + bf_ref[...]
    lf_ref[...] = _log_sigmoid(z)[:, :nh]


def _conv_fgt_proj(x16, w_cf, b_forget, cc, nh):
    T, D = x16.shape
    tm = _row_tile(T, 512)
    nw = w_cf.shape[1]
    bf = jnp.zeros((1, LANES), F32).at[0, :nh].set(b_forget)
    return pl.pallas_call(
        functools.partial(_conv_fgt_proj_kernel, cc, nh),
        grid=(T // tm,),
        in_specs=[pl.BlockSpec((tm, D), lambda i: (i, 0)),
                  pl.BlockSpec((D, nw), lambda i: (0, 0)),
                  pl.BlockSpec((1, LANES), lambda i: (0, 0))],
        out_specs=[pl.BlockSpec((tm, cc), lambda i: (i, 0)),
                   pl.BlockSpec((tm, nh), lambda i: (i, 0))],
        out_shape=[jax.ShapeDtypeStruct((T, cc), F32), jax.ShapeDtypeStruct((T, nh), F32)],
        compiler_params=_cparams("parallel"),
        name="conv_fgt_proj",
    )(x16, w_cf, bf)


def _trunc16(x):
    bits = lax.bitcast_convert_type(x, jnp.uint32) & jnp.uint32(0xFFFF0000)
    return lax.bitcast_convert_type(bits, F32)


def _split3(x):
    hi = _trunc16(x)
    r1 = x - hi
    mid = _trunc16(r1)
    lo = r1 - mid
    return hi.astype(BF16), mid.astype(BF16), lo.astype(BF16)


def _cumsum_kernel(nblk, x_ref, o_ref):
    r = lax.broadcasted_iota(jnp.int32, (LANES, LANES), 0)
    c = lax.broadcasted_iota(jnp.int32, (LANES, LANES), 1)
    tri = (r <= c).astype(BF16)

    def body(j, carry):
        hi, mid, lo = _split3(x_ref[0, j])
        s = (jnp.dot(hi, tri, preferred_element_type=F32)
             + jnp.dot(mid, tri, preferred_element_type=F32)
             + jnp.dot(lo, tri, preferred_element_type=F32)) + carry
        o_ref[0, j] = s
        return s[:, LANES - 1:LANES]

    lax.fori_loop(0, nblk, body, jnp.zeros((x_ref.shape[2], 1), F32))


def _cumsum_time(x):
    n, L, H = x.shape
    assert L % LANES == 0
    nblk = L // LANES
    xt = x.reshape(n, nblk, LANES, H).transpose(0, 1, 3, 2)
    blk = pl.BlockSpec((1, nblk, H, LANES), lambda i: (i, 0, 0, 0))
    out = pl.pallas_call(
        functools.partial(_cumsum_kernel, nblk),
        grid=(n,),
        in_specs=[blk],
        out_specs=blk,
        out_shape=jax.ShapeDtypeStruct((n, nblk, H, LANES), F32),
        compiler_params=_cparams("parallel"),
        name="cumsum_logf",
    )(xt)
    return out.transpose(0, 1, 3, 2).reshape(n, L, H)


RET_GROUP_HEADS = 4
RET_GROUP_LANES = RET_GROUP_HEADS * HEAD_DIM
RET_CHUNKS_PER_STEP = 8


def _ret_perm(n_heads):
    half = HEAD_DIM // 2
    perm = np.zeros(n_heads * HEAD_DIM, np.int32)
    for h in range(n_heads):
        g, hh = divmod(h, RET_GROUP_HEADS)
        for hf in range(2):
            for j in range(half):
                perm[g * RET_GROUP_LANES + hf * LANES + hh * half + j] = h * HEAD_DIM + hf * half + j
    return perm


def _retention_kernel(n_groups, cps, x_ref, cos_ref, sin_ref, inner_ref, qdec_ref, kdec_ref, cdec_ref,
                      s0_ref, o_ref, sout_ref, s_scr, o_scr):
    W = n_groups * RET_GROUP_LANES
    c = CHUNK
    step = pl.program_id(0)
    n_chunks = x_ref.shape[0] // c

    k_lane = lax.broadcasted_iota(jnp.int32, (c, RET_GROUP_LANES), 1)
    k_head = (k_lane % LANES) // (HEAD_DIM // 2)
    v_head = k_lane // HEAD_DIM
    sr = lax.broadcasted_iota(jnp.int32, (RET_GROUP_LANES, RET_GROUP_LANES), 0)
    sc = lax.broadcasted_iota(jnp.int32, (RET_GROUP_LANES, RET_GROUP_LANES), 1)
    s_mask = ((sr % LANES) // (HEAD_DIM // 2)) == (sc // HEAD_DIM)

    for ci in range(n_chunks):
        rows = slice(ci * c, (ci + 1) * c)
        cos = cos_ref[rows, :]
        sin = sin_ref[rows, :]
        if cps == 1:
            seq_local, first, last = ci, True, True
        else:
            seq_local = 0
            first = (step * n_chunks + ci) % cps == 0
            last = (step * n_chunks + ci) % cps == cps - 1
        for g in range(n_groups):
            lo = g * RET_GROUP_LANES

            def rot(t):
                t1, t2 = t[:, :LANES], t[:, LANES:]
                return jnp.concatenate([t1 * cos - t2 * sin, t1 * sin + t2 * cos], axis=1)

            q = rot(x_ref[rows, lo:lo + RET_GROUP_LANES])
            k = rot(x_ref[rows, W + lo:W + lo + RET_GROUP_LANES])
            v = x_ref[rows, 2 * W + lo:2 * W + lo + RET_GROUP_LANES]
            q16 = q.astype(BF16)
            k16 = k.astype(BF16)
            v16 = v.astype(BF16)
            kd16 = (k * kdec_ref[g]).astype(BF16)

            if cps == 1:
                s_prev = s0_ref[seq_local, g]
            else:
                @pl.when(first)
                def _():
                    s_scr[g] = s0_ref[0, g]
                s_prev = s_scr[g]

            zero = jnp.zeros_like(k16)
            k_bd = jnp.concatenate([jnp.where(k_head == hh, k16, zero) for hh in range(RET_GROUP_HEADS)], axis=0)
            v_bd = jnp.concatenate([jnp.where(v_head == hh, v16, zero) for hh in range(RET_GROUP_HEADS)], axis=0)
            s_all = lax.dot_general(q16, k_bd, (((1,), (1,)), ((), ())), preferred_element_type=F32)
            att = (s_all * inner_ref[g]).astype(BF16)
            o_intra = jnp.dot(att, v_bd, preferred_element_type=F32)
            o_cross = jnp.dot(q16, s_prev.astype(BF16), preferred_element_type=F32) * qdec_ref[g]
            o_scr[rows, lo:lo + RET_GROUP_LANES] = o_intra + o_cross

            kv = lax.dot_general(kd16, v16, (((0,), (0,)), ((), ())), preferred_element_type=F32)
            s_new = s_prev * cdec_ref[g] + jnp.where(s_mask, kv, 0.0)
            if cps == 1:
                sout_ref[seq_local, g] = s_new
            else:
                s_scr[g] = s_new

                @pl.when(last)
                def _():
                    sout_ref[0, g] = s_new

    pr = lax.broadcasted_iota(jnp.int32, (W, W), 0) // HEAD_DIM
    pc = lax.broadcasted_iota(jnp.int32, (W, W), 1) // HEAD_DIM
    avg = jnp.where(pr == pc, 1.0 / HEAD_DIM, 0.0).astype(BF16)

    def head_mean(t):
        hi, mid, lo3 = _split3(t)
        return (jnp.dot(hi, avg, preferred_element_type=F32)
                + jnp.dot(mid, avg, preferred_element_type=F32)
                + jnp.dot(lo3, avg, preferred_element_type=F32))

    o = o_scr[...]
    oc = o - head_mean(o)
    var = head_mean(oc * oc)
    gate = x_ref[:, 3 * W:4 * W]
    o_ref[...] = (oc * lax.rsqrt(var + LN_EPS) * (gate * jax.nn.sigmoid(gate))).astype(BF16)


def _retention_tables(n_heads):
    c = CHUNK
    n_groups = n_heads // RET_GROUP_HEADS
    lg = jnp.log1p(-jnp.exp2(-5.0 - jnp.arange(n_heads, dtype=F32)))
    idx = jnp.arange(c, dtype=F32)
    diff = idx[:, None] - idx[None, :]
    causal = diff >= 0
    inner = jnp.where(causal[None], jnp.exp(jnp.where(causal, diff, 0.0)[None] * lg[:, None, None]), 0.0)
    q_dec = jnp.exp((idx + 1.0)[:, None] * lg[None, :])
    k_dec = jnp.exp((c - 1.0 - idx)[:, None] * lg[None, :])
    c_dec = jnp.exp(c * lg)
    inner_t = inner.reshape(n_groups, RET_GROUP_HEADS, c, c).transpose(0, 2, 1, 3).reshape(n_groups, c, RET_GROUP_HEADS * c)
    out_head = np.arange(RET_GROUP_LANES) // HEAD_DIM
    k_head = (np.arange(RET_GROUP_LANES) % LANES) // (HEAD_DIM // 2)
    gh = np.arange(n_groups)[:, None] * RET_GROUP_HEADS
    qdec_t = q_dec[:, gh + out_head[None, :]].transpose(1, 0, 2)
    kdec_t = k_dec[:, gh + k_head[None, :]].transpose(1, 0, 2)
    cdec_t = c_dec[gh + out_head[None, :]][:, None, :]
    return inner_t, qdec_t, kdec_t, cdec_t


def _retention(xr, pos, s0g, seq_len, n_heads):
    T = xr.shape[0]
    W = n_heads * HEAD_DIM
    n_groups = n_heads // RET_GROUP_HEADS
    nseq = T // seq_len
    cps = seq_len // CHUNK
    tl = RET_CHUNKS_PER_STEP * CHUNK
    assert T % tl == 0 and (cps == 1 or cps % RET_CHUNKS_PER_STEP == 0)
    if cps == 1:
        sblk = pl.BlockSpec((RET_CHUNKS_PER_STEP, n_groups, RET_GROUP_LANES, RET_GROUP_LANES),
                            lambda i: (i, 0, 0, 0))
    else:
        spc = cps // RET_CHUNKS_PER_STEP
        sblk = pl.BlockSpec((1, n_groups, RET_GROUP_LANES, RET_GROUP_LANES), lambda i: (i // spc, 0, 0, 0))
    half = HEAD_DIM // 2
    inv = ROPE_BASE ** (-jnp.arange(half, dtype=F32) / half)
    ang = pos.astype(F32)[:, None] * inv[None, :]
    cos = jnp.tile(jnp.cos(ang), (1, LANES // half))
    sin = jnp.tile(jnp.sin(ang), (1, LANES // half))
    inner_t, qdec_t, kdec_t, cdec_t = _retention_tables(n_heads)

    def full(a):
        return pl.BlockSpec(a.shape, lambda i: (0,) * a.ndim)

    return pl.pallas_call(
        functools.partial(_retention_kernel, n_groups, cps),
        grid=(T // tl,),
        in_specs=[pl.BlockSpec((tl, 4 * W), lambda i: (i, 0)),
                  pl.BlockSpec((tl, LANES), lambda i: (i, 0)),
                  pl.BlockSpec((tl, LANES), lambda i: (i, 0)),
                  full(inner_t), full(qdec_t), full(kdec_t), full(cdec_t), sblk],
        out_specs=[pl.BlockSpec((tl, W), lambda i: (i, 0)), sblk],
        out_shape=[jax.ShapeDtypeStruct((T, W), BF16),
                   jax.ShapeDtypeStruct((nseq, n_groups, RET_GROUP_LANES, RET_GROUP_LANES), F32)],
        scratch_shapes=[pltpu.VMEM((n_groups, RET_GROUP_LANES, RET_GROUP_LANES), F32),
                        pltpu.VMEM((tl, W), F32)],
        compiler_params=_cparams("arbitrary"),
        name="retention",
    )(xr, cos, sin, inner_t, qdec_t, kdec_t, cdec_t, s0g)


def _ret_state_to_groups(s):
    n, H = s.shape[:2]
    G = H // RET_GROUP_HEADS
    half = HEAD_DIM // 2
    t = s.reshape(n, G, RET_GROUP_HEADS, 2, half, HEAD_DIM).transpose(0, 1, 3, 2, 4, 5)
    eye = jnp.eye(RET_GROUP_HEADS, dtype=s.dtype)
    t = t[:, :, :, :, :, None, :] * eye[None, None, None, :, None, :, None]
    return t.reshape(n, G, RET_GROUP_LANES, RET_GROUP_LANES)


def _ret_state_from_groups(sg):
    n, G = sg.shape[:2]
    half = HEAD_DIM // 2
    t = sg.reshape(n, G, 2, RET_GROUP_HEADS, half, RET_GROUP_HEADS, HEAD_DIM)
    t = jnp.stack([t[:, :, :, hh, :, hh, :] for hh in range(RET_GROUP_HEADS)], axis=2)
    return t.reshape(n, G * RET_GROUP_HEADS, HEAD_DIM, HEAD_DIM)


CONV_HALO = 32


def _conv_kernel(tl, u_ref, prev_ref, w_ref, cb_ref, g_ref, b_ref, o_ref, buf):
    j = pl.program_id(1)

    @pl.when(j == 0)
    def _():
        buf[0:CONV_HALO, :] = prev_ref[0]

    @pl.when(j > 0)
    def _():
        buf[0:CONV_HALO, :] = buf[tl:tl + CONV_HALO, :]

    buf[CONV_HALO:CONV_HALO + tl, :] = u_ref[0]
    off = CONV_HALO - (CONV_WIDTH - 1)
    acc = buf[off:off + tl, :] * w_ref[0:1, :]
    for w in range(1, CONV_WIDTH):
        acc = acc + buf[off + w:off + w + tl, :] * w_ref[w:w + 1, :]
    y = _ln_rows(acc + cb_ref[...], g_ref[...], b_ref[...])
    o_ref[0] = (y * jax.nn.sigmoid(y)).astype(BF16)


def _conformer_conv(u, prev, w, cb, g, b):
    nseq, L, C = u.shape
    tl = _row_tile(L, 512)
    assert tl >= CONV_HALO
    prev_p = jnp.pad(prev.astype(F32), ((0, 0), (CONV_HALO - (CONV_WIDTH - 1), 0), (0, 0)))
    w_p = jnp.pad(w, ((0, CONV_HALO - CONV_WIDTH), (0, 0)))
    vec = pl.BlockSpec((1, C), lambda s, j: (0, 0))
    return pl.pallas_call(
        functools.partial(_conv_kernel, tl),
        grid=(nseq, L // tl),
        in_specs=[pl.BlockSpec((1, tl, C), lambda s, j: (s, j, 0)),
                  pl.BlockSpec((1, CONV_HALO, C), lambda s, j: (s, 0, 0)),
                  pl.BlockSpec((CONV_HALO, C), lambda s, j: (0, 0)),
                  vec, vec, vec],
        out_specs=pl.BlockSpec((1, tl, C), lambda s, j: (s, j, 0)),
        out_shape=jax.ShapeDtypeStruct((nseq, L, C), BF16),
        scratch_shapes=[pltpu.VMEM((tl + CONV_HALO, C), F32)],
        compiler_params=_cparams("arbitrary", "arbitrary"),
        name="conformer_conv",
    )(u, prev_p, w_p, cb.reshape(1, C), g.reshape(1, C), b.reshape(1, C))


FOX_HEADS_PER_STEP = 2


def _fox_kernel(n_full_fn, bq, bk, q_ref, kf_ref, vf_ref, kd_ref, vd_ref, o_ref, m_scr, acc_scr):
    i = pl.program_id(1)
    n_full = n_full_fn(i)
    outs = []
    for h in range(FOX_HEADS_PER_STEP):
        q = q_ref[h]
        m_scr[h] = jnp.full((bq, LANES), NEG_BIG, F32)
        acc_scr[h] = jnp.zeros((bq, LANES), F32)

        def update(s, v, width):
            m_prev = m_scr[h]
            m_new = jnp.maximum(m_prev, jnp.max(s, axis=1, keepdims=True))
            p = jnp.exp(s - pltpu.repeat(m_new, width // LANES, axis=1) if width > LANES
                        else s - m_new[:, :width])
            acc_scr[h] = jnp.exp(m_prev - m_new) * acc_scr[h] + jnp.dot(
                p.astype(BF16), v, preferred_element_type=F32)
            m_scr[h] = m_new

        def body(j, carry):
            s = jnp.dot(q, kf_ref[h, j], preferred_element_type=F32)
            update(s, vf_ref[h, pl.ds(pl.multiple_of(j * bk, bk), bk), :], bk)
            return carry

        lax.fori_loop(0, n_full, body, 0)

        s = jnp.dot(q, kd_ref[h, 0], preferred_element_type=F32)
        r = lax.broadcasted_iota(jnp.int32, (bq, bq), 0)
        c = lax.broadcasted_iota(jnp.int32, (bq, bq), 1)
        update(jnp.where(c <= r, s, NEG_BIG), vd_ref[h], bq)

        acc = acc_scr[h]
        outs.append(acc / acc[:, HEAD_DIM:HEAD_DIM + 1])
    lane = lax.broadcasted_iota(jnp.int32, (bq, LANES), 1)
    o_ref[...] = jnp.where(lane < HEAD_DIM, outs[0], pltpu.roll(outs[1], HEAD_DIM, 1)).astype(BF16)


def _fox_attention(qp, kf, vf, kd, vd, bq, n_full_fn, kv_group_fn):
    G, Lq, _ = qp.shape
    nkb, bk = kf.shape[1], kf.shape[3]
    hp = FOX_HEADS_PER_STEP
    return pl.pallas_call(
        functools.partial(_fox_kernel, n_full_fn, bq, bk),
        grid=(G // hp, Lq // bq),
        in_specs=[pl.BlockSpec((hp, bq, LANES), lambda p, i: (p, i, 0)),
                  pl.BlockSpec((hp, nkb, LANES, bk), lambda p, i: (kv_group_fn(p), 0, 0, 0)),
                  pl.BlockSpec((hp, nkb * bk, LANES), lambda p, i: (kv_group_fn(p), 0, 0)),
                  pl.BlockSpec((hp, 1, LANES, bq), lambda p, i: (p, i, 0, 0)),
                  pl.BlockSpec((hp, bq, LANES), lambda p, i: (p, i, 0))],
        out_specs=pl.BlockSpec((bq, hp * HEAD_DIM), lambda p, i: (i, p)),
        out_shape=jax.ShapeDtypeStruct((Lq, G * HEAD_DIM), BF16),
        scratch_shapes=[pltpu.VMEM((hp, bq, LANES), F32), pltpu.VMEM((hp, bq, LANES), F32)],
        compiler_params=_cparams("parallel", "arbitrary"),
        name="fox_attention",
    )(qp, kf, vf, kd, vd)


def _fox_q_operand(q16, cq):
    n, L, H, d = q16.shape
    hi, mid, lo = _split3(cq)
    one = jnp.ones_like(hi)
    extra = jnp.stack([hi, mid, lo, one, one, one], axis=-1)
    t = jnp.concatenate([q16, extra, jnp.zeros((n, L, H, LANES - d - 6), BF16)], axis=-1)
    return t.transpose(0, 2, 1, 3).reshape(n * H, L, LANES)


def _fox_k_operand(k16, ck, bk):
    n, L, H, d = k16.shape
    hi, mid, lo = _split3(-ck)
    one = jnp.ones_like(hi)
    extra = jnp.stack([one, one, one, hi, mid, lo], axis=-1)
    t = jnp.concatenate([k16, extra, jnp.zeros((n, L, H, LANES - d - 6), BF16)], axis=-1)
    t = t.reshape(n, L // bk, bk, H, LANES).transpose(0, 3, 1, 4, 2)
    return t.reshape(n * H, L // bk, LANES, bk)


def _fox_v_operand(v16):
    n, L, H, d = v16.shape
    t = jnp.concatenate([v16, jnp.ones((n, L, H, 1), BF16), jnp.zeros((n, L, H, LANES - d - 1), BF16)], axis=-1)
    return t.transpose(0, 2, 1, 3).reshape(n * H, L, LANES)


def _out_router_kernel(alpha, x_ref, r_ref, f_ref, c_ref, w_ref, g_ref, b_ref, rw_ref, rb_ref,
                       o32_ref, o16_ref, route_ref):
    mixed = jnp.concatenate([r_ref[...], f_ref[...], c_ref[...]], axis=1)
    m = jnp.dot(mixed, w_ref[...], preferred_element_type=F32)
    y = _ln_rows(alpha * x_ref[...] + m, g_ref[...], b_ref[...])
    y16 = y.astype(BF16)
    o32_ref[...] = y
    o16_ref[...] = y16

    logits = jnp.dot(y16, rw_ref[...], preferred_element_type=F32) + rb_ref[...]
    lane = lax.broadcasted_iota(jnp.int32, logits.shape, 1).astype(F32)
    far = float(LANES)

    def rmax(t):
        return jnp.max(t, axis=1, keepdims=True)

    def rmin(t):
        return jnp.min(t, axis=1, keepdims=True)

    def rsum(t):
        return jnp.sum(t, axis=1, keepdims=True)

    gmask = lane < N_GROUPS
    lg = jnp.where(gmask, logits, -jnp.inf)
    mg = rmax(lg)
    p_grp = 1.0 / rsum(jnp.exp(lg - mg))
    g_idx = rmin(jnp.where(lg == mg, lane, far))
    e_lo = N_GROUPS + EXPERTS_PER_GROUP * g_idx
    emask = (lane >= e_lo) & (lane < e_lo + EXPERTS_PER_GROUP)
    le = jnp.where(emask, logits, -jnp.inf)
    ee = jnp.exp(le - rmax(le))
    pin = jnp.where(emask, ee / rsum(ee), -1.0)
    p1 = rmax(pin)
    i1 = rmin(jnp.where(pin == p1, lane, far))
    pin2 = jnp.where(lane == i1, -1.0, pin)
    p2 = rmax(pin2)
    i2 = rmin(jnp.where(pin2 == p2, lane, far))
    den = p1 + p2
    route_ref[...] = jnp.where(lane == 0, i1 - N_GROUPS,
                     jnp.where(lane == 1, i2 - N_GROUPS,
                     jnp.where(lane == 2, p_grp * p1 / den,
                     jnp.where(lane == 3, p_grp * p2 / den, 0.0))))


def _out_proj_router(x, o_r, o_f, o_c, w_out16, g, b, rw16, rb, alpha):
    T, D = x.shape
    tm = _row_tile(T, 256)

    def rows(a):
        return pl.BlockSpec((tm, a.shape[1]), lambda i: (i, 0))

    def full(a):
        return pl.BlockSpec(a.shape, lambda i: (0, 0))

    g2, b2 = g.reshape(1, D), b.reshape(1, D)
    return pl.pallas_call(
        functools.partial(_out_router_kernel, alpha),
        grid=(T // tm,),
        in_specs=[rows(x), rows(o_r), rows(o_f), rows(o_c), full(w_out16), full(g2), full(b2),
                  full(rw16), full(rb)],
        out_specs=[pl.BlockSpec((tm, D), lambda i: (i, 0)), pl.BlockSpec((tm, D), lambda i: (i, 0)),
                   pl.BlockSpec((tm, LANES), lambda i: (i, 0))],
        out_shape=[jax.ShapeDtypeStruct((T, D), F32), jax.ShapeDtypeStruct((T, D), BF16),
                   jax.ShapeDtypeStruct((T, LANES), F32)],
        compiler_params=_cparams("parallel"),
        name="out_proj_router",
    )(x, o_r, o_f, o_c, w_out16, g2, b2, rw16, rb)


def _moe_kernel(be_ref, nused_ref, x_ref, gate_ref, wg_ref, wu_ref, wd_ref, y_ref):
    blk = pl.program_id(0)

    @pl.when(blk < nused_ref[0])
    def _():
        x = x_ref[...]
        a = jnp.dot(x, wg_ref[0], preferred_element_type=F32)
        u = jnp.dot(x, wu_ref[0], preferred_element_type=F32)
        h = (a * jax.nn.sigmoid(a) * u).astype(BF16)
        y_ref[...] = jnp.dot(h, wd_ref[0], preferred_element_type=F32) * gate_ref[...]

    @pl.when(blk >= nused_ref[0])
    def _():
        y_ref[...] = jnp.zeros_like(y_ref)


def _moe_experts(xs, row_gate, block_expert, n_used, wg16, wu16, wd16):
    R, D = xs.shape
    E, _, DE = wg16.shape
    bm = MOE_BLOCK_ROWS
    grid_spec = pltpu.PrefetchScalarGridSpec(
        num_scalar_prefetch=2,
        grid=(R // bm,),
        in_specs=[pl.BlockSpec((bm, D), lambda i, be, nu: (i, 0)),
                  pl.BlockSpec((bm, 1), lambda i, be, nu: (i, 0)),
                  pl.BlockSpec((1, D, DE), lambda i, be, nu: (be[i], 0, 0)),
                  pl.BlockSpec((1, D, DE), lambda i, be, nu: (be[i], 0, 0)),
                  pl.BlockSpec((1, DE, D), lambda i, be, nu: (be[i], 0, 0))],
        out_specs=pl.BlockSpec((bm, D), lambda i, be, nu: (i, 0)),
    )
    return pl.pallas_call(
        _moe_kernel,
        grid_spec=grid_spec,
        out_shape=jax.ShapeDtypeStruct((R, D), F32),
        compiler_params=_cparams("arbitrary"),
        name="moe_experts",
    )(block_expert, n_used, xs, row_gate, wg16, wu16, wd16)


def _moe_dispatch(route):
    T = route.shape[0]
    bm = MOE_BLOCK_ROWS
    eid = route[:, 0:2].astype(jnp.int32).reshape(-1)
    gate = route[:, 2:4].reshape(-1)
    n_assign = 2 * T
    n_rows = n_assign + N_EXPERTS * bm
    n_blocks = n_rows // bm
    counts = jnp.zeros((N_EXPERTS,), jnp.int32).at[eid].add(1)
    padded = ((counts + bm - 1) // bm) * bm
    ends = jnp.cumsum(padded)
    starts = ends - padded
    order = jnp.argsort(eid, stable=True)
    sorted_rank = jnp.zeros((n_assign,), jnp.int32).at[order].set(jnp.arange(n_assign, dtype=jnp.int32))
    first = jnp.cumsum(counts) - counts
    pos = starts[eid] + sorted_rank - first[eid]
    row_token = jnp.zeros((n_rows,), jnp.int32).at[pos].set(jnp.arange(n_assign, dtype=jnp.int32) // 2)
    row_gate = jnp.zeros((n_rows,), F32).at[pos].set(gate)
    n_used = ends[-1] // bm
    blk_start = jnp.arange(n_blocks, dtype=jnp.int32) * bm
    block_expert = jnp.minimum(jnp.searchsorted(ends, blk_start, side='right'), N_EXPERTS - 1).astype(jnp.int32)
    last_used = block_expert[jnp.maximum(n_used - 1, 0)]
    block_expert = jnp.where(jnp.arange(n_blocks) < n_used, block_expert, last_used)
    return row_token, row_gate.reshape(n_rows, 1), block_expert, n_used.reshape(1).astype(jnp.int32), pos.reshape(T, 2)


def kernel(x_prompt, x_sample, cache_fox_k, cache_fox_v, cache_fox_logf, state_ret, state_conv, ln_in_g, ln_in_b, w_in, b_forget, conv_w, conv_b, conv_ln_g, conv_ln_b, w_out, ln1_g, ln1_b, router_group_w, router_group_b, router_expert_w, router_expert_b, w_gate, w_up, w_down, ln2_g, ln2_b):
    BP, LP, D = x_prompt.shape
    BS, LS, _ = x_sample.shape
    depth = w_in.shape[0]
    past = cache_fox_k.shape[2]
    fox_heads = cache_fox_k.shape[3]
    ret_heads = state_ret.shape[2]
    conv_ch = state_conv.shape[3]
    ret_w = ret_heads * HEAD_DIM
    fox_w = fox_heads * HEAD_DIM
    off_fox = 4 * ret_w
    off_fgt = off_fox + 3 * fox_w
    off_conv = off_fgt + fox_heads
    alpha = (2 * depth) ** 0.25
    TP, TS = BP * LP, BS * LS
    assert BP == 1 and LS == CHUNK and past % 512 == 0

    x_raw = jnp.concatenate([x_prompt.reshape(TP, D), x_sample.reshape(TS, D)], axis=0)
    x32, x16 = _layer_norm_in(x_raw, ln_in_g, ln_in_b)

    perm = _ret_perm(ret_heads)
    pos_all = jnp.concatenate([jnp.arange(LP), past + jnp.tile(jnp.arange(LS), BS)])
    fox_bq = _row_tile(LP, 512)

    ks, vs, lfs, rs_p, rs_s, cs_p, cs_s = [], [], [], [], [], [], []
    for l in range(depth):
        w16 = w_in[l].astype(BF16)
        w_ret = jnp.concatenate([w16[:, 0:ret_w][:, perm],
                                 (w16[:, ret_w:2 * ret_w] * (HEAD_DIM ** -0.5))[:, perm],
                                 w16[:, 2 * ret_w:4 * ret_w]], axis=1)
        w_fox = jnp.concatenate([w16[:, off_fox:off_fox + fox_w] * (HEAD_DIM ** -0.5),
                                 w16[:, off_fox + fox_w:off_fgt]], axis=1)
        w_cf = jnp.concatenate([w16[:, off_conv:], w16[:, off_fgt:off_conv],
                                jnp.zeros((D, LANES - fox_heads), BF16)], axis=1)

        xr = _matmul(x16, w_ret, F32, "ret_proj")
        fq16, fk32, fv32, fk16, fv16 = _fox_proj(x16, w_fox)
        u, logf = _conv_fgt_proj(x16, w_cf, b_forget[l], conv_ch, fox_heads)

        s0 = jnp.concatenate([jnp.zeros((BP,) + state_ret.shape[2:], F32), state_ret[l]], axis=0)
        s0g = _ret_state_to_groups(s0)
        o_r_p, sg_p = _retention(xr[:TP], pos_all[:TP], s0g[:BP], LP, ret_heads)
        o_r_s, sg_s = _retention(xr[TP:], pos_all[TP:], s0g[BP:], LS, ret_heads)
        o_r = jnp.concatenate([o_r_p, o_r_s], axis=0)
        rs_p.append(_ret_state_from_groups(sg_p))
        rs_s.append(_ret_state_from_groups(sg_s))

        u_p = u[:TP].reshape(BP, LP, conv_ch)
        u_s = u[TP:].reshape(BS, LS, conv_ch)
        cw, cb, cg, cbeta = conv_w[l], conv_b[l], conv_ln_g[l], conv_ln_b[l]
        o_c_p = _conformer_conv(u_p, jnp.zeros((BP, CONV_WIDTH - 1, conv_ch), F32), cw, cb, cg, cbeta)
        o_c_s = _conformer_conv(u_s, state_conv[l], cw, cb, cg, cbeta)
        o_c = jnp.concatenate([o_c_p.reshape(TP, conv_ch), o_c_s.reshape(TS, conv_ch)], axis=0)
        cs_p.append(u_p[:, LP - (CONV_WIDTH - 1):])
        cs_s.append(jnp.concatenate([state_conv[l], u_s], axis=1)[:, -(CONV_WIDTH - 1):])

        ct_p = _cumsum_time(logf[:TP].reshape(BP, LP, fox_heads))
        hd = (fox_heads, HEAD_DIM)
        q_p = _fox_q_operand(fq16[:TP].reshape((BP, LP) + hd), ct_p)
        k_p = _fox_k_operand(fk16[:TP].reshape((BP, LP) + hd), ct_p, fox_bq)
        v_p = _fox_v_operand(fv16[:TP].reshape((BP, LP) + hd))
        o_f_p = _fox_attention(q_p, k_p, v_p, k_p, v_p, fox_bq, lambda i: i, lambda p: p)

        lf_all = jnp.concatenate([cache_fox_logf[l].astype(F32), logf[TP:].reshape(BS, LS, fox_heads)], axis=1)
        pad_t = (-lf_all.shape[1]) % LANES
        ct_s = _cumsum_time(jnp.pad(lf_all, ((0, 0), (0, pad_t), (0, 0))))
        ct_cache, ct_new = ct_s[:, :past], ct_s[:, past:past + LS]
        q_s = _fox_q_operand(fq16[TP:].reshape((BS, LS) + hd), ct_new)
        kn_s = _fox_k_operand(fk16[TP:].reshape((BS, LS) + hd), ct_new, LS)
        vn_s = _fox_v_operand(fv16[TP:].reshape((BS, LS) + hd))
        kc_s = _fox_k_operand(cache_fox_k[l].astype(BF16), ct_cache, 512)
        vc_s = _fox_v_operand(cache_fox_v[l].astype(BF16))
        n_cache_blocks = past // 512
        o_f_s = _fox_attention(q_s, kc_s, vc_s, kn_s, vn_s, LS, lambda i: n_cache_blocks, lambda p: p)
        o_f_s = o_f_s.reshape(LS, BS, fox_w).transpose(1, 0, 2).reshape(TS, fox_w)
        o_f = jnp.concatenate([o_f_p, o_f_s], axis=0)

        rw = jnp.concatenate([router_group_w[l], router_expert_w[l],
                              jnp.zeros((D, LANES - N_GROUPS - N_EXPERTS), F32)], axis=1).astype(BF16)
        rb = jnp.concatenate([router_group_b[l], router_expert_b[l],
                              jnp.zeros((LANES - N_GROUPS - N_EXPERTS,), F32)]).reshape(1, LANES)
        x1_32, x1_16, route = _out_proj_router(x32, o_r, o_f, o_c, w_out[l].astype(BF16),
                                               ln1_g[l], ln1_b[l], rw, rb, alpha)

        row_token, row_gate, block_expert, n_used, pos = _moe_dispatch(route)
        xs = jnp.take(x1_16, row_token, axis=0)
        ys = _moe_experts(xs, row_gate, block_expert, n_used,
                          w_gate[l].astype(BF16), w_up[l].astype(BF16), w_down[l].astype(BF16))
        f0 = jnp.take(ys, pos[:, 0], axis=0)
        f1 = jnp.take(ys, pos[:, 1], axis=0)
        x32, x16 = _residual_ln(x1_32, f0, f1, ln2_g[l], ln2_b[l], alpha)

        ks.append(fk32)
        vs.append(fv32)
        lfs.append(logf)

    def split(ts, tail):
        a = jnp.stack(ts)
        return (a[:, :TP].reshape((depth, BP, LP) + tail), a[:, TP:].reshape((depth, BS, LS) + tail))

    pk, sk = split(ks, (fox_heads, HEAD_DIM))
    pv, sv = split(vs, (fox_heads, HEAD_DIM))
    plf, slf = split(lfs, (fox_heads,))
    return (x32[:TP].reshape(BP, LP, D), x32[TP:].reshape(BS, LS, D),
            pk, pv, plf, jnp.stack(rs_p), jnp.stack(cs_p),
            sk, sv, slf, jnp.stack(rs_s), jnp.stack(cs_s))
```

```python
import functools

import numpy as np
import jax
import jax.numpy as jnp
from jax import lax
from jax.experimental import pallas as pl
from jax.experimental.pallas import tpu as pltpu

F32 = jnp.float32
BF16 = jnp.bfloat16

HEAD_DIM = 64
CHUNK = 64
CONV_WIDTH = 31
N_GROUPS = 4
EXPERTS_PER_GROUP = 8
N_EXPERTS = N_GROUPS * EXPERTS_PER_GROUP
ROPE_BASE = 10000.0
LN_EPS = 1e-5
NEG_BIG = -1e30

LANES = 128
MOE_BLOCK_ROWS = 256
VMEM_LIMIT = 56 * 1024 * 1024


def _cparams(*sem):
    return pltpu.CompilerParams(dimension_semantics=sem, vmem_limit_bytes=VMEM_LIMIT)


def _row_tile(n, pref):
    t = min(n, pref)
    while n % t:
        t //= 2
    return t


def _ln_rows(y, g, b):
    mu = jnp.mean(y, axis=-1, keepdims=True)
    yc = y - mu
    var = jnp.mean(yc * yc, axis=-1, keepdims=True)
    return yc * lax.rsqrt(var + LN_EPS) * g + b


def _ln_kernel(x_ref, g_ref, b_ref, o32_ref, o16_ref):
    y = _ln_rows(x_ref[...], g_ref[...], b_ref[...])
    o32_ref[...] = y
    o16_ref[...] = y.astype(BF16)


def _layer_norm_in(x, g, b):
    T, D = x.shape
    tm = _row_tile(T, 256)
    row = pl.BlockSpec((tm, D), lambda i: (i, 0))
    vec = pl.BlockSpec((1, D), lambda i: (0, 0))
    return pl.pallas_call(
        _ln_kernel,
        grid=(T // tm,),
        in_specs=[row, vec, vec],
        out_specs=[row, row],
        out_shape=[jax.ShapeDtypeStruct((T, D), F32), jax.ShapeDtypeStruct((T, D), BF16)],
        compiler_params=_cparams("parallel"),
        name="ln_in",
    )(x, g.reshape(1, D), b.reshape(1, D))


def _res_ln_kernel(alpha, x_ref, f0_ref, f1_ref, g_ref, b_ref, o32_ref, o16_ref):
    y = _ln_rows(alpha * x_ref[...] + (f0_ref[...] + f1_ref[...]), g_ref[...], b_ref[...])
    o32_ref[...] = y
    o16_ref[...] = y.astype(BF16)


def _residual_ln(x, f0, f1, g, b, alpha):
    T, D = x.shape
    tm = _row_tile(T, 256)
    row = pl.BlockSpec((tm, D), lambda i: (i, 0))
    vec = pl.BlockSpec((1, D), lambda i: (0, 0))
    return pl.pallas_call(
        functools.partial(_res_ln_kernel, alpha),
        grid=(T // tm,),
        in_specs=[row, row, row, vec, vec],
        out_specs=[row, row],
        out_shape=[jax.ShapeDtypeStruct((T, D), F32), jax.ShapeDtypeStruct((T, D), BF16)],
        compiler_params=_cparams("parallel"),
        name="res_ln",
    )(x, f0, f1, g.reshape(1, D), b.reshape(1, D))


def _mm_kernel(x_ref, w_ref, o_ref):
    o_ref[...] = jnp.dot(x_ref[...], w_ref[...], preferred_element_type=F32).astype(o_ref.dtype)


def _matmul(x, w, out_dtype, name):
    T, K = x.shape
    N = w.shape[1]
    tm = _row_tile(T, 1024)
    tn = _row_tile(N, 1024)
    return pl.pallas_call(
        _mm_kernel,
        grid=(T // tm, N // tn),
        in_specs=[pl.BlockSpec((tm, K), lambda i, j: (i, 0)),
                  pl.BlockSpec((K, tn), lambda i, j: (0, j))],
        out_specs=pl.BlockSpec((tm, tn), lambda i, j: (i, j)),
        out_shape=jax.ShapeDtypeStruct((T, N), out_dtype),
        compiler_params=_cparams("parallel", "parallel"),
        name=name,
    )(x, w)


def _fox_proj_kernel(x_ref, wq_ref, wk_ref, wv_ref, q16_ref, k32_ref, v32_ref, k16_ref, v16_ref):
    x = x_ref[...]
    q16_ref[...] = jnp.dot(x, wq_ref[...], preferred_element_type=F32).astype(BF16)
    k = jnp.dot(x, wk_ref[...], preferred_element_type=F32)
    k32_ref[...] = k
    k16_ref[...] = k.astype(BF16)
    v = jnp.dot(x, wv_ref[...], preferred_element_type=F32)
    v32_ref[...] = v
    v16_ref[...] = v.astype(BF16)


def _fox_proj(x16, w_fox):
    T, D = x16.shape
    W = w_fox.shape[1] // 3
    tm = _row_tile(T, 512)
    tn = _row_tile(W, 512)
    nj = W // tn
    out = pl.BlockSpec((tm, tn), lambda i, j: (i, j))
    return pl.pallas_call(
        _fox_proj_kernel,
        grid=(T // tm, nj),
        in_specs=[pl.BlockSpec((tm, D), lambda i, j: (i, 0)),
                  pl.BlockSpec((D, tn), lambda i, j: (0, j)),
                  pl.BlockSpec((D, tn), lambda i, j: (0, nj + j)),
                  pl.BlockSpec((D, tn), lambda i, j: (0, 2 * nj + j))],
        out_specs=[out, out, out, out, out],
        out_shape=[jax.ShapeDtypeStruct((T, W), BF16), jax.ShapeDtypeStruct((T, W), F32),
                   jax.ShapeDtypeStruct((T, W), F32), jax.ShapeDtypeStruct((T, W), BF16),
                   jax.ShapeDtypeStruct((T, W), BF16)],
        compiler_params=_cparams("parallel", "parallel"),
        name="fox_proj",
    )(x16, w_fox, w_fox, w_fox)


def _log_sigmoid(z):
    return -(jnp.maximum(-z, 0.0) + jnp.log1p(jnp.exp(-jnp.abs(z))))


def _conv_fgt_proj_kernel(cc, nh, x_ref, w_ref, bf_ref, u_ref, lf_ref):
    x = x_ref[...]
    a = jnp.dot(x, w_ref[:, 0:cc], preferred_element_type=F32)
    gt = jnp.dot(x, w_ref[:, cc:2 * cc], preferred_element_type=F32)
    u_ref[...] = a * jax.nn.sigmoid(gt)
    z = jnp.dot(x, w_ref[:, 2 * cc:2 * cc + LANES], preferred_element_type=F32) + bf_ref[...]
    lf_ref[...] = _log_sigmoid(z)[:, :nh]


def _conv_fgt_proj(x16, w_cf, b_forget, cc, nh):
    T, D = x16.shape
    tm = _row_tile(T, 512)
    nw = w_cf.shape[1]
    bf = jnp.zeros((1, LANES), F32).at[0, :nh].set(b_forget)
    return pl.pallas_call(
        functools.partial(_conv_fgt_proj_kernel, cc, nh),
        grid=(T // tm,),
        in_specs=[pl.BlockSpec((tm, D), lambda i: (i, 0)),
                  pl.BlockSpec((D, nw), lambda i: (0, 0)),
                  pl.BlockSpec((1, LANES), lambda i: (0, 0))],
        out_specs=[pl.BlockSpec((tm, cc), lambda i: (i, 0)),
                   pl.BlockSpec((tm, nh), lambda i: (i, 0))],
        out_shape=[jax.ShapeDtypeStruct((T, cc), F32), jax.ShapeDtypeStruct((T, nh), F32)],
        compiler_params=_cparams("parallel"),
        name="conv_fgt_proj",
    )(x16, w_cf, bf)


def _trunc16(x):
    bits = lax.bitcast_convert_type(x, jnp.uint32) & jnp.uint32(0xFFFF0000)
    return lax.bitcast_convert_type(bits, F32)


def _split3(x):
    hi = _trunc16(x)
    r1 = x - hi
    mid = _trunc16(r1)
    lo = r1 - mid
    return hi.astype(BF16), mid.astype(BF16), lo.astype(BF16)


def _cumsum_kernel(nblk, x_ref, o_ref):
    r = lax.broadcasted_iota(jnp.int32, (LANES, LANES), 0)
    c = lax.broadcasted_iota(jnp.int32, (LANES, LANES), 1)
    tri = (r <= c).astype(BF16)

    def body(j, carry):
        hi, mid, lo = _split3(x_ref[0, j])
        s = (jnp.dot(hi, tri, preferred_element_type=F32)
             + jnp.dot(mid, tri, preferred_element_type=F32)
             + jnp.dot(lo, tri, preferred_element_type=F32)) + carry
        o_ref[0, j] = s
        return s[:, LANES - 1:LANES]

    lax.fori_loop(0, nblk, body, jnp.zeros((x_ref.shape[2], 1), F32))


def _cumsum_time(x):
    n, L, H = x.shape
    assert L % LANES == 0
    nblk = L // LANES
    xt = x.reshape(n, nblk, LANES, H).transpose(0, 1, 3, 2)
    blk = pl.BlockSpec((1, nblk, H, LANES), lambda i: (i, 0, 0, 0))
    out = pl.pallas_call(
        functools.partial(_cumsum_kernel, nblk),
        grid=(n,),
        in_specs=[blk],
        out_specs=blk,
        out_shape=jax.ShapeDtypeStruct((n, nblk, H, LANES), F32),
        compiler_params=_cparams("parallel"),
        name="cumsum_logf",
    )(xt)
    return out.transpose(0, 1, 3, 2).reshape(n, L, H)


RET_GROUP_HEADS = 4
RET_GROUP_LANES = RET_GROUP_HEADS * HEAD_DIM
RET_CHUNKS_PER_STEP = 8


def _ret_perm(n_heads):
    half = HEAD_DIM // 2
    perm = np.zeros(n_heads * HEAD_DIM, np.int32)
    for h in range(n_heads):
        g, hh = divmod(h, RET_GROUP_HEADS)
        for hf in range(2):
            for j in range(half):
                perm[g * RET_GROUP_LANES + hf * LANES + hh * half + j] = h * HEAD_DIM + hf * half + j
    return perm


def _retention_kernel(n_groups, cps, x_ref, cos_ref, sin_ref, inner_ref, qdec_ref, kdec_ref, cdec_ref,
                      s0_ref, o_ref, sout_ref, s_scr, o_scr):
    W = n_groups * RET_GROUP_LANES
    c = CHUNK
    step = pl.program_id(0)
    n_chunks = x_ref.shape[0] // c

    k_lane = lax.broadcasted_iota(jnp.int32, (c, RET_GROUP_LANES), 1)
    k_head = (k_lane % LANES) // (HEAD_DIM // 2)
    v_head = k_lane // HEAD_DIM
    sr = lax.broadcasted_iota(jnp.int32, (RET_GROUP_LANES, RET_GROUP_LANES), 0)
    sc = lax.broadcasted_iota(jnp.int32, (RET_GROUP_LANES, RET_GROUP_LANES), 1)
    s_mask = ((sr % LANES) // (HEAD_DIM // 2)) == (sc // HEAD_DIM)

    for ci in range(n_chunks):
        rows = slice(ci * c, (ci + 1) * c)
        cos = cos_ref[rows, :]
        sin = sin_ref[rows, :]
        if cps == 1:
            seq_local, first, last = ci, True, True
        else:
            seq_local = 0
            first = (step * n_chunks + ci) % cps == 0
            last = (step * n_chunks + ci) % cps == cps - 1
        for g in range(n_groups):
            lo = g * RET_GROUP_LANES

            def rot(t):
                t1, t2 = t[:, :LANES], t[:, LANES:]
                return jnp.concatenate([t1 * cos - t2 * sin, t1 * sin + t2 * cos], axis=1)

            q = rot(x_ref[rows, lo:lo + RET_GROUP_LANES])
            k = rot(x_ref[rows, W + lo:W + lo + RET_GROUP_LANES])
            v = x_ref[rows, 2 * W + lo:2 * W + lo + RET_GROUP_LANES]
            q16 = q.astype(BF16)
            k16 = k.astype(BF16)
            v16 = v.astype(BF16)
            kd16 = (k * kdec_ref[g]).astype(BF16)

            if cps == 1:
                s_prev = s0_ref[seq_local, g]
            else:
                @pl.when(first)
                def _():
                    s_scr[g] = s0_ref[0, g]
                s_prev = s_scr[g]

            zero = jnp.zeros_like(k16)
            k_bd = jnp.concatenate([jnp.where(k_head == hh, k16, zero) for hh in range(RET_GROUP_HEADS)], axis=0)
            v_bd = jnp.concatenate([jnp.where(v_head == hh, v16, zero) for hh in range(RET_GROUP_HEADS)], axis=0)
            s_all = lax.dot_general(q16, k_bd, (((1,), (1,)), ((), ())), preferred_element_type=F32)
            att = (s_all * inner_ref[g]).astype(BF16)
            o_intra = jnp.dot(att, v_bd, preferred_element_type=F32)
            o_cross = jnp.dot(q16, s_prev.astype(BF16), preferred_element_type=F32) * qdec_ref[g]
            o_scr[rows, lo:lo + RET_GROUP_LANES] = o_intra + o_cross

            kv = lax.dot_general(kd16, v16, (((0,), (0,)), ((), ())), preferred_element_type=F32)
            s_new = s_prev * cdec_ref[g] + jnp.where(s_mask, kv, 0.0)
            if cps == 1:
                sout_ref[seq_local, g] = s_new
            else:
                s_scr[g] = s_new

                @pl.when(last)
                def _():
                    sout_ref[0, g] = s_new

    pr = lax.broadcasted_iota(jnp.int32, (W, W), 0) // HEAD_DIM
    pc = lax.broadcasted_iota(jnp.int32, (W, W), 1) // HEAD_DIM
    avg = jnp.where(pr == pc, 1.0 / HEAD_DIM, 0.0).astype(BF16)

    def head_mean(t):
        hi, mid, lo3 = _split3(t)
        return (jnp.dot(hi, avg, preferred_element_type=F32)
                + jnp.dot(mid, avg, preferred_element_type=F32)
                + jnp.dot(lo3, avg, preferred_element_type=F32))

    o = o_scr[...]
    oc = o - head_mean(o)
    var = head_mean(oc * oc)
    gate = x_ref[:, 3 * W:4 * W]
    o_ref[...] = (oc * lax.rsqrt(var + LN_EPS) * (gate * jax.nn.sigmoid(gate))).astype(BF16)


def _retention_tables(n_heads):
    c = CHUNK
    n_groups = n_heads // RET_GROUP_HEADS
    lg = jnp.log1p(-jnp.exp2(-5.0 - jnp.arange(n_heads, dtype=F32)))
    idx = jnp.arange(c, dtype=F32)
    diff = idx[:, None] - idx[None, :]
    causal = diff >= 0
    inner = jnp.where(causal[None], jnp.exp(jnp.where(causal, diff, 0.0)[None] * lg[:, None, None]), 0.0)
    q_dec = jnp.exp((idx + 1.0)[:, None] * lg[None, :])
    k_dec = jnp.exp((c - 1.0 - idx)[:, None] * lg[None, :])
    c_dec = jnp.exp(c * lg)
    inner_t = inner.reshape(n_groups, RET_GROUP_HEADS, c, c).transpose(0, 2, 1, 3).reshape(n_groups, c, RET_GROUP_HEADS * c)
    out_head = np.arange(RET_GROUP_LANES) // HEAD_DIM
    k_head = (np.arange(RET_GROUP_LANES) % LANES) // (HEAD_DIM // 2)
    gh = np.arange(n_groups)[:, None] * RET_GROUP_HEADS
    qdec_t = q_dec[:, gh + out_head[None, :]].transpose(1, 0, 2)
    kdec_t = k_dec[:, gh + k_head[None, :]].transpose(1, 0, 2)
    cdec_t = c_dec[gh + out_head[None, :]][:, None, :]
    return inner_t, qdec_t, kdec_t, cdec_t


def _rotary_tables(pos):
    half = HEAD_DIM // 2
    inv = ROPE_BASE ** (-jnp.arange(half, dtype=F32) / half)
    ang = pos.astype(F32)[:, None] * inv[None, :]
    return (jnp.tile(jnp.cos(ang), (1, LANES // half)), jnp.tile(jnp.sin(ang), (1, LANES // half)))


def _skip_ref(kern, idx, *refs):
    return kern(*refs[:idx], *refs[idx + 1:])


def _with_row_alias(kern, in_specs, args, o_prev):
    if o_prev is None:
        return kern, in_specs, args, {}
    idx = len(args)
    return (functools.partial(_skip_ref, kern, idx), in_specs + [pl.BlockSpec(memory_space=pl.ANY)],
            args + [o_prev], {idx: 0})


def _retention(xr, cos, sin, s0g, row0, n_rows, seq_len, n_heads, o_prev=None):
    T = xr.shape[0]
    W = n_heads * HEAD_DIM
    n_groups = n_heads // RET_GROUP_HEADS
    nseq = n_rows // seq_len
    cps = seq_len // CHUNK
    tl = RET_CHUNKS_PER_STEP * CHUNK
    assert n_rows % tl == 0 and row0 % tl == 0 and (cps == 1 or cps % RET_CHUNKS_PER_STEP == 0)
    off = row0 // tl
    if cps == 1:
        sblk = pl.BlockSpec((RET_CHUNKS_PER_STEP, n_groups, RET_GROUP_LANES, RET_GROUP_LANES),
                            lambda i: (i, 0, 0, 0))
    else:
        spc = cps // RET_CHUNKS_PER_STEP
        sblk = pl.BlockSpec((1, n_groups, RET_GROUP_LANES, RET_GROUP_LANES), lambda i: (i // spc, 0, 0, 0))
    inner_t, qdec_t, kdec_t, cdec_t = _retention_tables(n_heads)

    def full(a):
        return pl.BlockSpec(a.shape, lambda i: (0,) * a.ndim)

    kern = functools.partial(_retention_kernel, n_groups, cps)
    in_specs = [pl.BlockSpec((tl, 4 * W), lambda i: (i + off, 0)),
                pl.BlockSpec((tl, LANES), lambda i: (i + off, 0)),
                pl.BlockSpec((tl, LANES), lambda i: (i + off, 0)),
                full(inner_t), full(qdec_t), full(kdec_t), full(cdec_t), sblk]
    args = [xr, cos, sin, inner_t, qdec_t, kdec_t, cdec_t, s0g]
    kern, in_specs, args, aliases = _with_row_alias(kern, in_specs, args, o_prev)
    return pl.pallas_call(
        kern,
        grid=(n_rows // tl,),
        in_specs=in_specs,
        out_specs=[pl.BlockSpec((tl, W), lambda i: (i + off, 0)), sblk],
        out_shape=[jax.ShapeDtypeStruct((T, W), BF16),
                   jax.ShapeDtypeStruct((nseq, n_groups, RET_GROUP_LANES, RET_GROUP_LANES), F32)],
        scratch_shapes=[pltpu.VMEM((n_groups, RET_GROUP_LANES, RET_GROUP_LANES), F32),
                        pltpu.VMEM((tl, W), F32)],
        input_output_aliases=aliases,
        compiler_params=_cparams("arbitrary"),
        name="retention",
    )(*args)


def _ret_state_to_groups(s):
    n, H = s.shape[:2]
    G = H // RET_GROUP_HEADS
    half = HEAD_DIM // 2
    t = s.reshape(n, G, RET_GROUP_HEADS, 2, half, HEAD_DIM).transpose(0, 1, 3, 2, 4, 5)
    eye = jnp.eye(RET_GROUP_HEADS, dtype=s.dtype)
    t = t[:, :, :, :, :, None, :] * eye[None, None, None, :, None, :, None]
    return t.reshape(n, G, RET_GROUP_LANES, RET_GROUP_LANES)


def _ret_state_from_groups(sg):
    n, G = sg.shape[:2]
    half = HEAD_DIM // 2
    t = sg.reshape(n, G, 2, RET_GROUP_HEADS, half, RET_GROUP_HEADS, HEAD_DIM)
    t = jnp.stack([t[:, :, :, hh, :, hh, :] for hh in range(RET_GROUP_HEADS)], axis=2)
    return t.reshape(n, G * RET_GROUP_HEADS, HEAD_DIM, HEAD_DIM)


CONV_HALO = 32


def _conv_kernel(tl, u_ref, prev_ref, w_ref, cb_ref, g_ref, b_ref, o_ref, buf):
    j = pl.program_id(1)

    @pl.when(j == 0)
    def _():
        buf[0:CONV_HALO, :] = prev_ref[0]

    @pl.when(j > 0)
    def _():
        buf[0:CONV_HALO, :] = buf[tl:tl + CONV_HALO, :]

    buf[CONV_HALO:CONV_HALO + tl, :] = u_ref[...]
    off = CONV_HALO - (CONV_WIDTH - 1)
    acc = buf[off:off + tl, :] * w_ref[0:1, :]
    for w in range(1, CONV_WIDTH):
        acc = acc + buf[off + w:off + w + tl, :] * w_ref[w:w + 1, :]
    y = _ln_rows(acc + cb_ref[...], g_ref[...], b_ref[...])
    o_ref[...] = (y * jax.nn.sigmoid(y)).astype(BF16)


def _conformer_conv(u, prev, w, cb, g, b, row0, seq_len, o_prev=None):
    T, C = u.shape
    nseq = prev.shape[0]
    tl = _row_tile(seq_len, 512)
    assert tl >= CONV_HALO and row0 % tl == 0
    off = row0 // tl
    nt = seq_len // tl
    prev_p = jnp.pad(prev.astype(F32), ((0, 0), (CONV_HALO - (CONV_WIDTH - 1), 0), (0, 0)))
    w_p = jnp.pad(w, ((0, CONV_HALO - CONV_WIDTH), (0, 0)))
    vec = pl.BlockSpec((1, C), lambda s, j: (0, 0))
    rows = pl.BlockSpec((tl, C), lambda s, j: (off + s * nt + j, 0))
    in_specs = [rows,
                pl.BlockSpec((1, CONV_HALO, C), lambda s, j: (s, 0, 0)),
                pl.BlockSpec((CONV_HALO, C), lambda s, j: (0, 0)),
                vec, vec, vec]
    args = [u, prev_p, w_p, cb.reshape(1, C), g.reshape(1, C), b.reshape(1, C)]
    kern, in_specs, args, aliases = _with_row_alias(functools.partial(_conv_kernel, tl), in_specs, args, o_prev)
    return pl.pallas_call(
        kern,
        grid=(nseq, nt),
        in_specs=in_specs,
        out_specs=rows,
        out_shape=jax.ShapeDtypeStruct((T, C), BF16),
        scratch_shapes=[pltpu.VMEM((tl + CONV_HALO, C), F32)],
        input_output_aliases=aliases,
        compiler_params=_cparams("arbitrary", "arbitrary"),
        name="conformer_conv",
    )(*args)


FOX_HEADS_PER_STEP = 2
LOG2E = 1.4426950408889634


def _fox_prompt_kernel(bq, q_ref, k_ref, v_ref, o_ref, m_scr, acc_scr):
    i = pl.program_id(1)
    heads = range(FOX_HEADS_PER_STEP)
    q = [q_ref[h] for h in heads]
    for h in heads:
        m_scr[h] = jnp.full((bq, LANES), NEG_BIG, F32)
        acc_scr[h] = jnp.zeros((bq, LANES), F32)

    def update(h, s, v):
        m_prev = m_scr[h]
        m_new = jnp.maximum(m_prev, jnp.max(s, axis=1, keepdims=True))
        p = jnp.exp2(s - pltpu.repeat(m_new, bq // LANES, axis=1))
        acc_scr[h] = jnp.exp2(m_prev - m_new) * acc_scr[h] + jnp.dot(
            p.astype(BF16), v, preferred_element_type=F32)
        m_scr[h] = m_new

    def key_rows(j):
        return pl.ds(pl.multiple_of(j * bq, bq), bq)

    def body(j, carry):
        s = [jnp.dot(q[h], k_ref[h, j], preferred_element_type=F32) for h in heads]
        for h in heads:
            update(h, s[h], v_ref[h, key_rows(j), :])
        return carry

    lax.fori_loop(0, i, body, 0)

    r = lax.broadcasted_iota(jnp.int32, (bq, bq), 0)
    c = lax.broadcasted_iota(jnp.int32, (bq, bq), 1)
    s = [jnp.dot(q[h], k_ref[h, i], preferred_element_type=F32) for h in heads]
    for h in heads:
        update(h, jnp.where(c <= r, s[h], NEG_BIG), v_ref[h, key_rows(i), :])

    outs = [acc_scr[h] / acc_scr[h][:, HEAD_DIM:HEAD_DIM + 1] for h in heads]
    lane = lax.broadcasted_iota(jnp.int32, (bq, LANES), 1)
    o_ref[...] = jnp.where(lane < HEAD_DIM, outs[0], pltpu.roll(outs[1], HEAD_DIM, 1)).astype(BF16)


def _fox_prompt_attention(qp, kp, vp, n_rows_out):
    H, L, _ = qp.shape
    nkb, bq = kp.shape[1], kp.shape[3]
    hp = FOX_HEADS_PER_STEP
    return pl.pallas_call(
        functools.partial(_fox_prompt_kernel, bq),
        grid=(H // hp, L // bq),
        in_specs=[pl.BlockSpec((hp, bq, LANES), lambda p, i: (p, i, 0)),
                  pl.BlockSpec((hp, nkb, LANES, bq), lambda p, i: (p, 0, 0, 0)),
                  pl.BlockSpec((hp, L, LANES), lambda p, i: (p, 0, 0))],
        out_specs=pl.BlockSpec((bq, hp * HEAD_DIM), lambda p, i: (i, p)),
        out_shape=jax.ShapeDtypeStruct((n_rows_out, H * HEAD_DIM), BF16),
        scratch_shapes=[pltpu.VMEM((hp, bq, LANES), F32), pltpu.VMEM((hp, bq, LANES), F32)],
        compiler_params=_cparams("parallel", "arbitrary"),
        name="fox_prompt",
    )(qp, kp, vp)


def _fox_sample_kernel(past, ls, q_ref, kn_ref, vn_ref, kc_ref, vc_ref, cq_ref, ck_ref, o_ref):
    q = q_ref[...]
    lane = lax.broadcasted_iota(jnp.int32, q.shape, 1)
    zero = jnp.zeros_like(q)
    q2 = jnp.concatenate([jnp.where(lane < HEAD_DIM, q, zero), jnp.where(lane >= HEAD_DIM, q, zero)], axis=0)
    pad = jnp.zeros((LANES - ls, LANES), BF16)
    k_all = jnp.concatenate([kc_ref[0].astype(BF16), kn_ref[...], pad], axis=0)
    v_all = jnp.concatenate([vc_ref[0].astype(BF16), vn_ref[...], pad], axis=0)
    nk = past + LANES
    s = lax.dot_general(q2, k_all, (((1,), (1,)), ((), ())), preferred_element_type=F32)
    row = lax.broadcasted_iota(jnp.int32, (2 * ls, nk), 0)
    col = lax.broadcasted_iota(jnp.int32, (2 * ls, nk), 1)
    ck = jnp.where(row < ls, ck_ref[0, 0:1, :], ck_ref[0, 1:2, :])
    visible = (col < past) | (col - past <= row % ls)
    s = jnp.where(visible, s + cq_ref[0] - ck, NEG_BIG)
    p = jnp.exp2(s - jnp.max(s, axis=1, keepdims=True))
    den = jnp.sum(p, axis=1, keepdims=True)
    o2 = jnp.dot(p.astype(BF16), v_all, preferred_element_type=F32) / den
    o_ref[...] = jnp.where(lane < HEAD_DIM, o2[0:ls], o2[ls:2 * ls]).astype(BF16)


def _fox_sample_attention(q16, k16, v16, cache_k, cache_v, layer, cq, ck, row0, ls, o_prev):
    T, W = q16.shape
    past = cache_k.shape[1]
    n_pairs = W // LANES
    B = cq.shape[0] // n_pairs
    assert ls <= LANES and row0 % ls == 0
    off = row0 // ls
    rows = pl.BlockSpec((ls, LANES), lambda b, p: (off + b, p))
    cache = pl.BlockSpec((1, past, LANES), lambda b, p: (layer * B + b, 0, p))
    in_specs = [rows, rows, rows, cache, cache,
                pl.BlockSpec((1, 2 * ls, 1), lambda b, p: (b * n_pairs + p, 0, 0)),
                pl.BlockSpec((1, 2, past + LANES), lambda b, p: (b * n_pairs + p, 0, 0))]
    args = [q16, k16, v16, cache_k, cache_v, cq, ck]
    kern, in_specs, args, aliases = _with_row_alias(
        functools.partial(_fox_sample_kernel, past, ls), in_specs, args, o_prev)
    return pl.pallas_call(
        kern,
        grid=(B, n_pairs),
        in_specs=in_specs,
        out_specs=rows,
        out_shape=jax.ShapeDtypeStruct((T, W), BF16),
        input_output_aliases=aliases,
        compiler_params=_cparams("parallel", "parallel"),
        name="fox_sample",
    )(*args)


def _fox_q_operand(q16, cq):
    n, L, H, d = q16.shape
    hi, mid, lo = _split3(cq)
    one = jnp.ones_like(hi)
    extra = jnp.stack([hi, mid, lo, one, one, one], axis=-1)
    t = jnp.concatenate([q16, extra, jnp.zeros((n, L, H, LANES - d - 6), BF16)], axis=-1)
    return t.transpose(0, 2, 1, 3).reshape(n * H, L, LANES)


def _fox_k_operand(k16, ck, bk):
    n, L, H, d = k16.shape
    hi, mid, lo = _split3(-ck)
    one = jnp.ones_like(hi)
    extra = jnp.stack([one, one, one, hi, mid, lo], axis=-1)
    t = jnp.concatenate([k16, extra, jnp.zeros((n, L, H, LANES - d - 6), BF16)], axis=-1)
    t = t.reshape(n, L // bk, bk, H, LANES).transpose(0, 3, 1, 4, 2)
    return t.reshape(n * H, L // bk, LANES, bk)


def _fox_v_operand(v16):
    n, L, H, d = v16.shape
    t = jnp.concatenate([v16, jnp.ones((n, L, H, 1), BF16), jnp.zeros((n, L, H, LANES - d - 1), BF16)], axis=-1)
    return t.transpose(0, 2, 1, 3).reshape(n * H, L, LANES)


def _out_router_kernel(alpha, x_ref, r_ref, f_ref, c_ref, w_ref, g_ref, b_ref, rw_ref, rb_ref,
                       o32_ref, o16_ref, route_ref):
    mixed = jnp.concatenate([r_ref[...], f_ref[...], c_ref[...]], axis=1)
    m = jnp.dot(mixed, w_ref[...], preferred_element_type=F32)
    y = _ln_rows(alpha * x_ref[...] + m, g_ref[...], b_ref[...])
    y16 = y.astype(BF16)
    o32_ref[...] = y
    o16_ref[...] = y16

    logits = jnp.dot(y16, rw_ref[...], preferred_element_type=F32) + rb_ref[...]
    lane = lax.broadcasted_iota(jnp.int32, logits.shape, 1).astype(F32)
    far = float(LANES)

    def rmax(t):
        return jnp.max(t, axis=1, keepdims=True)

    def rmin(t):
        return jnp.min(t, axis=1, keepdims=True)

    def rsum(t):
        return jnp.sum(t, axis=1, keepdims=True)

    gmask = lane < N_GROUPS
    lg = jnp.where(gmask, logits, -jnp.inf)
    mg = rmax(lg)
    p_grp = 1.0 / rsum(jnp.exp(lg - mg))
    g_idx = rmin(jnp.where(lg == mg, lane, far))
    e_lo = N_GROUPS + EXPERTS_PER_GROUP * g_idx
    emask = (lane >= e_lo) & (lane < e_lo + EXPERTS_PER_GROUP)
    le = jnp.where(emask, logits, -jnp.inf)
    ee = jnp.exp(le - rmax(le))
    pin = jnp.where(emask, ee / rsum(ee), -1.0)
    p1 = rmax(pin)
    i1 = rmin(jnp.where(pin == p1, lane, far))
    pin2 = jnp.where(lane == i1, -1.0, pin)
    p2 = rmax(pin2)
    i2 = rmin(jnp.where(pin2 == p2, lane, far))
    den = p1 + p2
    route_ref[...] = jnp.where(lane == 0, i1 - N_GROUPS,
                     jnp.where(lane == 1, i2 - N_GROUPS,
                     jnp.where(lane == 2, p_grp * p1 / den,
                     jnp.where(lane == 3, p_grp * p2 / den, 0.0))))


def _out_proj_router(x, o_r, o_f, o_c, w_out16, g, b, rw16, rb, alpha):
    T, D = x.shape
    tm = _row_tile(T, 256)

    def rows(a):
        return pl.BlockSpec((tm, a.shape[1]), lambda i: (i, 0))

    def full(a):
        return pl.BlockSpec(a.shape, lambda i: (0, 0))

    g2, b2 = g.reshape(1, D), b.reshape(1, D)
    return pl.pallas_call(
        functools.partial(_out_router_kernel, alpha),
        grid=(T // tm,),
        in_specs=[rows(x), rows(o_r), rows(o_f), rows(o_c), full(w_out16), full(g2), full(b2),
                  full(rw16), full(rb)],
        out_specs=[pl.BlockSpec((tm, D), lambda i: (i, 0)), pl.BlockSpec((tm, D), lambda i: (i, 0)),
                   pl.BlockSpec((tm, LANES), lambda i: (i, 0))],
        out_shape=[jax.ShapeDtypeStruct((T, D), F32), jax.ShapeDtypeStruct((T, D), BF16),
                   jax.ShapeDtypeStruct((T, LANES), F32)],
        compiler_params=_cparams("parallel"),
        name="out_proj_router",
    )(x, o_r, o_f, o_c, w_out16, g2, b2, rw16, rb)


def _moe_kernel(be_ref, nused_ref, x_ref, gate_ref, wg_ref, wu_ref, wd_ref, y_ref, wg16, wu16, wd16):
    blk = pl.program_id(0)
    used = blk < nused_ref[0]
    prev_expert = be_ref[jnp.maximum(blk - 1, 0)]

    @pl.when(used & ((blk == 0) | (be_ref[blk] != prev_expert)))
    def _():
        wg16[...] = wg_ref[0].astype(BF16)
        wu16[...] = wu_ref[0].astype(BF16)
        wd16[...] = wd_ref[0].astype(BF16)

    @pl.when(used)
    def _():
        x = x_ref[...]
        a = jnp.dot(x, wg16[...], preferred_element_type=F32)
        u = jnp.dot(x, wu16[...], preferred_element_type=F32)
        h = (a * jax.nn.sigmoid(a) * u).astype(BF16)
        y_ref[...] = jnp.dot(h, wd16[...], preferred_element_type=F32) * gate_ref[...]

    @pl.when(jnp.logical_not(used))
    def _():
        y_ref[...] = jnp.zeros_like(y_ref)


def _moe_experts(xs, row_gate, block_expert, n_used, wg, wu, wd, layer):
    R, D = xs.shape
    DE = wg.shape[3]
    bm = MOE_BLOCK_ROWS
    grid_spec = pltpu.PrefetchScalarGridSpec(
        num_scalar_prefetch=2,
        grid=(R // bm,),
        in_specs=[pl.BlockSpec((bm, D), lambda i, be, nu: (i, 0)),
                  pl.BlockSpec((bm, 1), lambda i, be, nu: (i, 0)),
                  pl.BlockSpec((None, 1, D, DE), lambda i, be, nu: (layer, be[i], 0, 0)),
                  pl.BlockSpec((None, 1, D, DE), lambda i, be, nu: (layer, be[i], 0, 0)),
                  pl.BlockSpec((None, 1, DE, D), lambda i, be, nu: (layer, be[i], 0, 0))],
        out_specs=pl.BlockSpec((bm, D), lambda i, be, nu: (i, 0)),
        scratch_shapes=[pltpu.VMEM((D, DE), BF16), pltpu.VMEM((D, DE), BF16), pltpu.VMEM((DE, D), BF16)],
    )
    return pl.pallas_call(
        _moe_kernel,
        grid_spec=grid_spec,
        out_shape=jax.ShapeDtypeStruct((R, D), F32),
        compiler_params=_cparams("arbitrary"),
        name="moe_experts",
    )(block_expert, n_used, xs, row_gate, wg, wu, wd)


def _moe_dispatch(route):
    T = route.shape[0]
    bm = MOE_BLOCK_ROWS
    eid = route[:, 0:2].astype(jnp.int32).reshape(-1)
    gate = route[:, 2:4].reshape(-1)
    n_assign = 2 * T
    n_rows = n_assign + N_EXPERTS * bm
    n_blocks = n_rows // bm
    counts = jnp.sum((eid[:, None] == jnp.arange(N_EXPERTS)[None, :]).astype(jnp.int32), axis=0)
    padded = ((counts + bm - 1) // bm) * bm
    ends = jnp.cumsum(padded)
    starts = ends - padded
    first = jnp.cumsum(counts) - counts
    order = jnp.argsort(eid, stable=True).astype(jnp.int32)
    rank = jnp.argsort(order).astype(jnp.int32)
    pos = rank + (starts - first)[eid]
    n_used = ends[-1] // bm
    blk_start = jnp.arange(n_blocks, dtype=jnp.int32) * bm
    block_expert = jnp.minimum(jnp.searchsorted(ends, blk_start, side='right'), N_EXPERTS - 1).astype(jnp.int32)
    last_used = block_expert[jnp.maximum(n_used - 1, 0)]
    block_expert = jnp.where(jnp.arange(n_blocks) < n_used, block_expert, last_used)
    r = jnp.arange(n_rows, dtype=jnp.int32)
    e_r = block_expert[r // bm]
    s_r = r - starts[e_r] + first[e_r]
    valid = (r < ends[-1]) & (s_r < first[e_r] + counts[e_r])
    a_r = order[jnp.clip(s_r, 0, n_assign - 1)]
    row_token = jnp.where(valid, a_r // 2, 0)
    row_gate = jnp.where(valid, gate[a_r], 0.0)
    return row_token, row_gate.reshape(n_rows, 1), block_expert, n_used.reshape(1).astype(jnp.int32), pos.reshape(T, 2)


def kernel(x_prompt, x_sample, cache_fox_k, cache_fox_v, cache_fox_logf, state_ret, state_conv, ln_in_g, ln_in_b, w_in, b_forget, conv_w, conv_b, conv_ln_g, conv_ln_b, w_out, ln1_g, ln1_b, router_group_w, router_group_b, router_expert_w, router_expert_b, w_gate, w_up, w_down, ln2_g, ln2_b):
    BP, LP, D = x_prompt.shape
    BS, LS, _ = x_sample.shape
    depth = w_in.shape[0]
    past = cache_fox_k.shape[2]
    fox_heads = cache_fox_k.shape[3]
    ret_heads = state_ret.shape[2]
    conv_ch = state_conv.shape[3]
    ret_w = ret_heads * HEAD_DIM
    fox_w = fox_heads * HEAD_DIM
    off_fox = 4 * ret_w
    off_fgt = off_fox + 3 * fox_w
    off_conv = off_fgt + fox_heads
    alpha = (2 * depth) ** 0.25
    TP, TS = BP * LP, BS * LS
    assert BP == 1 and LS == CHUNK and past % 512 == 0

    x_raw = jnp.concatenate([x_prompt.reshape(TP, D), x_sample.reshape(TS, D)], axis=0)
    x32, x16 = _layer_norm_in(x_raw, ln_in_g, ln_in_b)

    perm = _ret_perm(ret_heads)
    pos_all = jnp.concatenate([jnp.arange(LP), past + jnp.tile(jnp.arange(LS), BS)])
    cos_t, sin_t = _rotary_tables(pos_all)
    fox_bq = _row_tile(LP, 512)
    hd = (fox_heads, HEAD_DIM)
    n_pairs = fox_heads // FOX_HEADS_PER_STEP
    cache_k = cache_fox_k.reshape(depth * BS, past, fox_w)
    cache_v = cache_fox_v.reshape(depth * BS, past, fox_w)
    qk_scale = HEAD_DIM ** -0.5
    zero_state = jnp.zeros((BP, ret_heads // RET_GROUP_HEADS, RET_GROUP_LANES, RET_GROUP_LANES), F32)
    zero_conv = jnp.zeros((BP, CONV_WIDTH - 1, conv_ch), F32)

    ks, vs, lfs, rs_p, rs_s, cs_p, cs_s = [], [], [], [], [], [], []
    for l in range(depth):
        wl = w_in[l]
        w_ret = jnp.concatenate([wl[:, 0:ret_w][:, perm],
                                 (wl[:, ret_w:2 * ret_w] * qk_scale)[:, perm],
                                 wl[:, 2 * ret_w:4 * ret_w]], axis=1).astype(BF16)
        w_fox = jnp.concatenate([wl[:, off_fox:off_fox + fox_w] * (qk_scale * LOG2E),
                                 wl[:, off_fox + fox_w:off_fgt]], axis=1).astype(BF16)
        w_cf = jnp.concatenate([wl[:, off_conv:], wl[:, off_fgt:off_conv],
                                jnp.zeros((D, LANES - fox_heads), F32)], axis=1).astype(BF16)

        xr = _matmul(x16, w_ret, F32, "ret_proj")
        fq16, fk32, fv32, fk16, fv16 = _fox_proj(x16, w_fox)
        u, logf = _conv_fgt_proj(x16, w_cf, b_forget[l], conv_ch, fox_heads)

        o_r, sg_p = _retention(xr, cos_t, sin_t, zero_state, 0, TP, LP, ret_heads)
        o_r, sg_s = _retention(xr, cos_t, sin_t, _ret_state_to_groups(state_ret[l]), TP, TS, LS, ret_heads,
                               o_prev=o_r)
        rs_p.append(_ret_state_from_groups(sg_p))
        rs_s.append(_ret_state_from_groups(sg_s))

        cw, cb, cg, cbeta = conv_w[l], conv_b[l], conv_ln_g[l], conv_ln_b[l]
        o_c = _conformer_conv(u, zero_conv, cw, cb, cg, cbeta, 0, LP)
        o_c = _conformer_conv(u, state_conv[l], cw, cb, cg, cbeta, TP, LS, o_prev=o_c)
        u_p = u[:TP].reshape(BP, LP, conv_ch)
        u_s = u[TP:].reshape(BS, LS, conv_ch)
        cs_p.append(u_p[:, LP - (CONV_WIDTH - 1):])
        cs_s.append(jnp.concatenate([state_conv[l], u_s], axis=1)[:, -(CONV_WIDTH - 1):])

        ct_p = _cumsum_time(logf[:TP].reshape(BP, LP, fox_heads)) * LOG2E
        q_p = _fox_q_operand(fq16[:TP].reshape((BP, LP) + hd), ct_p)
        k_p = _fox_k_operand(fk16[:TP].reshape((BP, LP) + hd), ct_p, fox_bq)
        v_p = _fox_v_operand(fv16[:TP].reshape((BP, LP) + hd))
        o_f = _fox_prompt_attention(q_p, k_p, v_p, TP + TS)

        lf_all = jnp.concatenate([cache_fox_logf[l].astype(F32), logf[TP:].reshape(BS, LS, fox_heads),
                                  jnp.zeros((BS, LANES - LS, fox_heads), F32)], axis=1)
        ct_s = (_cumsum_time(lf_all) * LOG2E).transpose(0, 2, 1)
        ck_s = ct_s.reshape(BS * n_pairs, FOX_HEADS_PER_STEP, past + LANES)
        cq_s = ct_s[:, :, past:past + LS].reshape(BS * n_pairs, FOX_HEADS_PER_STEP * LS, 1)
        o_f = _fox_sample_attention(fq16, fk16, fv16, cache_k, cache_v, l, cq_s, ck_s, TP, LS, o_f)

        rw = jnp.concatenate([router_group_w[l], router_expert_w[l],
                              jnp.zeros((D, LANES - N_GROUPS - N_EXPERTS), F32)], axis=1).astype(BF16)
        rb = jnp.concatenate([router_group_b[l], router_expert_b[l],
                              jnp.zeros((LANES - N_GROUPS - N_EXPERTS,), F32)]).reshape(1, LANES)
        x1_32, x1_16, route = _out_proj_router(x32, o_r, o_f, o_c, w_out[l].astype(BF16),
                                               ln1_g[l], ln1_b[l], rw, rb, alpha)

        row_token, row_gate, block_expert, n_used, pos = _moe_dispatch(route)
        xs = jnp.take(x1_16, row_token, axis=0)
        ys = _moe_experts(xs, row_gate, block_expert, n_used, w_gate, w_up, w_down, l)
        f0 = jnp.take(ys, pos[:, 0], axis=0)
        f1 = jnp.take(ys, pos[:, 1], axis=0)
        x32, x16 = _residual_ln(x1_32, f0, f1, ln2_g[l], ln2_b[l], alpha)

        ks.append(fk32)
        vs.append(fv32)
        lfs.append(logf)

    def split(ts, tail):
        a = jnp.stack(ts)
        return (a[:, :TP].reshape((depth, BP, LP) + tail), a[:, TP:].reshape((depth, BS, LS) + tail))

    pk, sk = split(ks, (fox_heads, HEAD_DIM))
    pv, sv = split(vs, (fox_heads, HEAD_DIM))
    plf, slf = split(lfs, (fox_heads,))
    return (x32[:TP].reshape(BP, LP, D), x32[TP:].reshape(BS, LS, D),
            pk, pv, plf, jnp.stack(rs_p), jnp.stack(cs_p),
            sk, sv, slf, jnp.stack(rs_s), jnp.stack(cs_s))
```

```python
import functools

import numpy as np
import jax
import jax.numpy as jnp
from jax import lax
from jax.experimental import pallas as pl
from jax.experimental.pallas import tpu as pltpu

F32 = jnp.float32
BF16 = jnp.bfloat16

HEAD_DIM = 64
CHUNK = 64
CONV_WIDTH = 31
N_GROUPS = 4
EXPERTS_PER_GROUP = 8
N_EXPERTS = N_GROUPS * EXPERTS_PER_GROUP
ROPE_BASE = 10000.0
LN_EPS = 1e-5
NEG_BIG = -1e30

LANES = 128
MOE_BLOCK_ROWS = 256
VMEM_LIMIT = 56 * 1024 * 1024


def _cparams(*sem):
    return pltpu.CompilerParams(dimension_semantics=sem, vmem_limit_bytes=VMEM_LIMIT)


def _row_tile(n, pref):
    t = min(n, pref)
    while n % t:
        t //= 2
    return t


def _ln_rows(y, g, b):
    mu = jnp.mean(y, axis=-1, keepdims=True)
    yc = y - mu
    var = jnp.mean(yc * yc, axis=-1, keepdims=True)
    return yc * lax.rsqrt(var + LN_EPS) * g + b


def _ln_kernel(x_ref, g_ref, b_ref, o32_ref, o16_ref):
    y = _ln_rows(x_ref[...], g_ref[...], b_ref[...])
    o32_ref[...] = y
    o16_ref[...] = y.astype(BF16)


def _layer_norm_in(x, g, b):
    T, D = x.shape
    tm = _row_tile(T, 256)
    row = pl.BlockSpec((tm, D), lambda i: (i, 0))
    vec = pl.BlockSpec((1, D), lambda i: (0, 0))
    return pl.pallas_call(
        _ln_kernel,
        grid=(T // tm,),
        in_specs=[row, vec, vec],
        out_specs=[row, row],
        out_shape=[jax.ShapeDtypeStruct((T, D), F32), jax.ShapeDtypeStruct((T, D), BF16)],
        compiler_params=_cparams("parallel"),
        name="ln_in",
    )(x, g.reshape(1, D), b.reshape(1, D))


def _res_ln_kernel(alpha, x_ref, f0_ref, f1_ref, g_ref, b_ref, o32_ref, o16_ref):
    y = _ln_rows(alpha * x_ref[...] + (f0_ref[...] + f1_ref[...]), g_ref[...], b_ref[...])
    o32_ref[...] = y
    o16_ref[...] = y.astype(BF16)


def _residual_ln(x, f0, f1, g, b, alpha):
    T, D = x.shape
    tm = _row_tile(T, 256)
    row = pl.BlockSpec((tm, D), lambda i: (i, 0))
    vec = pl.BlockSpec((1, D), lambda i: (0, 0))
    return pl.pallas_call(
        functools.partial(_res_ln_kernel, alpha),
        grid=(T // tm,),
        in_specs=[row, row, row, vec, vec],
        out_specs=[row, row],
        out_shape=[jax.ShapeDtypeStruct((T, D), F32), jax.ShapeDtypeStruct((T, D), BF16)],
        compiler_params=_cparams("parallel"),
        name="res_ln",
    )(x, f0, f1, g.reshape(1, D), b.reshape(1, D))


def _mm_kernel(x_ref, w_ref, o_ref):
    o_ref[...] = jnp.dot(x_ref[...], w_ref[...], preferred_element_type=F32).astype(o_ref.dtype)


def _matmul(x, w, out_dtype, name):
    T, K = x.shape
    N = w.shape[1]
    tm = _row_tile(T, 1024)
    tn = _row_tile(N, 1024)
    return pl.pallas_call(
        _mm_kernel,
        grid=(T // tm, N // tn),
        in_specs=[pl.BlockSpec((tm, K), lambda i, j: (i, 0)),
                  pl.BlockSpec((K, tn), lambda i, j: (0, j))],
        out_specs=pl.BlockSpec((tm, tn), lambda i, j: (i, j)),
        out_shape=jax.ShapeDtypeStruct((T, N), out_dtype),
        compiler_params=_cparams("parallel", "parallel"),
        name=name,
    )(x, w)


def _fox_proj_kernel(x_ref, wq_ref, wk_ref, wv_ref, q16_ref, k32_ref, v32_ref, k16_ref, v16_ref):
    x = x_ref[...]
    q16_ref[...] = jnp.dot(x, wq_ref[...], preferred_element_type=F32).astype(BF16)
    k = jnp.dot(x, wk_ref[...], preferred_element_type=F32)
    k32_ref[...] = k
    k16_ref[...] = k.astype(BF16)
    v = jnp.dot(x, wv_ref[...], preferred_element_type=F32)
    v32_ref[...] = v
    v16_ref[...] = v.astype(BF16)


def _fox_proj(x16, w_fox):
    T, D = x16.shape
    W = w_fox.shape[1] // 3
    tm = _row_tile(T, 512)
    tn = _row_tile(W, 512)
    nj = W // tn
    out = pl.BlockSpec((tm, tn), lambda i, j: (i, j))
    return pl.pallas_call(
        _fox_proj_kernel,
        grid=(T // tm, nj),
        in_specs=[pl.BlockSpec((tm, D), lambda i, j: (i, 0)),
                  pl.BlockSpec((D, tn), lambda i, j: (0, j)),
                  pl.BlockSpec((D, tn), lambda i, j: (0, nj + j)),
                  pl.BlockSpec((D, tn), lambda i, j: (0, 2 * nj + j))],
        out_specs=[out, out, out, out, out],
        out_shape=[jax.ShapeDtypeStruct((T, W), BF16), jax.ShapeDtypeStruct((T, W), F32),
                   jax.ShapeDtypeStruct((T, W), F32), jax.ShapeDtypeStruct((T, W), BF16),
                   jax.ShapeDtypeStruct((T, W), BF16)],
        compiler_params=_cparams("parallel", "parallel"),
        name="fox_proj",
    )(x16, w_fox, w_fox, w_fox)


def _log_sigmoid(z):
    return -(jnp.maximum(-z, 0.0) + jnp.log1p(jnp.exp(-jnp.abs(z))))


def _conv_fgt_proj_kernel(cc, nh, x_ref, w_ref, bf_ref, u_ref, lf_ref):
    x = x_ref[...]
    a = jnp.dot(x, w_ref[:, 0:cc], preferred_element_type=F32)
    gt = jnp.dot(x, w_ref[:, cc:2 * cc], preferred_element_type=F32)
    u_ref[...] = a * jax.nn.sigmoid(gt)
    z = jnp.dot(x, w_ref[:, 2 * cc:2 * cc + LANES], preferred_element_type=F32) + bf_ref[...]
    lf_ref[...] = _log_sigmoid(z)[:, :nh]


def _conv_fgt_proj(x16, w_cf, b_forget, cc, nh):
    T, D = x16.shape
    tm = _row_tile(T, 512)
    nw = w_cf.shape[1]
    bf = jnp.zeros((1, LANES), F32).at[0, :nh].set(b_forget)
    return pl.pallas_call(
        functools.partial(_conv_fgt_proj_kernel, cc, nh),
        grid=(T // tm,),
        in_specs=[pl.BlockSpec((tm, D), lambda i: (i, 0)),
                  pl.BlockSpec((D, nw), lambda i: (0, 0)),
                  pl.BlockSpec((1, LANES), lambda i: (0, 0))],
        out_specs=[pl.BlockSpec((tm, cc), lambda i: (i, 0)),
                   pl.BlockSpec((tm, nh), lambda i: (i, 0))],
        out_shape=[jax.ShapeDtypeStruct((T, cc), F32), jax.ShapeDtypeStruct((T, nh), F32)],
        compiler_params=_cparams("parallel"),
        name="conv_fgt_proj",
    )(x16, w_cf, bf)


def _trunc16(x):
    bits = lax.bitcast_convert_type(x, jnp.uint32) & jnp.uint32(0xFFFF0000)
    return lax.bitcast_convert_type(bits, F32)


def _split3(x):
    hi = _trunc16(x)
    r1 = x - hi
    mid = _trunc16(r1)
    lo = r1 - mid
    return hi.astype(BF16), mid.astype(BF16), lo.astype(BF16)


def _cumsum_kernel(nblk, x_ref, o_ref):
    r = lax.broadcasted_iota(jnp.int32, (LANES, LANES), 0)
    c = lax.broadcasted_iota(jnp.int32, (LANES, LANES), 1)
    tri = (r <= c).astype(BF16)

    def body(j, carry):
        hi, mid, lo = _split3(x_ref[0, j])
        s = (jnp.dot(hi, tri, preferred_element_type=F32)
             + jnp.dot(mid, tri, preferred_element_type=F32)
             + jnp.dot(lo, tri, preferred_element_type=F32)) + carry
        o_ref[0, j] = s
        return s[:, LANES - 1:LANES]

    lax.fori_loop(0, nblk, body, jnp.zeros((x_ref.shape[2], 1), F32))


def _cumsum_time(x):
    n, L, H = x.shape
    assert L % LANES == 0
    nblk = L // LANES
    xt = x.reshape(n, nblk, LANES, H).transpose(0, 1, 3, 2)
    blk = pl.BlockSpec((1, nblk, H, LANES), lambda i: (i, 0, 0, 0))
    out = pl.pallas_call(
        functools.partial(_cumsum_kernel, nblk),
        grid=(n,),
        in_specs=[blk],
        out_specs=blk,
        out_shape=jax.ShapeDtypeStruct((n, nblk, H, LANES), F32),
        compiler_params=_cparams("parallel"),
        name="cumsum_logf",
    )(xt)
    return out.transpose(0, 1, 3, 2).reshape(n, L, H)


RET_GROUP_HEADS = 4
RET_GROUP_LANES = RET_GROUP_HEADS * HEAD_DIM
RET_CHUNKS_PER_STEP = 8


def _ret_perm(n_heads):
    half = HEAD_DIM // 2
    perm = np.zeros(n_heads * HEAD_DIM, np.int32)
    for h in range(n_heads):
        g, hh = divmod(h, RET_GROUP_HEADS)
        for hf in range(2):
            for j in range(half):
                perm[g * RET_GROUP_LANES + hf * LANES + hh * half + j] = h * HEAD_DIM + hf * half + j
    return perm


def _retention_kernel(n_groups, cps, x_ref, cos_ref, sin_ref, inner_ref, qdec_ref, kdec_ref, cdec_ref,
                      s0_ref, o_ref, sout_ref, s_scr, o_scr):
    W = n_groups * RET_GROUP_LANES
    c = CHUNK
    step = pl.program_id(0)
    n_chunks = x_ref.shape[0] // c

    k_lane = lax.broadcasted_iota(jnp.int32, (c, RET_GROUP_LANES), 1)
    k_head = (k_lane % LANES) // (HEAD_DIM // 2)
    v_head = k_lane // HEAD_DIM
    sr = lax.broadcasted_iota(jnp.int32, (RET_GROUP_LANES, RET_GROUP_LANES), 0)
    sc = lax.broadcasted_iota(jnp.int32, (RET_GROUP_LANES, RET_GROUP_LANES), 1)
    s_mask = ((sr % LANES) // (HEAD_DIM // 2)) == (sc // HEAD_DIM)

    for ci in range(n_chunks):
        rows = slice(ci * c, (ci + 1) * c)
        cos = cos_ref[rows, :]
        sin = sin_ref[rows, :]
        if cps == 1:
            seq_local, first, last = ci, True, True
        else:
            seq_local = 0
            first = (step * n_chunks + ci) % cps == 0
            last = (step * n_chunks + ci) % cps == cps - 1
        for g in range(n_groups):
            lo = g * RET_GROUP_LANES

            def rot(t):
                t1, t2 = t[:, :LANES], t[:, LANES:]
                return jnp.concatenate([t1 * cos - t2 * sin, t1 * sin + t2 * cos], axis=1)

            q = rot(x_ref[rows, lo:lo + RET_GROUP_LANES])
            k = rot(x_ref[rows, W + lo:W + lo + RET_GROUP_LANES])
            v = x_ref[rows, 2 * W + lo:2 * W + lo + RET_GROUP_LANES]
            q16 = q.astype(BF16)
            k16 = k.astype(BF16)
            v16 = v.astype(BF16)
            kd16 = (k * kdec_ref[g]).astype(BF16)

            if cps == 1:
                s_prev = s0_ref[seq_local, g]
            else:
                @pl.when(first)
                def _():
                    s_scr[g] = s0_ref[0, g]
                s_prev = s_scr[g]

            zero = jnp.zeros_like(k16)
            k_bd = jnp.concatenate([jnp.where(k_head == hh, k16, zero) for hh in range(RET_GROUP_HEADS)], axis=0)
            v_bd = jnp.concatenate([jnp.where(v_head == hh, v16, zero) for hh in range(RET_GROUP_HEADS)], axis=0)
            s_all = lax.dot_general(q16, k_bd, (((1,), (1,)), ((), ())), preferred_element_type=F32)
            att = (s_all * inner_ref[g]).astype(BF16)
            o_intra = jnp.dot(att, v_bd, preferred_element_type=F32)
            o_cross = jnp.dot(q16, s_prev.astype(BF16), preferred_element_type=F32) * qdec_ref[g]
            o_scr[rows, lo:lo + RET_GROUP_LANES] = o_intra + o_cross

            kv = lax.dot_general(kd16, v16, (((0,), (0,)), ((), ())), preferred_element_type=F32)
            s_new = s_prev * cdec_ref[g] + jnp.where(s_mask, kv, 0.0)
            if cps == 1:
                sout_ref[seq_local, g] = s_new
            else:
                s_scr[g] = s_new

                @pl.when(last)
                def _():
                    sout_ref[0, g] = s_new

    pr = lax.broadcasted_iota(jnp.int32, (W, W), 0) // HEAD_DIM
    pc = lax.broadcasted_iota(jnp.int32, (W, W), 1) // HEAD_DIM
    avg = jnp.where(pr == pc, 1.0 / HEAD_DIM, 0.0).astype(BF16)

    def head_mean(t):
        hi, mid, lo3 = _split3(t)
        return (jnp.dot(hi, avg, preferred_element_type=F32)
                + jnp.dot(mid, avg, preferred_element_type=F32)
                + jnp.dot(lo3, avg, preferred_element_type=F32))

    o = o_scr[...]
    oc = o - head_mean(o)
    var = head_mean(oc * oc)
    gate = x_ref[:, 3 * W:4 * W]
    o_ref[...] = (oc * lax.rsqrt(var + LN_EPS) * (gate * jax.nn.sigmoid(gate))).astype(BF16)


def _retention_tables(n_heads):
    c = CHUNK
    n_groups = n_heads // RET_GROUP_HEADS
    lg = jnp.log1p(-jnp.exp2(-5.0 - jnp.arange(n_heads, dtype=F32)))
    idx = jnp.arange(c, dtype=F32)
    diff = idx[:, None] - idx[None, :]
    causal = diff >= 0
    inner = jnp.where(causal[None], jnp.exp(jnp.where(causal, diff, 0.0)[None] * lg[:, None, None]), 0.0)
    q_dec = jnp.exp((idx + 1.0)[:, None] * lg[None, :])
    k_dec = jnp.exp((c - 1.0 - idx)[:, None] * lg[None, :])
    c_dec = jnp.exp(c * lg)
    inner_t = inner.reshape(n_groups, RET_GROUP_HEADS, c, c).transpose(0, 2, 1, 3).reshape(n_groups, c, RET_GROUP_HEADS * c)
    out_head = np.arange(RET_GROUP_LANES) // HEAD_DIM
    k_head = (np.arange(RET_GROUP_LANES) % LANES) // (HEAD_DIM // 2)
    gh = np.arange(n_groups)[:, None] * RET_GROUP_HEADS
    qdec_t = q_dec[:, gh + out_head[None, :]].transpose(1, 0, 2)
    kdec_t = k_dec[:, gh + k_head[None, :]].transpose(1, 0, 2)
    cdec_t = c_dec[gh + out_head[None, :]][:, None, :]
    return inner_t, qdec_t, kdec_t, cdec_t


def _rotary_tables(pos):
    half = HEAD_DIM // 2
    inv = ROPE_BASE ** (-jnp.arange(half, dtype=F32) / half)
    ang = pos.astype(F32)[:, None] * inv[None, :]
    return (jnp.tile(jnp.cos(ang), (1, LANES // half)), jnp.tile(jnp.sin(ang), (1, LANES // half)))


def _skip_ref(kern, idx, *refs):
    return kern(*refs[:idx], *refs[idx + 1:])


def _with_row_alias(kern, in_specs, args, o_prev):
    if o_prev is None:
        return kern, in_specs, args, {}
    idx = len(args)
    return (functools.partial(_skip_ref, kern, idx), in_specs + [pl.BlockSpec(memory_space=pl.ANY)],
            args + [o_prev], {idx: 0})


def _retention(xr, cos, sin, s0g, row0, n_rows, seq_len, n_heads, o_prev=None):
    T = xr.shape[0]
    W = n_heads * HEAD_DIM
    n_groups = n_heads // RET_GROUP_HEADS
    nseq = n_rows // seq_len
    cps = seq_len // CHUNK
    tl = RET_CHUNKS_PER_STEP * CHUNK
    assert n_rows % tl == 0 and row0 % tl == 0 and (cps == 1 or cps % RET_CHUNKS_PER_STEP == 0)
    off = row0 // tl
    if cps == 1:
        sblk = pl.BlockSpec((RET_CHUNKS_PER_STEP, n_groups, RET_GROUP_LANES, RET_GROUP_LANES),
                            lambda i: (i, 0, 0, 0))
    else:
        spc = cps // RET_CHUNKS_PER_STEP
        sblk = pl.BlockSpec((1, n_groups, RET_GROUP_LANES, RET_GROUP_LANES), lambda i: (i // spc, 0, 0, 0))
    inner_t, qdec_t, kdec_t, cdec_t = _retention_tables(n_heads)

    def full(a):
        return pl.BlockSpec(a.shape, lambda i: (0,) * a.ndim)

    kern = functools.partial(_retention_kernel, n_groups, cps)
    in_specs = [pl.BlockSpec((tl, 4 * W), lambda i: (i + off, 0)),
                pl.BlockSpec((tl, LANES), lambda i: (i + off, 0)),
                pl.BlockSpec((tl, LANES), lambda i: (i + off, 0)),
                full(inner_t), full(qdec_t), full(kdec_t), full(cdec_t), sblk]
    args = [xr, cos, sin, inner_t, qdec_t, kdec_t, cdec_t, s0g]
    kern, in_specs, args, aliases = _with_row_alias(kern, in_specs, args, o_prev)
    return pl.pallas_call(
        kern,
        grid=(n_rows // tl,),
        in_specs=in_specs,
        out_specs=[pl.BlockSpec((tl, W), lambda i: (i + off, 0)), sblk],
        out_shape=[jax.ShapeDtypeStruct((T, W), BF16),
                   jax.ShapeDtypeStruct((nseq, n_groups, RET_GROUP_LANES, RET_GROUP_LANES), F32)],
        scratch_shapes=[pltpu.VMEM((n_groups, RET_GROUP_LANES, RET_GROUP_LANES), F32),
                        pltpu.VMEM((tl, W), F32)],
        input_output_aliases=aliases,
        compiler_params=_cparams("arbitrary"),
        name="retention",
    )(*args)


def _ret_state_to_groups(s):
    n, H = s.shape[:2]
    G = H // RET_GROUP_HEADS
    half = HEAD_DIM // 2
    t = s.reshape(n, G, RET_GROUP_HEADS, 2, half, HEAD_DIM).transpose(0, 1, 3, 2, 4, 5)
    eye = jnp.eye(RET_GROUP_HEADS, dtype=s.dtype)
    t = t[:, :, :, :, :, None, :] * eye[None, None, None, :, None, :, None]
    return t.reshape(n, G, RET_GROUP_LANES, RET_GROUP_LANES)


def _ret_state_from_groups(sg):
    n, G = sg.shape[:2]
    half = HEAD_DIM // 2
    t = sg.reshape(n, G, 2, RET_GROUP_HEADS, half, RET_GROUP_HEADS, HEAD_DIM)
    t = jnp.stack([t[:, :, :, hh, :, hh, :] for hh in range(RET_GROUP_HEADS)], axis=2)
    return t.reshape(n, G * RET_GROUP_HEADS, HEAD_DIM, HEAD_DIM)


CONV_HALO = 32


def _conv_kernel(tl, u_ref, prev_ref, w_ref, cb_ref, g_ref, b_ref, o_ref, buf):
    j = pl.program_id(1)

    @pl.when(j == 0)
    def _():
        buf[0:CONV_HALO, :] = prev_ref[0]

    @pl.when(j > 0)
    def _():
        buf[0:CONV_HALO, :] = buf[tl:tl + CONV_HALO, :]

    buf[CONV_HALO:CONV_HALO + tl, :] = u_ref[...]
    off = CONV_HALO - (CONV_WIDTH - 1)
    acc = buf[off:off + tl, :] * w_ref[0:1, :]
    for w in range(1, CONV_WIDTH):
        acc = acc + buf[off + w:off + w + tl, :] * w_ref[w:w + 1, :]
    y = _ln_rows(acc + cb_ref[...], g_ref[...], b_ref[...])
    o_ref[...] = (y * jax.nn.sigmoid(y)).astype(BF16)


def _conformer_conv(u, prev, w, cb, g, b, row0, seq_len, o_prev=None):
    T, C = u.shape
    nseq = prev.shape[0]
    tl = _row_tile(seq_len, 512)
    assert tl >= CONV_HALO and row0 % tl == 0
    off = row0 // tl
    nt = seq_len // tl
    prev_p = jnp.pad(prev.astype(F32), ((0, 0), (CONV_HALO - (CONV_WIDTH - 1), 0), (0, 0)))
    w_p = jnp.pad(w, ((0, CONV_HALO - CONV_WIDTH), (0, 0)))
    vec = pl.BlockSpec((1, C), lambda s, j: (0, 0))
    rows = pl.BlockSpec((tl, C), lambda s, j: (off + s * nt + j, 0))
    in_specs = [rows,
                pl.BlockSpec((1, CONV_HALO, C), lambda s, j: (s, 0, 0)),
                pl.BlockSpec((CONV_HALO, C), lambda s, j: (0, 0)),
                vec, vec, vec]
    args = [u, prev_p, w_p, cb.reshape(1, C), g.reshape(1, C), b.reshape(1, C)]
    kern, in_specs, args, aliases = _with_row_alias(functools.partial(_conv_kernel, tl), in_specs, args, o_prev)
    return pl.pallas_call(
        kern,
        grid=(nseq, nt),
        in_specs=in_specs,
        out_specs=rows,
        out_shape=jax.ShapeDtypeStruct((T, C), BF16),
        scratch_shapes=[pltpu.VMEM((tl + CONV_HALO, C), F32)],
        input_output_aliases=aliases,
        compiler_params=_cparams("arbitrary", "arbitrary"),
        name="conformer_conv",
    )(*args)


FOX_HEADS_PER_STEP = 2
LOG2E = 1.4426950408889634
FOX_SKIP_LOG2 = -160.0


def _split3_f32(x):
    hi = _trunc16(x)
    r1 = x - hi
    mid = _trunc16(r1)
    return hi, mid, r1 - mid


def _fox_prompt_kernel(bq, nkb, jlo_ref, q_ref, k_ref, v_ref, cq_ref, ck_ref, o_ref,
                       kt_scr, m_scr, l_scr, acc_scr):
    pair = pl.program_id(0)
    i = pl.program_id(1)
    heads = range(FOX_HEADS_PER_STEP)

    def key_rows(j):
        return pl.ds(pl.multiple_of(j * bq, bq), bq)

    @pl.when(i == 0)
    def _():
        row = lax.broadcasted_iota(jnp.int32, (LANES, bq), 0)
        ones = ((row < 3) | ((row >= 6) & (row < 9))).astype(F32)

        def build(j, carry):
            kt_scr[j, 0:LANES, :] = k_ref[key_rows(j), :].T
            ck = ck_ref[j]
            bias = ones
            for h in heads:
                terms = _split3_f32(-ck[h:h + 1, :])
                for t in range(3):
                    bias = jnp.where(row == 6 * h + 3 + t, terms[t], bias)
            kt_scr[j, LANES:2 * LANES, :] = bias.astype(BF16)
            return carry

        lax.fori_loop(0, nkb, build, 0)

    q = q_ref[...]
    lane = lax.broadcasted_iota(jnp.int32, (bq, LANES), 1)
    cq = cq_ref[...]
    qs = []
    for h in heads:
        terms = _split3_f32(cq[:, h:h + 1])
        bias = ((lane >= 6 * h + 3) & (lane < 6 * h + 6)).astype(F32)
        for t in range(3):
            bias = jnp.where(lane == 6 * h + t, terms[t], bias)
        in_head = (lane < HEAD_DIM) if h == 0 else (lane >= HEAD_DIM)
        qs.append(jnp.concatenate([jnp.where(in_head, q, jnp.zeros_like(q)), bias.astype(BF16)], axis=1))
        m_scr[h] = jnp.full((bq, LANES), NEG_BIG, F32)
        l_scr[h] = jnp.zeros((bq, LANES), F32)
        acc_scr[h] = jnp.zeros((bq, LANES), F32)

    def update(h, s, v):
        m_prev = m_scr[h]
        m_new = jnp.maximum(m_prev, jnp.max(s, axis=1, keepdims=True))
        p = jnp.exp2(s - pltpu.repeat(m_new, bq // LANES, axis=1))
        alpha = jnp.exp2(m_prev - m_new)
        l_scr[h] = alpha * l_scr[h] + jnp.sum(p, axis=1, keepdims=True)
        acc_scr[h] = alpha * acc_scr[h] + jnp.dot(p.astype(BF16), v, preferred_element_type=F32)
        m_scr[h] = m_new

    def scores(h, j):
        return jnp.dot(qs[h], kt_scr[j], preferred_element_type=F32)

    def body(j, carry):
        s = [scores(h, j) for h in heads]
        v = v_ref[key_rows(j), :]
        for h in heads:
            update(h, s[h], v)
        return carry

    lax.fori_loop(jlo_ref[pair * nkb + i], i, body, 0)

    r = lax.broadcasted_iota(jnp.int32, (bq, bq), 0)
    c = lax.broadcasted_iota(jnp.int32, (bq, bq), 1)
    s = [scores(h, i) for h in heads]
    v = v_ref[key_rows(i), :]
    for h in heads:
        update(h, jnp.where(c <= r, s[h], NEG_BIG), v)

    o_ref[...] = jnp.where(lane < HEAD_DIM, acc_scr[0] / l_scr[0], acc_scr[1] / l_scr[1]).astype(BF16)


def _fox_skip(q16, k16, ct, n_heads, bq):
    L = ct.shape[0]
    nb = L // bq

    def block_norm(t):
        n = jnp.sqrt(jnp.sum(jnp.square(t[:L].astype(F32).reshape(L, n_heads, HEAD_DIM)), axis=-1))
        return jnp.max(n.reshape(nb, bq, n_heads), axis=1)

    qn, kn = block_norm(q16), block_norm(k16)
    kmax = jnp.max(kn, axis=0, keepdims=True)
    ctb = ct.reshape(nb, bq, n_heads)
    bound = (1.02 * qn * (kmax + kn) + ctb[:, 0, :])[:, None, :] - ctb[None, :, bq - 1, :]
    skip = (bound < FOX_SKIP_LOG2) & (jnp.arange(nb)[None, :, None] < jnp.arange(nb)[:, None, None])
    jlo = jnp.sum(jnp.cumprod(skip.astype(jnp.int32), axis=1), axis=1)
    jlo = jnp.min(jlo.reshape(nb, n_heads // FOX_HEADS_PER_STEP, FOX_HEADS_PER_STEP), axis=2)
    return jlo.T.reshape(-1).astype(jnp.int32)


def _fox_prompt_attention(q16, k16, v16, ct, L, bq):
    T, W = q16.shape
    n_heads = W // HEAD_DIM
    hp = FOX_HEADS_PER_STEP
    n_pairs = n_heads // hp
    nkb = L // bq
    jlo = _fox_skip(q16, k16, ct, n_heads, bq)
    cq = ct.reshape(L, n_pairs, hp).transpose(1, 0, 2)
    ck = ct.reshape(nkb, bq, n_pairs, hp).transpose(2, 0, 3, 1)
    grid_spec = pltpu.PrefetchScalarGridSpec(
        num_scalar_prefetch=1,
        grid=(n_pairs, nkb),
        in_specs=[pl.BlockSpec((bq, LANES), lambda p, i, jl: (i, p)),
                  pl.BlockSpec((L, LANES), lambda p, i, jl: (0, p)),
                  pl.BlockSpec((L, LANES), lambda p, i, jl: (0, p)),
                  pl.BlockSpec((None, bq, hp), lambda p, i, jl: (p, i, 0)),
                  pl.BlockSpec((None, nkb, hp, bq), lambda p, i, jl: (p, 0, 0, 0))],
        out_specs=pl.BlockSpec((bq, LANES), lambda p, i, jl: (i, p)),
        scratch_shapes=[pltpu.VMEM((nkb, 2 * LANES, bq), BF16),
                        pltpu.VMEM((hp, bq, LANES), F32), pltpu.VMEM((hp, bq, LANES), F32),
                        pltpu.VMEM((hp, bq, LANES), F32)],
    )
    return pl.pallas_call(
        functools.partial(_fox_prompt_kernel, bq, nkb),
        grid_spec=grid_spec,
        out_shape=jax.ShapeDtypeStruct((T, W), BF16),
        compiler_params=_cparams("arbitrary", "arbitrary"),
        name="fox_prompt",
    )(jlo, q16, k16, v16, cq, ck)


def _fox_sample_kernel(past, ls, q_ref, kn_ref, vn_ref, kc_ref, vc_ref, cq_ref, ck_ref, o_ref):
    q = q_ref[...]
    lane = lax.broadcasted_iota(jnp.int32, q.shape, 1)
    zero = jnp.zeros_like(q)
    q2 = jnp.concatenate([jnp.where(lane < HEAD_DIM, q, zero), jnp.where(lane >= HEAD_DIM, q, zero)], axis=0)
    pad = jnp.zeros((LANES - ls, LANES), BF16)
    k_all = jnp.concatenate([kc_ref[0].astype(BF16), kn_ref[...], pad], axis=0)
    v_all = jnp.concatenate([vc_ref[0].astype(BF16), vn_ref[...], pad], axis=0)
    nk = past + LANES
    s = lax.dot_general(q2, k_all, (((1,), (1,)), ((), ())), preferred_element_type=F32)
    row = lax.broadcasted_iota(jnp.int32, (2 * ls, nk), 0)
    col = lax.broadcasted_iota(jnp.int32, (2 * ls, nk), 1)
    ck = jnp.where(row < ls, ck_ref[0, 0:1, :], ck_ref[0, 1:2, :])
    visible = (col < past) | (col - past <= row % ls)
    s = jnp.where(visible, s + cq_ref[0] - ck, NEG_BIG)
    p = jnp.exp2(s - jnp.max(s, axis=1, keepdims=True))
    den = jnp.sum(p, axis=1, keepdims=True)
    o2 = jnp.dot(p.astype(BF16), v_all, preferred_element_type=F32) / den
    o_ref[...] = jnp.where(lane < HEAD_DIM, o2[0:ls], o2[ls:2 * ls]).astype(BF16)


def _fox_sample_attention(q16, k16, v16, cache_k, cache_v, layer, cq, ck, row0, ls, o_prev):
    T, W = q16.shape
    past = cache_k.shape[1]
    n_pairs = W // LANES
    B = cq.shape[0] // n_pairs
    assert ls <= LANES and row0 % ls == 0
    off = row0 // ls
    rows = pl.BlockSpec((ls, LANES), lambda b, p: (off + b, p))
    cache = pl.BlockSpec((1, past, LANES), lambda b, p: (layer * B + b, 0, p))
    in_specs = [rows, rows, rows, cache, cache,
                pl.BlockSpec((1, 2 * ls, 1), lambda b, p: (b * n_pairs + p, 0, 0)),
                pl.BlockSpec((1, 2, past + LANES), lambda b, p: (b * n_pairs + p, 0, 0))]
    args = [q16, k16, v16, cache_k, cache_v, cq, ck]
    kern, in_specs, args, aliases = _with_row_alias(
        functools.partial(_fox_sample_kernel, past, ls), in_specs, args, o_prev)
    return pl.pallas_call(
        kern,
        grid=(B, n_pairs),
        in_specs=in_specs,
        out_specs=rows,
        out_shape=jax.ShapeDtypeStruct((T, W), BF16),
        input_output_aliases=aliases,
        compiler_params=_cparams("parallel", "parallel"),
        name="fox_sample",
    )(*args)


def _out_router_kernel(alpha, x_ref, r_ref, f_ref, c_ref, w_ref, g_ref, b_ref, rw_ref, rb_ref,
                       o32_ref, o16_ref, route_ref):
    mixed = jnp.concatenate([r_ref[...], f_ref[...], c_ref[...]], axis=1)
    m = jnp.dot(mixed, w_ref[...], preferred_element_type=F32)
    y = _ln_rows(alpha * x_ref[...] + m, g_ref[...], b_ref[...])
    y16 = y.astype(BF16)
    o32_ref[...] = y
    o16_ref[...] = y16

    logits = jnp.dot(y16, rw_ref[...], preferred_element_type=F32) + rb_ref[...]
    lane = lax.broadcasted_iota(jnp.int32, logits.shape, 1).astype(F32)
    far = float(LANES)

    def rmax(t):
        return jnp.max(t, axis=1, keepdims=True)

    def rmin(t):
        return jnp.min(t, axis=1, keepdims=True)

    def rsum(t):
        return jnp.sum(t, axis=1, keepdims=True)

    gmask = lane < N_GROUPS
    lg = jnp.where(gmask, logits, -jnp.inf)
    mg = rmax(lg)
    p_grp = 1.0 / rsum(jnp.exp(lg - mg))
    g_idx = rmin(jnp.where(lg == mg, lane, far))
    e_lo = N_GROUPS + EXPERTS_PER_GROUP * g_idx
    emask = (lane >= e_lo) & (lane < e_lo + EXPERTS_PER_GROUP)
    le = jnp.where(emask, logits, -jnp.inf)
    ee = jnp.exp(le - rmax(le))
    pin = jnp.where(emask, ee / rsum(ee), -1.0)
    p1 = rmax(pin)
    i1 = rmin(jnp.where(pin == p1, lane, far))
    pin2 = jnp.where(lane == i1, -1.0, pin)
    p2 = rmax(pin2)
    i2 = rmin(jnp.where(pin2 == p2, lane, far))
    den = p1 + p2
    route_ref[...] = jnp.where(lane == 0, i1 - N_GROUPS,
                     jnp.where(lane == 1, i2 - N_GROUPS,
                     jnp.where(lane == 2, p_grp * p1 / den,
                     jnp.where(lane == 3, p_grp * p2 / den, 0.0))))


def _out_proj_router(x, o_r, o_f, o_c, w_out16, g, b, rw16, rb, alpha):
    T, D = x.shape
    tm = _row_tile(T, 256)

    def rows(a):
        return pl.BlockSpec((tm, a.shape[1]), lambda i: (i, 0))

    def full(a):
        return pl.BlockSpec(a.shape, lambda i: (0, 0))

    g2, b2 = g.reshape(1, D), b.reshape(1, D)
    return pl.pallas_call(
        functools.partial(_out_router_kernel, alpha),
        grid=(T // tm,),
        in_specs=[rows(x), rows(o_r), rows(o_f), rows(o_c), full(w_out16), full(g2), full(b2),
                  full(rw16), full(rb)],
        out_specs=[pl.BlockSpec((tm, D), lambda i: (i, 0)), pl.BlockSpec((tm, D), lambda i: (i, 0)),
                   pl.BlockSpec((tm, LANES), lambda i: (i, 0))],
        out_shape=[jax.ShapeDtypeStruct((T, D), F32), jax.ShapeDtypeStruct((T, D), BF16),
                   jax.ShapeDtypeStruct((T, LANES), F32)],
        compiler_params=_cparams("parallel"),
        name="out_proj_router",
    )(x, o_r, o_f, o_c, w_out16, g2, b2, rw16, rb)


def _moe_kernel(be_ref, nused_ref, x_ref, gate_ref, wg_ref, wu_ref, wd_ref, y_ref, wg16, wu16, wd16):
    blk = pl.program_id(0)
    used = blk < nused_ref[0]
    prev_expert = be_ref[jnp.maximum(blk - 1, 0)]

    @pl.when(used & ((blk == 0) | (be_ref[blk] != prev_expert)))
    def _():
        wg16[...] = wg_ref[0].astype(BF16)
        wu16[...] = wu_ref[0].astype(BF16)
        wd16[...] = wd_ref[0].astype(BF16)

    @pl.when(used)
    def _():
        x = x_ref[...]
        a = jnp.dot(x, wg16[...], preferred_element_type=F32)
        u = jnp.dot(x, wu16[...], preferred_element_type=F32)
        h = (a * jax.nn.sigmoid(a) * u).astype(BF16)
        y_ref[...] = jnp.dot(h, wd16[...], preferred_element_type=F32) * gate_ref[...]

    @pl.when(jnp.logical_not(used))
    def _():
        y_ref[...] = jnp.zeros_like(y_ref)


def _moe_experts(xs, row_gate, block_expert, n_used, wg, wu, wd, layer):
    R, D = xs.shape
    DE = wg.shape[3]
    bm = MOE_BLOCK_ROWS
    grid_spec = pltpu.PrefetchScalarGridSpec(
        num_scalar_prefetch=2,
        grid=(R // bm,),
        in_specs=[pl.BlockSpec((bm, D), lambda i, be, nu: (i, 0)),
                  pl.BlockSpec((bm, 1), lambda i, be, nu: (i, 0)),
                  pl.BlockSpec((None, 1, D, DE), lambda i, be, nu: (layer, be[i], 0, 0)),
                  pl.BlockSpec((None, 1, D, DE), lambda i, be, nu: (layer, be[i], 0, 0)),
                  pl.BlockSpec((None, 1, DE, D), lambda i, be, nu: (layer, be[i], 0, 0))],
        out_specs=pl.BlockSpec((bm, D), lambda i, be, nu: (i, 0)),
        scratch_shapes=[pltpu.VMEM((D, DE), BF16), pltpu.VMEM((D, DE), BF16), pltpu.VMEM((DE, D), BF16)],
    )
    return pl.pallas_call(
        _moe_kernel,
        grid_spec=grid_spec,
        out_shape=jax.ShapeDtypeStruct((R, D), F32),
        compiler_params=_cparams("arbitrary"),
        name="moe_experts",
    )(block_expert, n_used, xs, row_gate, wg, wu, wd)


def _moe_dispatch(route):
    T = route.shape[0]
    bm = MOE_BLOCK_ROWS
    eid = route[:, 0:2].astype(jnp.int32).reshape(-1)
    gate = route[:, 2:4].reshape(-1)
    n_assign = 2 * T
    n_rows = n_assign + N_EXPERTS * bm
    n_blocks = n_rows // bm
    counts = jnp.sum((jnp.arange(N_EXPERTS)[:, None] == eid[None, :]).astype(jnp.int32), axis=1)
    padded = ((counts + bm - 1) // bm) * bm
    ends = jnp.cumsum(padded)
    starts = ends - padded
    first = jnp.cumsum(counts) - counts
    order = jnp.argsort(eid, stable=True).astype(jnp.int32)
    rank = jnp.argsort(order).astype(jnp.int32)
    pos = rank + (starts - first)[eid]
    n_used = ends[-1] // bm
    blk_start = jnp.arange(n_blocks, dtype=jnp.int32) * bm
    block_expert = jnp.sum((ends[None, :] <= blk_start[:, None]).astype(jnp.int32), axis=1)
    block_expert = jnp.minimum(block_expert, N_EXPERTS - 1)
    last_used = block_expert[jnp.maximum(n_used - 1, 0)]
    block_expert = jnp.where(jnp.arange(n_blocks) < n_used, block_expert, last_used)
    r = jnp.arange(n_rows, dtype=jnp.int32)
    e_r = block_expert[r // bm]
    s_r = r - starts[e_r] + first[e_r]
    valid = (r < ends[-1]) & (s_r < first[e_r] + counts[e_r])
    a_r = order[jnp.clip(s_r, 0, n_assign - 1)]
    row_token = jnp.where(valid, a_r // 2, 0)
    row_gate = jnp.where(valid, gate[a_r], 0.0)
    return row_token, row_gate.reshape(n_rows, 1), block_expert, n_used.reshape(1).astype(jnp.int32), pos.reshape(T, 2)


def kernel(x_prompt, x_sample, cache_fox_k, cache_fox_v, cache_fox_logf, state_ret, state_conv, ln_in_g, ln_in_b, w_in, b_forget, conv_w, conv_b, conv_ln_g, conv_ln_b, w_out, ln1_g, ln1_b, router_group_w, router_group_b, router_expert_w, router_expert_b, w_gate, w_up, w_down, ln2_g, ln2_b):
    BP, LP, D = x_prompt.shape
    BS, LS, _ = x_sample.shape
    depth = w_in.shape[0]
    past = cache_fox_k.shape[2]
    fox_heads = cache_fox_k.shape[3]
    ret_heads = state_ret.shape[2]
    conv_ch = state_conv.shape[3]
    ret_w = ret_heads * HEAD_DIM
    fox_w = fox_heads * HEAD_DIM
    off_fox = 4 * ret_w
    off_fgt = off_fox + 3 * fox_w
    off_conv = off_fgt + fox_heads
    alpha = (2 * depth) ** 0.25
    TP, TS = BP * LP, BS * LS
    assert BP == 1 and LS == CHUNK and past % 512 == 0

    x_raw = jnp.concatenate([x_prompt.reshape(TP, D), x_sample.reshape(TS, D)], axis=0)
    x32, x16 = _layer_norm_in(x_raw, ln_in_g, ln_in_b)

    perm = _ret_perm(ret_heads)
    pos_all = jnp.concatenate([jnp.arange(LP), past + jnp.tile(jnp.arange(LS), BS)])
    cos_t, sin_t = _rotary_tables(pos_all)
    fox_bq = _row_tile(LP, 512)
    hd = (fox_heads, HEAD_DIM)
    n_pairs = fox_heads // FOX_HEADS_PER_STEP
    cache_k = cache_fox_k.reshape(depth * BS, past, fox_w)
    cache_v = cache_fox_v.reshape(depth * BS, past, fox_w)
    qk_scale = HEAD_DIM ** -0.5
    zero_state = jnp.zeros((BP, ret_heads // RET_GROUP_HEADS, RET_GROUP_LANES, RET_GROUP_LANES), F32)
    zero_conv = jnp.zeros((BP, CONV_WIDTH - 1, conv_ch), F32)

    ks, vs, lfs, rs_p, rs_s, cs_p, cs_s = [], [], [], [], [], [], []
    for l in range(depth):
        wl = w_in[l]
        w_ret = jnp.concatenate([wl[:, 0:ret_w][:, perm],
                                 (wl[:, ret_w:2 * ret_w] * qk_scale)[:, perm],
                                 wl[:, 2 * ret_w:4 * ret_w]], axis=1).astype(BF16)
        w_fox = jnp.concatenate([wl[:, off_fox:off_fox + fox_w] * (qk_scale * LOG2E),
                                 wl[:, off_fox + fox_w:off_fgt]], axis=1).astype(BF16)
        w_cf = jnp.concatenate([wl[:, off_conv:], wl[:, off_fgt:off_conv],
                                jnp.zeros((D, LANES - fox_heads), F32)], axis=1).astype(BF16)

        xr = _matmul(x16, w_ret, F32, "ret_proj")
        fq16, fk32, fv32, fk16, fv16 = _fox_proj(x16, w_fox)
        u, logf = _conv_fgt_proj(x16, w_cf, b_forget[l], conv_ch, fox_heads)

        o_r, sg_p = _retention(xr, cos_t, sin_t, zero_state, 0, TP, LP, ret_heads)
        o_r, sg_s = _retention(xr, cos_t, sin_t, _ret_state_to_groups(state_ret[l]), TP, TS, LS, ret_heads,
                               o_prev=o_r)
        rs_p.append(_ret_state_from_groups(sg_p))
        rs_s.append(_ret_state_from_groups(sg_s))

        cw, cb, cg, cbeta = conv_w[l], conv_b[l], conv_ln_g[l], conv_ln_b[l]
        o_c = _conformer_conv(u, zero_conv, cw, cb, cg, cbeta, 0, LP)
        o_c = _conformer_conv(u, state_conv[l], cw, cb, cg, cbeta, TP, LS, o_prev=o_c)
        u_p = u[:TP].reshape(BP, LP, conv_ch)
        u_s = u[TP:].reshape(BS, LS, conv_ch)
        cs_p.append(u_p[:, LP - (CONV_WIDTH - 1):])
        cs_s.append(jnp.concatenate([state_conv[l], u_s], axis=1)[:, -(CONV_WIDTH - 1):])

        ct_p = _cumsum_time(logf[:TP].reshape(BP, LP, fox_heads))[0] * LOG2E
        o_f = _fox_prompt_attention(fq16, fk16, fv16, ct_p, LP, fox_bq)

        lf_all = jnp.concatenate([cache_fox_logf[l].astype(F32), logf[TP:].reshape(BS, LS, fox_heads),
                                  jnp.zeros((BS, LANES - LS, fox_heads), F32)], axis=1)
        ct_s = (_cumsum_time(lf_all) * LOG2E).transpose(0, 2, 1)
        ck_s = ct_s.reshape(BS * n_pairs, FOX_HEADS_PER_STEP, past + LANES)
        cq_s = ct_s[:, :, past:past + LS].reshape(BS * n_pairs, FOX_HEADS_PER_STEP * LS, 1)
        o_f = _fox_sample_attention(fq16, fk16, fv16, cache_k, cache_v, l, cq_s, ck_s, TP, LS, o_f)

        rw = jnp.concatenate([router_group_w[l], router_expert_w[l],
                              jnp.zeros((D, LANES - N_GROUPS - N_EXPERTS), F32)], axis=1).astype(BF16)
        rb = jnp.concatenate([router_group_b[l], router_expert_b[l],
                              jnp.zeros((LANES - N_GROUPS - N_EXPERTS,), F32)]).reshape(1, LANES)
        x1_32, x1_16, route = _out_proj_router(x32, o_r, o_f, o_c, w_out[l].astype(BF16),
                                               ln1_g[l], ln1_b[l], rw, rb, alpha)

        row_token, row_gate, block_expert, n_used, pos = _moe_dispatch(route)
        xs = jnp.take(x1_16, row_token, axis=0)
        ys = _moe_experts(xs, row_gate, block_expert, n_used, w_gate, w_up, w_down, l)
        f0 = jnp.take(ys, pos[:, 0], axis=0)
        f1 = jnp.take(ys, pos[:, 1], axis=0)
        x32, x16 = _residual_ln(x1_32, f0, f1, ln2_g[l], ln2_b[l], alpha)

        ks.append(fk32)
        vs.append(fv32)
        lfs.append(logf)

    def split(ts, tail):
        a = jnp.stack(ts)
        return (a[:, :TP].reshape((depth, BP, LP) + tail), a[:, TP:].reshape((depth, BS, LS) + tail))

    pk, sk = split(ks, (fox_heads, HEAD_DIM))
    pv, sv = split(vs, (fox_heads, HEAD_DIM))
    plf, slf = split(lfs, (fox_heads,))
    return (x32[:TP].reshape(BP, LP, D), x32[TP:].reshape(BS, LS, D),
            pk, pv, plf, jnp.stack(rs_p), jnp.stack(cs_p),
            sk, sv, slf, jnp.stack(rs_s), jnp.stack(cs_s))
```

```python
import functools

import numpy as np
import jax
import jax.numpy as jnp
from jax import lax
from jax.experimental import pallas as pl
from jax.experimental.pallas import tpu as pltpu

F32 = jnp.float32
BF16 = jnp.bfloat16

HEAD_DIM = 64
CHUNK = 64
CONV_WIDTH = 31
N_GROUPS = 4
EXPERTS_PER_GROUP = 8
N_EXPERTS = N_GROUPS * EXPERTS_PER_GROUP
ROPE_BASE = 10000.0
LN_EPS = 1e-5
NEG_BIG = -1e30

LANES = 128
MOE_BLOCK_ROWS = 256
VMEM_LIMIT = 56 * 1024 * 1024


def _cparams(*sem):
    return pltpu.CompilerParams(dimension_semantics=sem, vmem_limit_bytes=VMEM_LIMIT)


def _row_tile(n, pref):
    t = min(n, pref)
    while n % t:
        t //= 2
    return t


def _ln_rows(y, g, b):
    mu = jnp.mean(y, axis=-1, keepdims=True)
    yc = y - mu
    var = jnp.mean(yc * yc, axis=-1, keepdims=True)
    return yc * lax.rsqrt(var + LN_EPS) * g + b


def _ln_kernel(x_ref, g_ref, b_ref, o32_ref, o16_ref):
    y = _ln_rows(x_ref[...], g_ref[...], b_ref[...])
    o32_ref[...] = y
    o16_ref[...] = y.astype(BF16)


def _layer_norm_in(x, g, b):
    T, D = x.shape
    tm = _row_tile(T, 256)
    row = pl.BlockSpec((tm, D), lambda i: (i, 0))
    vec = pl.BlockSpec((1, D), lambda i: (0, 0))
    return pl.pallas_call(
        _ln_kernel,
        grid=(T // tm,),
        in_specs=[row, vec, vec],
        out_specs=[row, row],
        out_shape=[jax.ShapeDtypeStruct((T, D), F32), jax.ShapeDtypeStruct((T, D), BF16)],
        compiler_params=_cparams("parallel"),
        name="ln_in",
    )(x, g.reshape(1, D), b.reshape(1, D))


def _res_ln_kernel(alpha, x_ref, f0_ref, f1_ref, g_ref, b_ref, o32_ref, o16_ref):
    y = _ln_rows(alpha * x_ref[...] + (f0_ref[...] + f1_ref[...]), g_ref[...], b_ref[...])
    o32_ref[...] = y
    o16_ref[...] = y.astype(BF16)


def _residual_ln(x, f0, f1, g, b, alpha):
    T, D = x.shape
    tm = _row_tile(T, 256)
    row = pl.BlockSpec((tm, D), lambda i: (i, 0))
    vec = pl.BlockSpec((1, D), lambda i: (0, 0))
    return pl.pallas_call(
        functools.partial(_res_ln_kernel, alpha),
        grid=(T // tm,),
        in_specs=[row, row, row, vec, vec],
        out_specs=[row, row],
        out_shape=[jax.ShapeDtypeStruct((T, D), F32), jax.ShapeDtypeStruct((T, D), BF16)],
        compiler_params=_cparams("parallel"),
        name="res_ln",
    )(x, f0, f1, g.reshape(1, D), b.reshape(1, D))


def _mm_kernel(x_ref, w_ref, o_ref):
    o_ref[...] = jnp.dot(x_ref[...], w_ref[...], preferred_element_type=F32).astype(o_ref.dtype)


def _matmul(x, w, out_dtype, name):
    T, K = x.shape
    N = w.shape[1]
    tm = _row_tile(T, 1024)
    tn = _row_tile(N, 1024)
    return pl.pallas_call(
        _mm_kernel,
        grid=(T // tm, N // tn),
        in_specs=[pl.BlockSpec((tm, K), lambda i, j: (i, 0)),
                  pl.BlockSpec((K, tn), lambda i, j: (0, j))],
        out_specs=pl.BlockSpec((tm, tn), lambda i, j: (i, j)),
        out_shape=jax.ShapeDtypeStruct((T, N), out_dtype),
        compiler_params=_cparams("parallel", "parallel"),
        name=name,
    )(x, w)


def _fox_proj_kernel(W, x_ref, w_ref, q16_ref, k32_ref, v32_ref, k16_ref, v16_ref, qn_ref, kn_ref):
    x = x_ref[...]
    q16 = jnp.dot(x, w_ref[:, 0:W], preferred_element_type=F32).astype(BF16)
    q16_ref[...] = q16
    k = jnp.dot(x, w_ref[:, W:2 * W], preferred_element_type=F32)
    k16 = k.astype(BF16)
    k32_ref[...] = k
    k16_ref[...] = k16
    v = jnp.dot(x, w_ref[:, 2 * W:3 * W], preferred_element_type=F32)
    v32_ref[...] = v
    v16_ref[...] = v.astype(BF16)

    col_head = lax.broadcasted_iota(jnp.int32, (W, LANES), 0) // HEAD_DIM
    head = lax.broadcasted_iota(jnp.int32, (W, LANES), 1)
    head_sum = (col_head == head).astype(BF16)

    def max_sq_norm(t16):
        t = t16.astype(F32)
        n2 = jnp.dot((t * t).astype(BF16), head_sum, preferred_element_type=F32)
        return jnp.broadcast_to(jnp.max(n2, axis=0, keepdims=True), (8, LANES))

    qn_ref[0] = max_sq_norm(q16)
    kn_ref[0] = max_sq_norm(k16)


def _fox_proj(x16, w_fox):
    T, D = x16.shape
    W = w_fox.shape[1] // 3
    tm = _row_tile(T, 512)
    out = pl.BlockSpec((tm, W), lambda i: (i, 0))
    nrm = pl.BlockSpec((1, 8, LANES), lambda i: (i, 0, 0))
    return pl.pallas_call(
        functools.partial(_fox_proj_kernel, W),
        grid=(T // tm,),
        in_specs=[pl.BlockSpec((tm, D), lambda i: (i, 0)),
                  pl.BlockSpec((D, 3 * W), lambda i: (0, 0))],
        out_specs=[out, out, out, out, out, nrm, nrm],
        out_shape=[jax.ShapeDtypeStruct((T, W), BF16), jax.ShapeDtypeStruct((T, W), F32),
                   jax.ShapeDtypeStruct((T, W), F32), jax.ShapeDtypeStruct((T, W), BF16),
                   jax.ShapeDtypeStruct((T, W), BF16),
                   jax.ShapeDtypeStruct((T // tm, 8, LANES), F32),
                   jax.ShapeDtypeStruct((T // tm, 8, LANES), F32)],
        compiler_params=_cparams("parallel"),
        name="fox_proj",
    )(x16, w_fox)


def _log_sigmoid(z):
    return -(jnp.maximum(-z, 0.0) + jnp.log1p(jnp.exp(-jnp.abs(z))))


def _conv_fgt_proj_kernel(cc, nh, x_ref, w_ref, bf_ref, u_ref, lf_ref):
    x = x_ref[...]
    a = jnp.dot(x, w_ref[:, 0:cc], preferred_element_type=F32)
    gt = jnp.dot(x, w_ref[:, cc:2 * cc], preferred_element_type=F32)
    u_ref[...] = a * jax.nn.sigmoid(gt)
    z = jnp.dot(x, w_ref[:, 2 * cc:2 * cc + LANES], preferred_element_type=F32) + bf_ref[...]
    lf_ref[...] = _log_sigmoid(z)[:, :nh]


def _conv_fgt_proj(x16, w_cf, b_forget, cc, nh):
    T, D = x16.shape
    tm = _row_tile(T, 512)
    nw = w_cf.shape[1]
    bf = jnp.zeros((1, LANES), F32).at[0, :nh].set(b_forget)
    return pl.pallas_call(
        functools.partial(_conv_fgt_proj_kernel, cc, nh),
        grid=(T // tm,),
        in_specs=[pl.BlockSpec((tm, D), lambda i: (i, 0)),
                  pl.BlockSpec((D, nw), lambda i: (0, 0)),
                  pl.BlockSpec((1, LANES), lambda i: (0, 0))],
        out_specs=[pl.BlockSpec((tm, cc), lambda i: (i, 0)),
                   pl.BlockSpec((tm, nh), lambda i: (i, 0))],
        out_shape=[jax.ShapeDtypeStruct((T, cc), F32), jax.ShapeDtypeStruct((T, nh), F32)],
        compiler_params=_cparams("parallel"),
        name="conv_fgt_proj",
    )(x16, w_cf, bf)


def _trunc16(x):
    bits = lax.bitcast_convert_type(x, jnp.uint32) & jnp.uint32(0xFFFF0000)
    return lax.bitcast_convert_type(bits, F32)


def _split3(x):
    hi = _trunc16(x)
    r1 = x - hi
    mid = _trunc16(r1)
    lo = r1 - mid
    return hi.astype(BF16), mid.astype(BF16), lo.astype(BF16)


def _cumsum_kernel(nblk, x_ref, o_ref):
    r = lax.broadcasted_iota(jnp.int32, (LANES, LANES), 0)
    c = lax.broadcasted_iota(jnp.int32, (LANES, LANES), 1)
    tri = (r <= c).astype(BF16)

    def body(j, carry):
        hi, mid, lo = _split3(x_ref[0, j])
        s = (jnp.dot(hi, tri, preferred_element_type=F32)
             + jnp.dot(mid, tri, preferred_element_type=F32)
             + jnp.dot(lo, tri, preferred_element_type=F32)) + carry
        o_ref[0, j] = s
        return s[:, LANES - 1:LANES]

    lax.fori_loop(0, nblk, body, jnp.zeros((x_ref.shape[2], 1), F32))


def _cumsum_time(x):
    n, L, H = x.shape
    assert L % LANES == 0
    nblk = L // LANES
    xt = x.reshape(n, nblk, LANES, H).transpose(0, 1, 3, 2)
    blk = pl.BlockSpec((1, nblk, H, LANES), lambda i: (i, 0, 0, 0))
    out = pl.pallas_call(
        functools.partial(_cumsum_kernel, nblk),
        grid=(n,),
        in_specs=[blk],
        out_specs=blk,
        out_shape=jax.ShapeDtypeStruct((n, nblk, H, LANES), F32),
        compiler_params=_cparams("parallel"),
        name="cumsum_logf",
    )(xt)
    return out.transpose(0, 1, 3, 2).reshape(n, L, H)


RET_GROUP_HEADS = 4
RET_GROUP_LANES = RET_GROUP_HEADS * HEAD_DIM
RET_CHUNKS_PER_STEP = 8


def _ret_perm(n_heads):
    half = HEAD_DIM // 2
    perm = np.zeros(n_heads * HEAD_DIM, np.int32)
    for h in range(n_heads):
        g, hh = divmod(h, RET_GROUP_HEADS)
        for hf in range(2):
            for j in range(half):
                perm[g * RET_GROUP_LANES + hf * LANES + hh * half + j] = h * HEAD_DIM + hf * half + j
    return perm


def _retention_kernel(n_groups, cps, x_ref, cos_ref, sin_ref, inner_ref, qdec_ref, kdec_ref, cdec_ref,
                      s0_ref, o_ref, sout_ref, s_scr, o_scr):
    W = n_groups * RET_GROUP_LANES
    c = CHUNK
    step = pl.program_id(0)
    n_chunks = x_ref.shape[0] // c

    k_lane = lax.broadcasted_iota(jnp.int32, (c, RET_GROUP_LANES), 1)
    k_head = (k_lane % LANES) // (HEAD_DIM // 2)
    v_head = k_lane // HEAD_DIM
    sr = lax.broadcasted_iota(jnp.int32, (RET_GROUP_LANES, RET_GROUP_LANES), 0)
    sc = lax.broadcasted_iota(jnp.int32, (RET_GROUP_LANES, RET_GROUP_LANES), 1)
    s_mask = ((sr % LANES) // (HEAD_DIM // 2)) == (sc // HEAD_DIM)

    for ci in range(n_chunks):
        rows = slice(ci * c, (ci + 1) * c)
        cos = cos_ref[rows, :]
        sin = sin_ref[rows, :]
        if cps == 1:
            seq_local, first, last = ci, True, True
        else:
            seq_local = 0
            first = (step * n_chunks + ci) % cps == 0
            last = (step * n_chunks + ci) % cps == cps - 1
        for g in range(n_groups):
            lo = g * RET_GROUP_LANES

            def rot(t):
                t1, t2 = t[:, :LANES], t[:, LANES:]
                return jnp.concatenate([t1 * cos - t2 * sin, t1 * sin + t2 * cos], axis=1)

            q = rot(x_ref[rows, lo:lo + RET_GROUP_LANES])
            k = rot(x_ref[rows, W + lo:W + lo + RET_GROUP_LANES])
            v = x_ref[rows, 2 * W + lo:2 * W + lo + RET_GROUP_LANES]
            q16 = q.astype(BF16)
            k16 = k.astype(BF16)
            v16 = v.astype(BF16)
            kd16 = (k * kdec_ref[g]).astype(BF16)

            if cps == 1:
                s_prev = s0_ref[seq_local, g]
            else:
                @pl.when(first)
                def _():
                    s_scr[g] = s0_ref[0, g]
                s_prev = s_scr[g]

            zero = jnp.zeros_like(k16)
            k_bd = jnp.concatenate([jnp.where(k_head == hh, k16, zero) for hh in range(RET_GROUP_HEADS)], axis=0)
            v_bd = jnp.concatenate([jnp.where(v_head == hh, v16, zero) for hh in range(RET_GROUP_HEADS)], axis=0)
            s_all = lax.dot_general(q16, k_bd, (((1,), (1,)), ((), ())), preferred_element_type=F32)
            att = (s_all * inner_ref[g]).astype(BF16)
            o_intra = jnp.dot(att, v_bd, preferred_element_type=F32)
            o_cross = jnp.dot(q16, s_prev.astype(BF16), preferred_element_type=F32) * qdec_ref[g]
            o_scr[rows, lo:lo + RET_GROUP_LANES] = o_intra + o_cross

            kv = lax.dot_general(kd16, v16, (((0,), (0,)), ((), ())), preferred_element_type=F32)
            s_new = s_prev * cdec_ref[g] + jnp.where(s_mask, kv, 0.0)
            if cps == 1:
                sout_ref[seq_local, g] = s_new
            else:
                s_scr[g] = s_new

                @pl.when(last)
                def _():
                    sout_ref[0, g] = s_new

    pr = lax.broadcasted_iota(jnp.int32, (W, W), 0) // HEAD_DIM
    pc = lax.broadcasted_iota(jnp.int32, (W, W), 1) // HEAD_DIM
    avg = jnp.where(pr == pc, 1.0 / HEAD_DIM, 0.0).astype(BF16)

    def head_mean(t):
        hi, mid, lo3 = _split3(t)
        return (jnp.dot(hi, avg, preferred_element_type=F32)
                + jnp.dot(mid, avg, preferred_element_type=F32)
                + jnp.dot(lo3, avg, preferred_element_type=F32))

    o = o_scr[...]
    oc = o - head_mean(o)
    var = head_mean(oc * oc)
    gate = x_ref[:, 3 * W:4 * W]
    o_ref[...] = (oc * lax.rsqrt(var + LN_EPS) * (gate * jax.nn.sigmoid(gate))).astype(BF16)


def _retention_tables(n_heads):
    c = CHUNK
    n_groups = n_heads // RET_GROUP_HEADS
    lg = jnp.log1p(-jnp.exp2(-5.0 - jnp.arange(n_heads, dtype=F32)))
    idx = jnp.arange(c, dtype=F32)
    diff = idx[:, None] - idx[None, :]
    causal = diff >= 0
    inner = jnp.where(causal[None], jnp.exp(jnp.where(causal, diff, 0.0)[None] * lg[:, None, None]), 0.0)
    q_dec = jnp.exp((idx + 1.0)[:, None] * lg[None, :])
    k_dec = jnp.exp((c - 1.0 - idx)[:, None] * lg[None, :])
    c_dec = jnp.exp(c * lg)
    inner_t = inner.reshape(n_groups, RET_GROUP_HEADS, c, c).transpose(0, 2, 1, 3).reshape(n_groups, c, RET_GROUP_HEADS * c)
    out_head = np.arange(RET_GROUP_LANES) // HEAD_DIM
    k_head = (np.arange(RET_GROUP_LANES) % LANES) // (HEAD_DIM // 2)
    gh = np.arange(n_groups)[:, None] * RET_GROUP_HEADS
    qdec_t = q_dec[:, gh + out_head[None, :]].transpose(1, 0, 2)
    kdec_t = k_dec[:, gh + k_head[None, :]].transpose(1, 0, 2)
    cdec_t = c_dec[gh + out_head[None, :]][:, None, :]
    return inner_t, qdec_t, kdec_t, cdec_t


def _rotary_tables(pos):
    half = HEAD_DIM // 2
    inv = ROPE_BASE ** (-jnp.arange(half, dtype=F32) / half)
    ang = pos.astype(F32)[:, None] * inv[None, :]
    return (jnp.tile(jnp.cos(ang), (1, LANES // half)), jnp.tile(jnp.sin(ang), (1, LANES // half)))


def _retention(xr, cos, sin, s0g, row0, n_rows, seq_len, n_heads):
    W = n_heads * HEAD_DIM
    n_groups = n_heads // RET_GROUP_HEADS
    nseq = n_rows // seq_len
    cps = seq_len // CHUNK
    tl = RET_CHUNKS_PER_STEP * CHUNK
    assert n_rows % tl == 0 and row0 % tl == 0 and (cps == 1 or cps % RET_CHUNKS_PER_STEP == 0)
    off = row0 // tl
    if cps == 1:
        sblk = pl.BlockSpec((RET_CHUNKS_PER_STEP, n_groups, RET_GROUP_LANES, RET_GROUP_LANES),
                            lambda i: (i, 0, 0, 0))
    else:
        spc = cps // RET_CHUNKS_PER_STEP
        sblk = pl.BlockSpec((1, n_groups, RET_GROUP_LANES, RET_GROUP_LANES), lambda i: (i // spc, 0, 0, 0))
    inner_t, qdec_t, kdec_t, cdec_t = _retention_tables(n_heads)

    def full(a):
        return pl.BlockSpec(a.shape, lambda i: (0,) * a.ndim)

    return pl.pallas_call(
        functools.partial(_retention_kernel, n_groups, cps),
        grid=(n_rows // tl,),
        in_specs=[pl.BlockSpec((tl, 4 * W), lambda i: (i + off, 0)),
                  pl.BlockSpec((tl, LANES), lambda i: (i + off, 0)),
                  pl.BlockSpec((tl, LANES), lambda i: (i + off, 0)),
                  full(inner_t), full(qdec_t), full(kdec_t), full(cdec_t), sblk],
        out_specs=[pl.BlockSpec((tl, W), lambda i: (i, 0)), sblk],
        out_shape=[jax.ShapeDtypeStruct((n_rows, W), BF16),
                   jax.ShapeDtypeStruct((nseq, n_groups, RET_GROUP_LANES, RET_GROUP_LANES), F32)],
        scratch_shapes=[pltpu.VMEM((n_groups, RET_GROUP_LANES, RET_GROUP_LANES), F32),
                        pltpu.VMEM((tl, W), F32)],
        compiler_params=_cparams("arbitrary"),
        name="retention",
    )(xr, cos, sin, inner_t, qdec_t, kdec_t, cdec_t, s0g)


def _ret_state_to_groups(s):
    n, H = s.shape[:2]
    G = H // RET_GROUP_HEADS
    half = HEAD_DIM // 2
    t = s.reshape(n, G, RET_GROUP_HEADS, 2, half, HEAD_DIM).transpose(0, 1, 3, 2, 4, 5)
    eye = jnp.eye(RET_GROUP_HEADS, dtype=s.dtype)
    t = t[:, :, :, :, :, None, :] * eye[None, None, None, :, None, :, None]
    return t.reshape(n, G, RET_GROUP_LANES, RET_GROUP_LANES)


def _ret_state_from_groups(sg):
    n, G = sg.shape[:2]
    half = HEAD_DIM // 2
    t = sg.reshape(n, G, 2, RET_GROUP_HEADS, half, RET_GROUP_HEADS, HEAD_DIM)
    t = jnp.stack([t[:, :, :, hh, :, hh, :] for hh in range(RET_GROUP_HEADS)], axis=2)
    return t.reshape(n, G * RET_GROUP_HEADS, HEAD_DIM, HEAD_DIM)


CONV_HALO = 32


def _conv_kernel(tl, u_ref, prev_ref, w_ref, cb_ref, g_ref, b_ref, o_ref, buf):
    j = pl.program_id(1)

    @pl.when(j == 0)
    def _():
        buf[0:CONV_HALO, :] = prev_ref[0]

    @pl.when(j > 0)
    def _():
        buf[0:CONV_HALO, :] = buf[tl:tl + CONV_HALO, :]

    buf[CONV_HALO:CONV_HALO + tl, :] = u_ref[...]
    off = CONV_HALO - (CONV_WIDTH - 1)
    acc = buf[off:off + tl, :] * w_ref[0:1, :]
    for w in range(1, CONV_WIDTH):
        acc = acc + buf[off + w:off + w + tl, :] * w_ref[w:w + 1, :]
    y = _ln_rows(acc + cb_ref[...], g_ref[...], b_ref[...])
    o_ref[...] = (y * jax.nn.sigmoid(y)).astype(BF16)


def _conformer_conv(u, prev, w, cb, g, b, row0, seq_len):
    C = u.shape[1]
    nseq = prev.shape[0]
    tl = _row_tile(seq_len, 512)
    assert tl >= CONV_HALO and row0 % tl == 0
    off = row0 // tl
    nt = seq_len // tl
    prev_p = jnp.pad(prev.astype(F32), ((0, 0), (CONV_HALO - (CONV_WIDTH - 1), 0), (0, 0)))
    w_p = jnp.pad(w, ((0, CONV_HALO - CONV_WIDTH), (0, 0)))
    vec = pl.BlockSpec((1, C), lambda s, j: (0, 0))
    return pl.pallas_call(
        functools.partial(_conv_kernel, tl),
        grid=(nseq, nt),
        in_specs=[pl.BlockSpec((tl, C), lambda s, j: (off + s * nt + j, 0)),
                  pl.BlockSpec((1, CONV_HALO, C), lambda s, j: (s, 0, 0)),
                  pl.BlockSpec((CONV_HALO, C), lambda s, j: (0, 0)),
                  vec, vec, vec],
        out_specs=pl.BlockSpec((tl, C), lambda s, j: (s * nt + j, 0)),
        out_shape=jax.ShapeDtypeStruct((nseq * seq_len, C), BF16),
        scratch_shapes=[pltpu.VMEM((tl + CONV_HALO, C), F32)],
        compiler_params=_cparams("arbitrary", "arbitrary"),
        name="conformer_conv",
    )(u, prev_p, w_p, cb.reshape(1, C), g.reshape(1, C), b.reshape(1, C))


FOX_HEADS_PER_STEP = 2
LOG2E = 1.4426950408889634
FOX_SKIP_LOG2 = -160.0


def _split3_f32(x):
    hi = _trunc16(x)
    r1 = x - hi
    mid = _trunc16(r1)
    return hi, mid, r1 - mid


def _fox_prompt_kernel(bq, nkb, jlo_ref, q_ref, k_ref, v_ref, cq_ref, ck_ref, o_ref,
                       kt_scr, m_scr, l_scr, acc_scr):
    pair = pl.program_id(0)
    i = pl.program_id(1)
    heads = range(FOX_HEADS_PER_STEP)

    def key_rows(j):
        return pl.ds(pl.multiple_of(j * bq, bq), bq)

    @pl.when(i == 0)
    def _():
        row = lax.broadcasted_iota(jnp.int32, (LANES, bq), 0)
        ones = ((row < 3) | ((row >= 6) & (row < 9))).astype(F32)

        def build(j, carry):
            kt_scr[j, 0:LANES, :] = k_ref[key_rows(j), :].T
            ck = ck_ref[j]
            bias = ones
            for h in heads:
                terms = _split3_f32(-ck[h:h + 1, :])
                for t in range(3):
                    bias = jnp.where(row == 6 * h + 3 + t, terms[t], bias)
            kt_scr[j, LANES:2 * LANES, :] = bias.astype(BF16)
            return carry

        lax.fori_loop(0, nkb, build, 0)

    q = q_ref[...]
    lane = lax.broadcasted_iota(jnp.int32, (bq, LANES), 1)
    cq = cq_ref[...]
    qs = []
    for h in heads:
        terms = _split3_f32(cq[:, h:h + 1])
        bias = ((lane >= 6 * h + 3) & (lane < 6 * h + 6)).astype(F32)
        for t in range(3):
            bias = jnp.where(lane == 6 * h + t, terms[t], bias)
        in_head = (lane < HEAD_DIM) if h == 0 else (lane >= HEAD_DIM)
        qs.append(jnp.concatenate([jnp.where(in_head, q, jnp.zeros_like(q)), bias.astype(BF16)], axis=1))
        m_scr[h] = jnp.full((bq, LANES), NEG_BIG, F32)
        l_scr[h] = jnp.zeros((bq, LANES), F32)
        acc_scr[h] = jnp.zeros((bq, LANES), F32)

    def update(h, s, v):
        m_prev = m_scr[h]
        m_new = jnp.maximum(m_prev, jnp.max(s, axis=1, keepdims=True))
        p = jnp.exp2(s - jnp.concatenate([m_new] * (bq // LANES), axis=1))
        alpha = jnp.exp2(m_prev - m_new)
        l_scr[h] = alpha * l_scr[h] + jnp.sum(p, axis=1, keepdims=True)
        acc_scr[h] = alpha * acc_scr[h] + jnp.dot(p.astype(BF16), v, preferred_element_type=F32)
        m_scr[h] = m_new

    def scores(h, j):
        return jnp.dot(qs[h], kt_scr[j], preferred_element_type=F32)

    def body(j, carry):
        s = [scores(h, j) for h in heads]
        v = v_ref[key_rows(j), :]
        for h in heads:
            update(h, s[h], v)
        return carry

    lax.fori_loop(jlo_ref[pair * nkb + i], i, body, 0)

    r = lax.broadcasted_iota(jnp.int32, (bq, bq), 0)
    c = lax.broadcasted_iota(jnp.int32, (bq, bq), 1)
    s = [scores(h, i) for h in heads]
    v = v_ref[key_rows(i), :]
    for h in heads:
        update(h, jnp.where(c <= r, s[h], NEG_BIG), v)

    o_ref[...] = jnp.where(lane < HEAD_DIM, acc_scr[0] / l_scr[0], acc_scr[1] / l_scr[1]).astype(BF16)


def _fox_skip(qn2, kn2, ct, n_heads, bq):
    L = ct.shape[0]
    nb = L // bq
    qn = jnp.sqrt(qn2[:nb, 0, :n_heads])
    kn = jnp.sqrt(kn2[:nb, 0, :n_heads])
    kmax = jnp.max(kn, axis=0, keepdims=True)
    ctb = ct.reshape(nb, bq, n_heads)
    bound = (1.02 * qn * (kmax + kn) + ctb[:, 0, :])[:, None, :] - ctb[None, :, bq - 1, :]
    skip = (bound < FOX_SKIP_LOG2) & (jnp.arange(nb)[None, :, None] < jnp.arange(nb)[:, None, None])
    jlo = jnp.sum(jnp.cumprod(skip.astype(jnp.int32), axis=1), axis=1)
    jlo = jnp.min(jlo.reshape(nb, n_heads // FOX_HEADS_PER_STEP, FOX_HEADS_PER_STEP), axis=2)
    return jlo.T.reshape(-1).astype(jnp.int32)


def _fox_prompt_attention(q16, k16, v16, qn2, kn2, ct, L, bq):
    W = q16.shape[1]
    n_heads = W // HEAD_DIM
    hp = FOX_HEADS_PER_STEP
    n_pairs = n_heads // hp
    nkb = L // bq
    jlo = _fox_skip(qn2, kn2, ct, n_heads, bq)
    cq = ct.reshape(L, n_pairs, hp).transpose(1, 0, 2)
    ck = ct.reshape(nkb, bq, n_pairs, hp).transpose(2, 0, 3, 1)
    grid_spec = pltpu.PrefetchScalarGridSpec(
        num_scalar_prefetch=1,
        grid=(n_pairs, nkb),
        in_specs=[pl.BlockSpec((bq, LANES), lambda p, i, jl: (i, p)),
                  pl.BlockSpec((L, LANES), lambda p, i, jl: (0, p)),
                  pl.BlockSpec((L, LANES), lambda p, i, jl: (0, p)),
                  pl.BlockSpec((None, bq, hp), lambda p, i, jl: (p, i, 0)),
                  pl.BlockSpec((None, nkb, hp, bq), lambda p, i, jl: (p, 0, 0, 0))],
        out_specs=pl.BlockSpec((bq, LANES), lambda p, i, jl: (i, p)),
        scratch_shapes=[pltpu.VMEM((nkb, 2 * LANES, bq), BF16),
                        pltpu.VMEM((hp, bq, LANES), F32), pltpu.VMEM((hp, bq, LANES), F32),
                        pltpu.VMEM((hp, bq, LANES), F32)],
    )
    return pl.pallas_call(
        functools.partial(_fox_prompt_kernel, bq, nkb),
        grid_spec=grid_spec,
        out_shape=jax.ShapeDtypeStruct((L, W), BF16),
        compiler_params=_cparams("arbitrary", "arbitrary"),
        name="fox_prompt",
    )(jlo, q16, k16, v16, cq, ck)


def _fox_sample_kernel(past, ls, q_ref, kn_ref, vn_ref, kc_ref, vc_ref, cq_ref, ck_ref, o_ref):
    q = q_ref[...]
    lane = lax.broadcasted_iota(jnp.int32, q.shape, 1)
    zero = jnp.zeros_like(q)
    q2 = jnp.concatenate([jnp.where(lane < HEAD_DIM, q, zero), jnp.where(lane >= HEAD_DIM, q, zero)], axis=0)
    pad = jnp.zeros((LANES - ls, LANES), BF16)
    k_all = jnp.concatenate([kc_ref[0].astype(BF16), kn_ref[...], pad], axis=0)
    v_all = jnp.concatenate([vc_ref[0].astype(BF16), vn_ref[...], pad], axis=0)
    nk = past + LANES
    s = lax.dot_general(q2, k_all, (((1,), (1,)), ((), ())), preferred_element_type=F32)
    row = lax.broadcasted_iota(jnp.int32, (2 * ls, nk), 0)
    col = lax.broadcasted_iota(jnp.int32, (2 * ls, nk), 1)
    ck = jnp.where(row < ls, ck_ref[0, 0:1, :], ck_ref[0, 1:2, :])
    visible = (col < past) | (col - past <= row % ls)
    s = jnp.where(visible, s + cq_ref[0] - ck, NEG_BIG)
    p = jnp.exp2(s - jnp.max(s, axis=1, keepdims=True))
    den = jnp.sum(p, axis=1, keepdims=True)
    o2 = jnp.dot(p.astype(BF16), v_all, preferred_element_type=F32) / den
    o_ref[...] = jnp.where(lane < HEAD_DIM, o2[0:ls], o2[ls:2 * ls]).astype(BF16)


def _fox_sample_attention(q16, k16, v16, cache_k, cache_v, layer, cq, ck, row0, ls):
    W = q16.shape[1]
    past = cache_k.shape[1]
    n_pairs = W // LANES
    B = cq.shape[0] // n_pairs
    assert ls <= LANES and row0 % ls == 0
    off = row0 // ls
    rows = pl.BlockSpec((ls, LANES), lambda b, p: (off + b, p))
    cache = pl.BlockSpec((1, past, LANES), lambda b, p: (layer * B + b, 0, p))
    return pl.pallas_call(
        functools.partial(_fox_sample_kernel, past, ls),
        grid=(B, n_pairs),
        in_specs=[rows, rows, rows, cache, cache,
                  pl.BlockSpec((1, 2 * ls, 1), lambda b, p: (b * n_pairs + p, 0, 0)),
                  pl.BlockSpec((1, 2, past + LANES), lambda b, p: (b * n_pairs + p, 0, 0))],
        out_specs=pl.BlockSpec((ls, LANES), lambda b, p: (b, p)),
        out_shape=jax.ShapeDtypeStruct((B * ls, W), BF16),
        compiler_params=_cparams("parallel", "parallel"),
        name="fox_sample",
    )(q16, k16, v16, cache_k, cache_v, cq, ck)


def _out_router_kernel(alpha, n_first, x_ref, r0_ref, f0_ref, c0_ref, r1_ref, f1_ref, c1_ref,
                       w_ref, g_ref, b_ref, rw_ref, rb_ref, o32_ref, route_ref):
    first = pl.program_id(0) < n_first
    mixed = jnp.concatenate([jnp.where(first, r0_ref[...], r1_ref[...]),
                             jnp.where(first, f0_ref[...], f1_ref[...]),
                             jnp.where(first, c0_ref[...], c1_ref[...])], axis=1)
    m = jnp.dot(mixed, w_ref[...], preferred_element_type=F32)
    y = _ln_rows(alpha * x_ref[...] + m, g_ref[...], b_ref[...])
    o32_ref[...] = y

    logits = jnp.dot(y.astype(BF16), rw_ref[...], preferred_element_type=F32) + rb_ref[...]
    lane = lax.broadcasted_iota(jnp.int32, logits.shape, 1).astype(F32)
    far = float(LANES)

    def rmax(t):
        return jnp.max(t, axis=1, keepdims=True)

    def rmin(t):
        return jnp.min(t, axis=1, keepdims=True)

    def rsum(t):
        return jnp.sum(t, axis=1, keepdims=True)

    gmask = lane < N_GROUPS
    lg = jnp.where(gmask, logits, -jnp.inf)
    mg = rmax(lg)
    p_grp = 1.0 / rsum(jnp.exp(lg - mg))
    g_idx = rmin(jnp.where(lg == mg, lane, far))
    e_lo = N_GROUPS + EXPERTS_PER_GROUP * g_idx
    emask = (lane >= e_lo) & (lane < e_lo + EXPERTS_PER_GROUP)
    le = jnp.where(emask, logits, -jnp.inf)
    ee = jnp.exp(le - rmax(le))
    pin = jnp.where(emask, ee / rsum(ee), -1.0)
    p1 = rmax(pin)
    i1 = rmin(jnp.where(pin == p1, lane, far))
    pin2 = jnp.where(lane == i1, -1.0, pin)
    p2 = rmax(pin2)
    i2 = rmin(jnp.where(pin2 == p2, lane, far))
    den = p1 + p2
    route_ref[...] = jnp.where(lane == 0, i1 - N_GROUPS,
                     jnp.where(lane == 1, i2 - N_GROUPS,
                     jnp.where(lane == 2, p_grp * p1 / den,
                     jnp.where(lane == 3, p_grp * p2 / den, 0.0))))


def _out_proj_router(x, mix0, mix1, w_out16, g, b, rw16, rb, alpha):
    T, D = x.shape
    tm = _row_tile(mix1[0].shape[0], 256)
    n0 = mix0[0].shape[0] // tm
    n1 = mix1[0].shape[0] // tm
    assert mix0[0].shape[0] % tm == 0 and (n0 + n1) * tm == T

    def rows(a):
        return pl.BlockSpec((tm, a.shape[1]), lambda i: (i, 0))

    def rows0(a):
        return pl.BlockSpec((tm, a.shape[1]), lambda i: (jnp.minimum(i, n0 - 1), 0))

    def rows1(a):
        return pl.BlockSpec((tm, a.shape[1]), lambda i: (jnp.maximum(i - n0, 0), 0))

    def full(a):
        return pl.BlockSpec(a.shape, lambda i: (0, 0))

    g2, b2 = g.reshape(1, D), b.reshape(1, D)
    return pl.pallas_call(
        functools.partial(_out_router_kernel, alpha, n0),
        grid=(T // tm,),
        in_specs=[rows(x)] + [rows0(a) for a in mix0] + [rows1(a) for a in mix1]
                 + [full(w_out16), full(g2), full(b2), full(rw16), full(rb)],
        out_specs=[pl.BlockSpec((tm, D), lambda i: (i, 0)), pl.BlockSpec((tm, LANES), lambda i: (i, 0))],
        out_shape=[jax.ShapeDtypeStruct((T, D), F32), jax.ShapeDtypeStruct((T, LANES), F32)],
        compiler_params=_cparams("parallel"),
        name="out_proj_router",
    )(x, *mix0, *mix1, w_out16, g2, b2, rw16, rb)


def _moe_kernel(be_ref, nused_ref, idx_ref, idx_next_ref, x_hbm, gate_ref, wg_ref, wu_ref, wd_ref, y_ref,
                wg16, wu16, wd16, xbuf, sem):
    blk = pl.program_id(0)
    n_blk = pl.num_programs(0)
    bm = xbuf.shape[1]
    used = blk < nused_ref[0]
    prev_expert = be_ref[jnp.maximum(blk - 1, 0)]

    def start_gather(rows_ref, slot):
        def issue(r, carry):
            pltpu.make_async_copy(x_hbm.at[pl.ds(rows_ref[0, r], 1)], xbuf.at[slot, pl.ds(r, 1)],
                                  sem.at[slot]).start()
            return carry

        lax.fori_loop(0, bm, issue, 0, unroll=8)

    @pl.when(blk == 0)
    def _():
        start_gather(idx_ref, 0)

    @pl.when(blk + 1 < n_blk)
    def _():
        start_gather(idx_next_ref, (blk + 1) % 2)

    slot = blk % 2
    pltpu.make_async_copy(x_hbm.at[pl.ds(0, bm)], xbuf.at[slot], sem.at[slot]).wait()

    @pl.when(used & ((blk == 0) | (be_ref[blk] != prev_expert)))
    def _():
        wg16[...] = wg_ref[0].astype(BF16)
        wu16[...] = wu_ref[0].astype(BF16)
        wd16[...] = wd_ref[0].astype(BF16)

    @pl.when(used)
    def _():
        x = xbuf[slot].astype(BF16)
        a = jnp.dot(x, wg16[...], preferred_element_type=F32)
        u = jnp.dot(x, wu16[...], preferred_element_type=F32)
        h = (a * jax.nn.sigmoid(a) * u).astype(BF16)
        y_ref[...] = jnp.dot(h, wd16[...], preferred_element_type=F32) * gate_ref[...]

    @pl.when(jnp.logical_not(used))
    def _():
        y_ref[...] = jnp.zeros_like(y_ref)


def _moe_experts(x, row_token, row_gate, block_expert, n_used, wg, wu, wd, layer):
    T, D = x.shape
    R = row_token.shape[0]
    DE = wg.shape[3]
    bm = MOE_BLOCK_ROWS
    n_blk = R // bm
    rows2d = row_token.reshape(n_blk, 1, bm)
    grid_spec = pltpu.PrefetchScalarGridSpec(
        num_scalar_prefetch=2,
        grid=(n_blk,),
        in_specs=[pl.BlockSpec((None, 1, bm), lambda i, be, nu: (i, 0, 0), memory_space=pltpu.SMEM),
                  pl.BlockSpec((None, 1, bm), lambda i, be, nu: (jnp.minimum(i + 1, n_blk - 1), 0, 0),
                               memory_space=pltpu.SMEM),
                  pl.BlockSpec(memory_space=pl.ANY),
                  pl.BlockSpec((bm, 1), lambda i, be, nu: (i, 0)),
                  pl.BlockSpec((None, 1, D, DE), lambda i, be, nu: (layer, be[i], 0, 0)),
                  pl.BlockSpec((None, 1, D, DE), lambda i, be, nu: (layer, be[i], 0, 0)),
                  pl.BlockSpec((None, 1, DE, D), lambda i, be, nu: (layer, be[i], 0, 0))],
        out_specs=pl.BlockSpec((bm, D), lambda i, be, nu: (i, 0)),
        scratch_shapes=[pltpu.VMEM((D, DE), BF16), pltpu.VMEM((D, DE), BF16), pltpu.VMEM((DE, D), BF16),
                        pltpu.VMEM((2, bm, D), F32), pltpu.SemaphoreType.DMA((2,))],
    )
    return pl.pallas_call(
        _moe_kernel,
        grid_spec=grid_spec,
        out_shape=jax.ShapeDtypeStruct((R, D), F32),
        compiler_params=_cparams("arbitrary"),
        name="moe_experts",
    )(block_expert, n_used, rows2d, rows2d, x, row_gate, wg, wu, wd)


def _moe_dispatch(route):
    T = route.shape[0]
    bm = MOE_BLOCK_ROWS
    eid = route[:, 0:2].astype(jnp.int32).reshape(-1)
    gate = route[:, 2:4].reshape(-1)
    n_assign = 2 * T
    n_rows = n_assign + N_EXPERTS * bm
    n_blocks = n_rows // bm
    counts = jnp.sum((jnp.arange(N_EXPERTS)[:, None] == eid[None, :]).astype(jnp.int32), axis=1)
    padded = ((counts + bm - 1) // bm) * bm
    ends = jnp.cumsum(padded)
    starts = ends - padded
    first = jnp.cumsum(counts) - counts
    order = jnp.argsort(eid, stable=True).astype(jnp.int32)
    rank = jnp.argsort(order).astype(jnp.int32)
    pos = rank + (starts - first)[eid]
    n_used = ends[-1] // bm
    blk_start = jnp.arange(n_blocks, dtype=jnp.int32) * bm
    block_expert = jnp.sum((ends[None, :] <= blk_start[:, None]).astype(jnp.int32), axis=1)
    block_expert = jnp.minimum(block_expert, N_EXPERTS - 1)
    last_used = block_expert[jnp.maximum(n_used - 1, 0)]
    block_expert = jnp.where(jnp.arange(n_blocks) < n_used, block_expert, last_used)
    r = jnp.arange(n_rows, dtype=jnp.int32)
    e_r = block_expert[r // bm]
    s_r = r - starts[e_r] + first[e_r]
    valid = (r < ends[-1]) & (s_r < first[e_r] + counts[e_r])
    a_r = order[jnp.clip(s_r, 0, n_assign - 1)]
    row_token = jnp.where(valid, a_r // 2, 0)
    row_gate = jnp.where(valid, gate[a_r], 0.0)
    return row_token, row_gate.reshape(n_rows, 1), block_expert, n_used.reshape(1).astype(jnp.int32), pos.reshape(T, 2)


def kernel(x_prompt, x_sample, cache_fox_k, cache_fox_v, cache_fox_logf, state_ret, state_conv, ln_in_g, ln_in_b, w_in, b_forget, conv_w, conv_b, conv_ln_g, conv_ln_b, w_out, ln1_g, ln1_b, router_group_w, router_group_b, router_expert_w, router_expert_b, w_gate, w_up, w_down, ln2_g, ln2_b):
    BP, LP, D = x_prompt.shape
    BS, LS, _ = x_sample.shape
    depth = w_in.shape[0]
    past = cache_fox_k.shape[2]
    fox_heads = cache_fox_k.shape[3]
    ret_heads = state_ret.shape[2]
    conv_ch = state_conv.shape[3]
    ret_w = ret_heads * HEAD_DIM
    fox_w = fox_heads * HEAD_DIM
    off_fox = 4 * ret_w
    off_fgt = off_fox + 3 * fox_w
    off_conv = off_fgt + fox_heads
    alpha = (2 * depth) ** 0.25
    TP, TS = BP * LP, BS * LS
    assert BP == 1 and LS == CHUNK and past % 512 == 0

    x_raw = jnp.concatenate([x_prompt.reshape(TP, D), x_sample.reshape(TS, D)], axis=0)
    x32, x16 = _layer_norm_in(x_raw, ln_in_g, ln_in_b)

    perm = _ret_perm(ret_heads)
    pos_all = jnp.concatenate([jnp.arange(LP), past + jnp.tile(jnp.arange(LS), BS)])
    cos_t, sin_t = _rotary_tables(pos_all)
    fox_bq = _row_tile(LP, 512)
    n_pairs = fox_heads // FOX_HEADS_PER_STEP
    cache_k = cache_fox_k.astype(BF16).reshape(depth * BS, past, fox_w)
    cache_v = cache_fox_v.astype(BF16).reshape(depth * BS, past, fox_w)
    qk_scale = HEAD_DIM ** -0.5
    zero_state = jnp.zeros((BP, ret_heads // RET_GROUP_HEADS, RET_GROUP_LANES, RET_GROUP_LANES), F32)
    zero_conv = jnp.zeros((BP, CONV_WIDTH - 1, conv_ch), F32)

    ks, vs, lfs, rs_p, rs_s, cs_p, cs_s = [], [], [], [], [], [], []
    for l in range(depth):
        wl = w_in[l]
        w_ret = jnp.concatenate([wl[:, 0:ret_w][:, perm],
                                 (wl[:, ret_w:2 * ret_w] * qk_scale)[:, perm],
                                 wl[:, 2 * ret_w:4 * ret_w]], axis=1).astype(BF16)
        w_fox = jnp.concatenate([wl[:, off_fox:off_fox + fox_w] * (qk_scale * LOG2E),
                                 wl[:, off_fox + fox_w:off_fgt]], axis=1).astype(BF16)
        w_cf = jnp.concatenate([wl[:, off_conv:], wl[:, off_fgt:off_conv],
                                jnp.zeros((D, LANES - fox_heads), F32)], axis=1).astype(BF16)

        xr = _matmul(x16, w_ret, F32, "ret_proj")
        fq16, fk32, fv32, fk16, fv16, qn2, kn2 = _fox_proj(x16, w_fox)
        u, logf = _conv_fgt_proj(x16, w_cf, b_forget[l], conv_ch, fox_heads)

        o_r_p, sg_p = _retention(xr, cos_t, sin_t, zero_state, 0, TP, LP, ret_heads)
        o_r_s, sg_s = _retention(xr, cos_t, sin_t, _ret_state_to_groups(state_ret[l]), TP, TS, LS, ret_heads)
        rs_p.append(_ret_state_from_groups(sg_p))
        rs_s.append(_ret_state_from_groups(sg_s))

        cw, cb, cg, cbeta = conv_w[l], conv_b[l], conv_ln_g[l], conv_ln_b[l]
        o_c_p = _conformer_conv(u, zero_conv, cw, cb, cg, cbeta, 0, LP)
        o_c_s = _conformer_conv(u, state_conv[l], cw, cb, cg, cbeta, TP, LS)
        u_p = u[:TP].reshape(BP, LP, conv_ch)
        u_s = u[TP:].reshape(BS, LS, conv_ch)
        cs_p.append(u_p[:, LP - (CONV_WIDTH - 1):])
        cs_s.append(jnp.concatenate([state_conv[l], u_s], axis=1)[:, -(CONV_WIDTH - 1):])

        ct_p = _cumsum_time(logf[:TP].reshape(BP, LP, fox_heads))[0] * LOG2E
        o_f_p = _fox_prompt_attention(fq16, fk16, fv16, qn2, kn2, ct_p, LP, fox_bq)

        lf_all = jnp.concatenate([cache_fox_logf[l].astype(F32), logf[TP:].reshape(BS, LS, fox_heads),
                                  jnp.zeros((BS, LANES - LS, fox_heads), F32)], axis=1)
        ct_s = (_cumsum_time(lf_all) * LOG2E).transpose(0, 2, 1)
        ck_s = ct_s.reshape(BS * n_pairs, FOX_HEADS_PER_STEP, past + LANES)
        cq_s = ct_s[:, :, past:past + LS].reshape(BS * n_pairs, FOX_HEADS_PER_STEP * LS, 1)
        o_f_s = _fox_sample_attention(fq16, fk16, fv16, cache_k, cache_v, l, cq_s, ck_s, TP, LS)

        rw = jnp.concatenate([router_group_w[l], router_expert_w[l],
                              jnp.zeros((D, LANES - N_GROUPS - N_EXPERTS), F32)], axis=1).astype(BF16)
        rb = jnp.concatenate([router_group_b[l], router_expert_b[l],
                              jnp.zeros((LANES - N_GROUPS - N_EXPERTS,), F32)]).reshape(1, LANES)
        x1_32, route = _out_proj_router(x32, (o_r_p, o_f_p, o_c_p), (o_r_s, o_f_s, o_c_s),
                                        w_out[l].astype(BF16), ln1_g[l], ln1_b[l], rw, rb, alpha)

        row_token, row_gate, block_expert, n_used, pos = _moe_dispatch(route)
        ys = _moe_experts(x1_32, row_token, row_gate, block_expert, n_used, w_gate, w_up, w_down, l)
        f0 = jnp.take(ys, pos[:, 0], axis=0)
        f1 = jnp.take(ys, pos[:, 1], axis=0)
        x32, x16 = _residual_ln(x1_32, f0, f1, ln2_g[l], ln2_b[l], alpha)

        ks.append(fk32)
        vs.append(fv32)
        lfs.append(logf)

    def split(ts, tail):
        a = jnp.stack(ts)
        return (a[:, :TP].reshape((depth, BP, LP) + tail), a[:, TP:].reshape((depth, BS, LS) + tail))

    pk, sk = split(ks, (fox_heads, HEAD_DIM))
    pv, sv = split(vs, (fox_heads, HEAD_DIM))
    plf, slf = split(lfs, (fox_heads,))
    return (x32[:TP].reshape(BP, LP, D), x32[TP:].reshape(BS, LS, D),
            pk, pv, plf, jnp.stack(rs_p), jnp.stack(cs_p),
            sk, sv, slf, jnp.stack(rs_s), jnp.stack(cs_s))
```

```python
import functools

import numpy as np
import jax
import jax.numpy as jnp
from jax import lax
from jax.experimental import pallas as pl
from jax.experimental.pallas import tpu as pltpu

F32 = jnp.float32
BF16 = jnp.bfloat16

HEAD_DIM = 64
CHUNK = 64
CONV_WIDTH = 31
N_GROUPS = 4
EXPERTS_PER_GROUP = 8
N_EXPERTS = N_GROUPS * EXPERTS_PER_GROUP
ROPE_BASE = 10000.0
LN_EPS = 1e-5
NEG_BIG = -1e30

LANES = 128
MOE_BLOCK_ROWS = 256
VMEM_LIMIT = 56 * 1024 * 1024


def _cparams(*sem):
    return pltpu.CompilerParams(dimension_semantics=sem, vmem_limit_bytes=VMEM_LIMIT)


def _row_tile(n, pref):
    t = min(n, pref)
    while n % t:
        t //= 2
    return t


def _ln_rows(y, g, b):
    mu = jnp.mean(y, axis=-1, keepdims=True)
    yc = y - mu
    var = jnp.mean(yc * yc, axis=-1, keepdims=True)
    return yc * lax.rsqrt(var + LN_EPS) * g + b


def _ln_kernel(n_first, x0_ref, x1_ref, g_ref, b_ref, o32_ref, o16_ref):
    x = jnp.where(pl.program_id(0) < n_first, x0_ref[...], x1_ref[...])
    y = _ln_rows(x, g_ref[...], b_ref[...])
    o32_ref[...] = y
    o16_ref[...] = y.astype(BF16)


def _layer_norm_in(x0, x1, g, b):
    D = x0.shape[1]
    tm = _row_tile(x1.shape[0], 256)
    n0, n1 = x0.shape[0] // tm, x1.shape[0] // tm
    assert x0.shape[0] % tm == 0
    T = (n0 + n1) * tm
    row = pl.BlockSpec((tm, D), lambda i: (i, 0))
    vec = pl.BlockSpec((1, D), lambda i: (0, 0))
    return pl.pallas_call(
        functools.partial(_ln_kernel, n0),
        grid=(n0 + n1,),
        in_specs=[pl.BlockSpec((tm, D), lambda i: (jnp.minimum(i, n0 - 1), 0)),
                  pl.BlockSpec((tm, D), lambda i: (jnp.maximum(i - n0, 0), 0)), vec, vec],
        out_specs=[row, row],
        out_shape=[jax.ShapeDtypeStruct((T, D), F32), jax.ShapeDtypeStruct((T, D), BF16)],
        compiler_params=_cparams("parallel"),
        name="ln_in",
    )(x0, x1, g.reshape(1, D), b.reshape(1, D))


def _res_ln_kernel(alpha, x_ref, f0_ref, f1_ref, g_ref, b_ref, o32_ref, o16_ref):
    y = _ln_rows(alpha * x_ref[...] + (f0_ref[...] + f1_ref[...]), g_ref[...], b_ref[...])
    o32_ref[...] = y
    o16_ref[...] = y.astype(BF16)


def _residual_ln(x, f0, f1, g, b, alpha):
    T, D = x.shape
    tm = _row_tile(T, 256)
    row = pl.BlockSpec((tm, D), lambda i: (i, 0))
    vec = pl.BlockSpec((1, D), lambda i: (0, 0))
    return pl.pallas_call(
        functools.partial(_res_ln_kernel, alpha),
        grid=(T // tm,),
        in_specs=[row, row, row, vec, vec],
        out_specs=[row, row],
        out_shape=[jax.ShapeDtypeStruct((T, D), F32), jax.ShapeDtypeStruct((T, D), BF16)],
        compiler_params=_cparams("parallel"),
        name="res_ln",
    )(x, f0, f1, g.reshape(1, D), b.reshape(1, D))


def _mm_kernel(x_ref, w_ref, o_ref):
    o_ref[...] = jnp.dot(x_ref[...], w_ref[...], preferred_element_type=F32).astype(o_ref.dtype)


def _matmul(x, w, out_dtype, name):
    T, K = x.shape
    N = w.shape[1]
    tm = _row_tile(T, 1024)
    tn = _row_tile(N, 1024)
    return pl.pallas_call(
        _mm_kernel,
        grid=(T // tm, N // tn),
        in_specs=[pl.BlockSpec((tm, K), lambda i, j: (i, 0)),
                  pl.BlockSpec((K, tn), lambda i, j: (0, j))],
        out_specs=pl.BlockSpec((tm, tn), lambda i, j: (i, j)),
        out_shape=jax.ShapeDtypeStruct((T, N), out_dtype),
        compiler_params=_cparams("parallel", "parallel"),
        name=name,
    )(x, w)


def _fox_proj_kernel(W, x_ref, w_ref, q16_ref, k32_ref, v32_ref, k16_ref, v16_ref, qn_ref, kn_ref):
    x = x_ref[...]
    q16 = jnp.dot(x, w_ref[:, 0:W], preferred_element_type=F32).astype(BF16)
    q16_ref[...] = q16
    k = jnp.dot(x, w_ref[:, W:2 * W], preferred_element_type=F32)
    k16 = k.astype(BF16)
    k32_ref[...] = k
    k16_ref[...] = k16
    v = jnp.dot(x, w_ref[:, 2 * W:3 * W], preferred_element_type=F32)
    v32_ref[...] = v
    v16_ref[...] = v.astype(BF16)

    col_head = lax.broadcasted_iota(jnp.int32, (W, LANES), 0) // HEAD_DIM
    head = lax.broadcasted_iota(jnp.int32, (W, LANES), 1)
    head_sum = (col_head == head).astype(BF16)

    def max_sq_norm(t16):
        t = t16.astype(F32)
        n2 = jnp.dot((t * t).astype(BF16), head_sum, preferred_element_type=F32)
        return jnp.broadcast_to(jnp.max(n2, axis=0, keepdims=True), (8, LANES))

    qn_ref[0] = max_sq_norm(q16)
    kn_ref[0] = max_sq_norm(k16)


def _fox_proj(x16, w_fox):
    T, D = x16.shape
    W = w_fox.shape[1] // 3
    tm = _row_tile(T, 512)
    out = pl.BlockSpec((tm, W), lambda i: (i, 0))
    nrm = pl.BlockSpec((1, 8, LANES), lambda i: (i, 0, 0))
    return pl.pallas_call(
        functools.partial(_fox_proj_kernel, W),
        grid=(T // tm,),
        in_specs=[pl.BlockSpec((tm, D), lambda i: (i, 0)),
                  pl.BlockSpec((D, 3 * W), lambda i: (0, 0))],
        out_specs=[out, out, out, out, out, nrm, nrm],
        out_shape=[jax.ShapeDtypeStruct((T, W), BF16), jax.ShapeDtypeStruct((T, W), F32),
                   jax.ShapeDtypeStruct((T, W), F32), jax.ShapeDtypeStruct((T, W), BF16),
                   jax.ShapeDtypeStruct((T, W), BF16),
                   jax.ShapeDtypeStruct((T // tm, 8, LANES), F32),
                   jax.ShapeDtypeStruct((T // tm, 8, LANES), F32)],
        compiler_params=_cparams("parallel"),
        name="fox_proj",
    )(x16, w_fox)


def _log_sigmoid(z):
    return -(jnp.maximum(-z, 0.0) + jnp.log1p(jnp.exp(-jnp.abs(z))))


def _conv_fgt_proj_kernel(cc, nh, x_ref, w_ref, bf_ref, u_ref, lf_ref):
    x = x_ref[...]
    a = jnp.dot(x, w_ref[:, 0:cc], preferred_element_type=F32)
    gt = jnp.dot(x, w_ref[:, cc:2 * cc], preferred_element_type=F32)
    u_ref[...] = a * jax.nn.sigmoid(gt)
    z = jnp.dot(x, w_ref[:, 2 * cc:2 * cc + LANES], preferred_element_type=F32) + bf_ref[...]
    lf_ref[...] = _log_sigmoid(z)[:, :nh]


def _conv_fgt_proj(x16, w_cf, b_forget, cc, nh):
    T, D = x16.shape
    tm = _row_tile(T, 512)
    nw = w_cf.shape[1]
    bf = jnp.zeros((1, LANES), F32).at[0, :nh].set(b_forget)
    return pl.pallas_call(
        functools.partial(_conv_fgt_proj_kernel, cc, nh),
        grid=(T // tm,),
        in_specs=[pl.BlockSpec((tm, D), lambda i: (i, 0)),
                  pl.BlockSpec((D, nw), lambda i: (0, 0)),
                  pl.BlockSpec((1, LANES), lambda i: (0, 0))],
        out_specs=[pl.BlockSpec((tm, cc), lambda i: (i, 0)),
                   pl.BlockSpec((tm, nh), lambda i: (i, 0))],
        out_shape=[jax.ShapeDtypeStruct((T, cc), F32), jax.ShapeDtypeStruct((T, nh), F32)],
        compiler_params=_cparams("parallel"),
        name="conv_fgt_proj",
    )(x16, w_cf, bf)


def _trunc16(x):
    bits = lax.bitcast_convert_type(x, jnp.uint32) & jnp.uint32(0xFFFF0000)
    return lax.bitcast_convert_type(bits, F32)


def _split3(x):
    hi = _trunc16(x)
    r1 = x - hi
    mid = _trunc16(r1)
    lo = r1 - mid
    return hi.astype(BF16), mid.astype(BF16), lo.astype(BF16)


def _cumsum_kernel(nblk, x_ref, o_ref):
    r = lax.broadcasted_iota(jnp.int32, (LANES, LANES), 0)
    c = lax.broadcasted_iota(jnp.int32, (LANES, LANES), 1)
    tri = (r <= c).astype(BF16)

    def body(j, carry):
        hi, mid, lo = _split3(x_ref[0, j])
        s = (jnp.dot(hi, tri, preferred_element_type=F32)
             + jnp.dot(mid, tri, preferred_element_type=F32)
             + jnp.dot(lo, tri, preferred_element_type=F32)) + carry
        o_ref[0, j] = s
        return s[:, LANES - 1:LANES]

    lax.fori_loop(0, nblk, body, jnp.zeros((x_ref.shape[2], 1), F32))


def _cumsum_time(x):
    n, L, H = x.shape
    assert L % LANES == 0
    nblk = L // LANES
    xt = x.reshape(n, nblk, LANES, H).transpose(0, 1, 3, 2)
    blk = pl.BlockSpec((1, nblk, H, LANES), lambda i: (i, 0, 0, 0))
    out = pl.pallas_call(
        functools.partial(_cumsum_kernel, nblk),
        grid=(n,),
        in_specs=[blk],
        out_specs=blk,
        out_shape=jax.ShapeDtypeStruct((n, nblk, H, LANES), F32),
        compiler_params=_cparams("parallel"),
        name="cumsum_logf",
    )(xt)
    return out.transpose(0, 1, 3, 2).reshape(n, L, H)


RET_GROUP_HEADS = 4
RET_GROUP_LANES = RET_GROUP_HEADS * HEAD_DIM
RET_CHUNKS_PER_STEP = 8


def _ret_perm(n_heads):
    half = HEAD_DIM // 2
    perm = np.zeros(n_heads * HEAD_DIM, np.int32)
    for h in range(n_heads):
        g, hh = divmod(h, RET_GROUP_HEADS)
        for hf in range(2):
            for j in range(half):
                perm[g * RET_GROUP_LANES + hf * LANES + hh * half + j] = h * HEAD_DIM + hf * half + j
    return perm


def _retention_kernel(n_groups, cps, x_ref, cos_ref, sin_ref, inner_ref, qdec_ref, kdec_ref, cdec_ref,
                      s0_ref, o_ref, sout_ref, s_scr, o_scr):
    W = n_groups * RET_GROUP_LANES
    c = CHUNK
    step = pl.program_id(0)
    n_chunks = x_ref.shape[0] // c

    k_lane = lax.broadcasted_iota(jnp.int32, (c, RET_GROUP_LANES), 1)
    k_head = (k_lane % LANES) // (HEAD_DIM // 2)
    v_head = k_lane // HEAD_DIM
    sr = lax.broadcasted_iota(jnp.int32, (RET_GROUP_LANES, RET_GROUP_LANES), 0)
    sc = lax.broadcasted_iota(jnp.int32, (RET_GROUP_LANES, RET_GROUP_LANES), 1)
    s_mask = ((sr % LANES) // (HEAD_DIM // 2)) == (sc // HEAD_DIM)

    for ci in range(n_chunks):
        rows = slice(ci * c, (ci + 1) * c)
        cos = cos_ref[rows, :]
        sin = sin_ref[rows, :]
        if cps == 1:
            seq_local, first, last = ci, True, True
        else:
            seq_local = 0
            first = (step * n_chunks + ci) % cps == 0
            last = (step * n_chunks + ci) % cps == cps - 1
        for g in range(n_groups):
            lo = g * RET_GROUP_LANES

            def rot(t):
                t1, t2 = t[:, :LANES], t[:, LANES:]
                return jnp.concatenate([t1 * cos - t2 * sin, t1 * sin + t2 * cos], axis=1)

            q = rot(x_ref[rows, lo:lo + RET_GROUP_LANES])
            k = rot(x_ref[rows, W + lo:W + lo + RET_GROUP_LANES])
            v = x_ref[rows, 2 * W + lo:2 * W + lo + RET_GROUP_LANES]
            q16 = q.astype(BF16)
            k16 = k.astype(BF16)
            v16 = v.astype(BF16)
            kd16 = (k * kdec_ref[g]).astype(BF16)

            if cps == 1:
                s_prev = s0_ref[seq_local, g]
            else:
                @pl.when(first)
                def _():
                    s_scr[g] = s0_ref[0, g]
                s_prev = s_scr[g]

            zero = jnp.zeros_like(k16)
            k_bd = jnp.concatenate([jnp.where(k_head == hh, k16, zero) for hh in range(RET_GROUP_HEADS)], axis=0)
            v_bd = jnp.concatenate([jnp.where(v_head == hh, v16, zero) for hh in range(RET_GROUP_HEADS)], axis=0)
            s_all = lax.dot_general(q16, k_bd, (((1,), (1,)), ((), ())), preferred_element_type=F32)
            att = (s_all * inner_ref[g]).astype(BF16)
            o_intra = jnp.dot(att, v_bd, preferred_element_type=F32)
            o_cross = jnp.dot(q16, s_prev.astype(BF16), preferred_element_type=F32) * qdec_ref[g]
            o_scr[rows, lo:lo + RET_GROUP_LANES] = o_intra + o_cross

            kv = lax.dot_general(kd16, v16, (((0,), (0,)), ((), ())), preferred_element_type=F32)
            s_new = s_prev * cdec_ref[g] + jnp.where(s_mask, kv, 0.0)
            if cps == 1:
                sout_ref[seq_local, g] = s_new
            else:
                s_scr[g] = s_new

                @pl.when(last)
                def _():
                    sout_ref[0, g] = s_new

    pr = lax.broadcasted_iota(jnp.int32, (W, W), 0) // HEAD_DIM
    pc = lax.broadcasted_iota(jnp.int32, (W, W), 1) // HEAD_DIM
    avg = jnp.where(pr == pc, 1.0 / HEAD_DIM, 0.0).astype(BF16)

    def head_mean(t):
        hi, mid, lo3 = _split3(t)
        return (jnp.dot(hi, avg, preferred_element_type=F32)
                + jnp.dot(mid, avg, preferred_element_type=F32)
                + jnp.dot(lo3, avg, preferred_element_type=F32))

    o = o_scr[...]
    oc = o - head_mean(o)
    var = head_mean(oc * oc)
    gate = x_ref[:, 3 * W:4 * W]
    o_ref[...] = (oc * lax.rsqrt(var + LN_EPS) * (gate * jax.nn.sigmoid(gate))).astype(BF16)


def _retention_tables(n_heads):
    c = CHUNK
    n_groups = n_heads // RET_GROUP_HEADS
    lg = jnp.log1p(-jnp.exp2(-5.0 - jnp.arange(n_heads, dtype=F32)))
    idx = jnp.arange(c, dtype=F32)
    diff = idx[:, None] - idx[None, :]
    causal = diff >= 0
    inner = jnp.where(causal[None], jnp.exp(jnp.where(causal, diff, 0.0)[None] * lg[:, None, None]), 0.0)
    q_dec = jnp.exp((idx + 1.0)[:, None] * lg[None, :])
    k_dec = jnp.exp((c - 1.0 - idx)[:, None] * lg[None, :])
    c_dec = jnp.exp(c * lg)
    inner_t = inner.reshape(n_groups, RET_GROUP_HEADS, c, c).transpose(0, 2, 1, 3).reshape(n_groups, c, RET_GROUP_HEADS * c)
    out_head = np.arange(RET_GROUP_LANES) // HEAD_DIM
    k_head = (np.arange(RET_GROUP_LANES) % LANES) // (HEAD_DIM // 2)
    gh = np.arange(n_groups)[:, None] * RET_GROUP_HEADS
    qdec_t = q_dec[:, gh + out_head[None, :]].transpose(1, 0, 2)
    kdec_t = k_dec[:, gh + k_head[None, :]].transpose(1, 0, 2)
    cdec_t = c_dec[gh + out_head[None, :]][:, None, :]
    return inner_t, qdec_t, kdec_t, cdec_t


def _rotary_tables(pos):
    half = HEAD_DIM // 2
    inv = ROPE_BASE ** (-jnp.arange(half, dtype=F32) / half)
    ang = pos.astype(F32)[:, None] * inv[None, :]
    return (jnp.tile(jnp.cos(ang), (1, LANES // half)), jnp.tile(jnp.sin(ang), (1, LANES // half)))


def _retention(xr, cos, sin, s0g, row0, n_rows, seq_len, n_heads):
    W = n_heads * HEAD_DIM
    n_groups = n_heads // RET_GROUP_HEADS
    nseq = n_rows // seq_len
    cps = seq_len // CHUNK
    tl = RET_CHUNKS_PER_STEP * CHUNK
    assert n_rows % tl == 0 and row0 % tl == 0 and (cps == 1 or cps % RET_CHUNKS_PER_STEP == 0)
    off = row0 // tl
    if cps == 1:
        sblk = pl.BlockSpec((RET_CHUNKS_PER_STEP, n_groups, RET_GROUP_LANES, RET_GROUP_LANES),
                            lambda i: (i, 0, 0, 0))
    else:
        spc = cps // RET_CHUNKS_PER_STEP
        sblk = pl.BlockSpec((1, n_groups, RET_GROUP_LANES, RET_GROUP_LANES), lambda i: (i // spc, 0, 0, 0))
    inner_t, qdec_t, kdec_t, cdec_t = _retention_tables(n_heads)

    def full(a):
        return pl.BlockSpec(a.shape, lambda i: (0,) * a.ndim)

    return pl.pallas_call(
        functools.partial(_retention_kernel, n_groups, cps),
        grid=(n_rows // tl,),
        in_specs=[pl.BlockSpec((tl, 4 * W), lambda i: (i + off, 0)),
                  pl.BlockSpec((tl, LANES), lambda i: (i + off, 0)),
                  pl.BlockSpec((tl, LANES), lambda i: (i + off, 0)),
                  full(inner_t), full(qdec_t), full(kdec_t), full(cdec_t), sblk],
        out_specs=[pl.BlockSpec((tl, W), lambda i: (i, 0)), sblk],
        out_shape=[jax.ShapeDtypeStruct((n_rows, W), BF16),
                   jax.ShapeDtypeStruct((nseq, n_groups, RET_GROUP_LANES, RET_GROUP_LANES), F32)],
        scratch_shapes=[pltpu.VMEM((n_groups, RET_GROUP_LANES, RET_GROUP_LANES), F32),
                        pltpu.VMEM((tl, W), F32)],
        compiler_params=_cparams("arbitrary"),
        name="retention",
    )(xr, cos, sin, inner_t, qdec_t, kdec_t, cdec_t, s0g)


def _ret_state_to_groups(s):
    n, H = s.shape[:2]
    G = H // RET_GROUP_HEADS
    half = HEAD_DIM // 2
    t = s.reshape(n, G, RET_GROUP_HEADS, 2, half, HEAD_DIM).transpose(0, 1, 3, 2, 4, 5)
    eye = jnp.eye(RET_GROUP_HEADS, dtype=s.dtype)
    t = t[:, :, :, :, :, None, :] * eye[None, None, None, :, None, :, None]
    return t.reshape(n, G, RET_GROUP_LANES, RET_GROUP_LANES)


def _ret_state_from_groups(sg):
    n, G = sg.shape[:2]
    half = HEAD_DIM // 2
    t = sg.reshape(n, G, 2, RET_GROUP_HEADS, half, RET_GROUP_HEADS, HEAD_DIM)
    t = jnp.stack([t[:, :, :, hh, :, hh, :] for hh in range(RET_GROUP_HEADS)], axis=2)
    return t.reshape(n, G * RET_GROUP_HEADS, HEAD_DIM, HEAD_DIM)


CONV_HALO = 32


def _conv_kernel(tl, u_ref, prev_ref, w_ref, cb_ref, g_ref, b_ref, o_ref, buf):
    j = pl.program_id(1)

    @pl.when(j == 0)
    def _():
        buf[0:CONV_HALO, :] = prev_ref[0]

    @pl.when(j > 0)
    def _():
        buf[0:CONV_HALO, :] = buf[tl:tl + CONV_HALO, :]

    buf[CONV_HALO:CONV_HALO + tl, :] = u_ref[...]
    off = CONV_HALO - (CONV_WIDTH - 1)
    acc = buf[off:off + tl, :] * w_ref[0:1, :]
    for w in range(1, CONV_WIDTH):
        acc = acc + buf[off + w:off + w + tl, :] * w_ref[w:w + 1, :]
    y = _ln_rows(acc + cb_ref[...], g_ref[...], b_ref[...])
    o_ref[...] = (y * jax.nn.sigmoid(y)).astype(BF16)


def _conformer_conv(u, prev, w, cb, g, b, row0, seq_len):
    C = u.shape[1]
    nseq = prev.shape[0]
    tl = _row_tile(seq_len, 512)
    assert tl >= CONV_HALO and row0 % tl == 0
    off = row0 // tl
    nt = seq_len // tl
    prev_p = jnp.pad(prev.astype(F32), ((0, 0), (CONV_HALO - (CONV_WIDTH - 1), 0), (0, 0)))
    w_p = jnp.pad(w, ((0, CONV_HALO - CONV_WIDTH), (0, 0)))
    vec = pl.BlockSpec((1, C), lambda s, j: (0, 0))
    return pl.pallas_call(
        functools.partial(_conv_kernel, tl),
        grid=(nseq, nt),
        in_specs=[pl.BlockSpec((tl, C), lambda s, j: (off + s * nt + j, 0)),
                  pl.BlockSpec((1, CONV_HALO, C), lambda s, j: (s, 0, 0)),
                  pl.BlockSpec((CONV_HALO, C), lambda s, j: (0, 0)),
                  vec, vec, vec],
        out_specs=pl.BlockSpec((tl, C), lambda s, j: (s * nt + j, 0)),
        out_shape=jax.ShapeDtypeStruct((nseq * seq_len, C), BF16),
        scratch_shapes=[pltpu.VMEM((tl + CONV_HALO, C), F32)],
        compiler_params=_cparams("arbitrary", "arbitrary"),
        name="conformer_conv",
    )(u, prev_p, w_p, cb.reshape(1, C), g.reshape(1, C), b.reshape(1, C))


FOX_HEADS_PER_STEP = 2
LOG2E = 1.4426950408889634
FOX_SKIP_LOG2 = -160.0


def _split3_f32(x):
    hi = _trunc16(x)
    r1 = x - hi
    mid = _trunc16(r1)
    return hi, mid, r1 - mid


def _fox_prompt_kernel(bq, nkb, jlo_ref, q_ref, k_ref, v_ref, cq_ref, ck_ref, o_ref,
                       kt_scr, m_scr, l_scr, acc_scr):
    pair = pl.program_id(0)
    i = pl.program_id(1)
    heads = range(FOX_HEADS_PER_STEP)

    def key_rows(j):
        return pl.ds(pl.multiple_of(j * bq, bq), bq)

    @pl.when(i == 0)
    def _():
        row = lax.broadcasted_iota(jnp.int32, (LANES, bq), 0)
        ones = ((row < 3) | ((row >= 6) & (row < 9))).astype(F32)

        def build(j, carry):
            kt_scr[j, 0:LANES, :] = k_ref[key_rows(j), :].T
            ck = ck_ref[j]
            bias = ones
            for h in heads:
                terms = _split3_f32(-ck[h:h + 1, :])
                for t in range(3):
                    bias = jnp.where(row == 6 * h + 3 + t, terms[t], bias)
            kt_scr[j, LANES:2 * LANES, :] = bias.astype(BF16)
            return carry

        lax.fori_loop(0, nkb, build, 0)

    q = q_ref[...]
    lane = lax.broadcasted_iota(jnp.int32, (bq, LANES), 1)
    cq = cq_ref[...]
    qs = []
    for h in heads:
        terms = _split3_f32(cq[:, h:h + 1])
        bias = ((lane >= 6 * h + 3) & (lane < 6 * h + 6)).astype(F32)
        for t in range(3):
            bias = jnp.where(lane == 6 * h + t, terms[t], bias)
        in_head = (lane < HEAD_DIM) if h == 0 else (lane >= HEAD_DIM)
        qs.append(jnp.concatenate([jnp.where(in_head, q, jnp.zeros_like(q)), bias.astype(BF16)], axis=1))
        m_scr[h] = jnp.full((bq, LANES), NEG_BIG, F32)
        l_scr[h] = jnp.zeros((bq, LANES), F32)
        acc_scr[h] = jnp.zeros((bq, LANES), F32)

    def update(h, s, v):
        m_prev = m_scr[h]
        m_new = jnp.maximum(m_prev, jnp.max(s, axis=1, keepdims=True))
        p = jnp.exp2(s - jnp.concatenate([m_new] * (bq // LANES), axis=1))
        alpha = jnp.exp2(m_prev - m_new)
        l_scr[h] = alpha * l_scr[h] + jnp.sum(p, axis=1, keepdims=True)
        acc_scr[h] = alpha * acc_scr[h] + jnp.dot(p.astype(BF16), v, preferred_element_type=F32)
        m_scr[h] = m_new

    def scores(h, j):
        return jnp.dot(qs[h], kt_scr[j], preferred_element_type=F32)

    def body(j, carry):
        s = [scores(h, j) for h in heads]
        v = v_ref[key_rows(j), :]
        for h in heads:
            update(h, s[h], v)
        return carry

    lax.fori_loop(jlo_ref[pair * nkb + i], i, body, 0)

    r = lax.broadcasted_iota(jnp.int32, (bq, bq), 0)
    c = lax.broadcasted_iota(jnp.int32, (bq, bq), 1)
    s = [scores(h, i) for h in heads]
    v = v_ref[key_rows(i), :]
    for h in heads:
        update(h, jnp.where(c <= r, s[h], NEG_BIG), v)

    o_ref[...] = jnp.where(lane < HEAD_DIM, acc_scr[0] / l_scr[0], acc_scr[1] / l_scr[1]).astype(BF16)


def _fox_skip(qn2, kn2, ct, n_heads, bq):
    L = ct.shape[0]
    nb = L // bq
    qn = jnp.sqrt(qn2[:nb, 0, :n_heads])
    kn = jnp.sqrt(kn2[:nb, 0, :n_heads])
    kmax = jnp.max(kn, axis=0, keepdims=True)
    ctb = ct.reshape(nb, bq, n_heads)
    bound = (1.02 * qn * (kmax + kn) + ctb[:, 0, :])[:, None, :] - ctb[None, :, bq - 1, :]
    skip = (bound < FOX_SKIP_LOG2) & (jnp.arange(nb)[None, :, None] < jnp.arange(nb)[:, None, None])
    jlo = jnp.sum(jnp.cumprod(skip.astype(jnp.int32), axis=1), axis=1)
    jlo = jnp.min(jlo.reshape(nb, n_heads // FOX_HEADS_PER_STEP, FOX_HEADS_PER_STEP), axis=2)
    return jlo.T.reshape(-1).astype(jnp.int32)


def _fox_prompt_attention(q16, k16, v16, qn2, kn2, ct, L, bq):
    W = q16.shape[1]
    n_heads = W // HEAD_DIM
    hp = FOX_HEADS_PER_STEP
    n_pairs = n_heads // hp
    nkb = L // bq
    jlo = _fox_skip(qn2, kn2, ct, n_heads, bq)
    cq = ct.reshape(L, n_pairs, hp).transpose(1, 0, 2)
    ck = ct.reshape(nkb, bq, n_pairs, hp).transpose(2, 0, 3, 1)
    grid_spec = pltpu.PrefetchScalarGridSpec(
        num_scalar_prefetch=1,
        grid=(n_pairs, nkb),
        in_specs=[pl.BlockSpec((bq, LANES), lambda p, i, jl: (i, p)),
                  pl.BlockSpec((L, LANES), lambda p, i, jl: (0, p)),
                  pl.BlockSpec((L, LANES), lambda p, i, jl: (0, p)),
                  pl.BlockSpec((None, bq, hp), lambda p, i, jl: (p, i, 0)),
                  pl.BlockSpec((None, nkb, hp, bq), lambda p, i, jl: (p, 0, 0, 0))],
        out_specs=pl.BlockSpec((bq, LANES), lambda p, i, jl: (i, p)),
        scratch_shapes=[pltpu.VMEM((nkb, 2 * LANES, bq), BF16),
                        pltpu.VMEM((hp, bq, LANES), F32), pltpu.VMEM((hp, bq, LANES), F32),
                        pltpu.VMEM((hp, bq, LANES), F32)],
    )
    return pl.pallas_call(
        functools.partial(_fox_prompt_kernel, bq, nkb),
        grid_spec=grid_spec,
        out_shape=jax.ShapeDtypeStruct((L, W), BF16),
        compiler_params=_cparams("arbitrary", "arbitrary"),
        name="fox_prompt",
    )(jlo, q16, k16, v16, cq, ck)


def _fox_sample_kernel(past, ls, q_ref, kn_ref, vn_ref, kc_ref, vc_ref, cq_ref, ck_ref, o_ref):
    q = q_ref[...]
    lane = lax.broadcasted_iota(jnp.int32, q.shape, 1)
    zero = jnp.zeros_like(q)
    q2 = jnp.concatenate([jnp.where(lane < HEAD_DIM, q, zero), jnp.where(lane >= HEAD_DIM, q, zero)], axis=0)
    pad = jnp.zeros((LANES - ls, LANES), BF16)
    k_all = jnp.concatenate([kc_ref[0].astype(BF16), kn_ref[...], pad], axis=0)
    v_all = jnp.concatenate([vc_ref[0].astype(BF16), vn_ref[...], pad], axis=0)
    nk = past + LANES
    s = lax.dot_general(q2, k_all, (((1,), (1,)), ((), ())), preferred_element_type=F32)
    row = lax.broadcasted_iota(jnp.int32, (2 * ls, nk), 0)
    col = lax.broadcasted_iota(jnp.int32, (2 * ls, nk), 1)
    ck = jnp.where(row < ls, ck_ref[0, 0:1, :], ck_ref[0, 1:2, :])
    visible = (col < past) | (col - past <= row % ls)
    s = jnp.where(visible, s + cq_ref[0] - ck, NEG_BIG)
    p = jnp.exp2(s - jnp.max(s, axis=1, keepdims=True))
    den = jnp.sum(p, axis=1, keepdims=True)
    o2 = jnp.dot(p.astype(BF16), v_all, preferred_element_type=F32) / den
    o_ref[...] = jnp.where(lane < HEAD_DIM, o2[0:ls], o2[ls:2 * ls]).astype(BF16)


def _fox_sample_attention(q16, k16, v16, cache_k, cache_v, layer, cq, ck, row0, ls):
    W = q16.shape[1]
    past = cache_k.shape[1]
    n_pairs = W // LANES
    B = cq.shape[0] // n_pairs
    assert ls <= LANES and row0 % ls == 0
    off = row0 // ls
    rows = pl.BlockSpec((ls, LANES), lambda b, p: (off + b, p))
    cache = pl.BlockSpec((1, past, LANES), lambda b, p: (layer * B + b, 0, p))
    return pl.pallas_call(
        functools.partial(_fox_sample_kernel, past, ls),
        grid=(B, n_pairs),
        in_specs=[rows, rows, rows, cache, cache,
                  pl.BlockSpec((1, 2 * ls, 1), lambda b, p: (b * n_pairs + p, 0, 0)),
                  pl.BlockSpec((1, 2, past + LANES), lambda b, p: (b * n_pairs + p, 0, 0))],
        out_specs=pl.BlockSpec((ls, LANES), lambda b, p: (b, p)),
        out_shape=jax.ShapeDtypeStruct((B * ls, W), BF16),
        compiler_params=_cparams("parallel", "parallel"),
        name="fox_sample",
    )(q16, k16, v16, cache_k, cache_v, cq, ck)


def _out_router_kernel(alpha, n_first, x_ref, r0_ref, f0_ref, c0_ref, r1_ref, f1_ref, c1_ref,
                       w_ref, g_ref, b_ref, rw_ref, rb_ref, o32_ref, route_ref):
    first = pl.program_id(0) < n_first
    mixed = jnp.concatenate([jnp.where(first, r0_ref[...], r1_ref[...]),
                             jnp.where(first, f0_ref[...], f1_ref[...]),
                             jnp.where(first, c0_ref[...], c1_ref[...])], axis=1)
    m = jnp.dot(mixed, w_ref[...], preferred_element_type=F32)
    y = _ln_rows(alpha * x_ref[...] + m, g_ref[...], b_ref[...])
    o32_ref[...] = y

    logits = jnp.dot(y.astype(BF16), rw_ref[...], preferred_element_type=F32) + rb_ref[...]
    lane = lax.broadcasted_iota(jnp.int32, logits.shape, 1).astype(F32)
    far = float(LANES)

    def rmax(t):
        return jnp.max(t, axis=1, keepdims=True)

    def rmin(t):
        return jnp.min(t, axis=1, keepdims=True)

    def rsum(t):
        return jnp.sum(t, axis=1, keepdims=True)

    gmask = lane < N_GROUPS
    lg = jnp.where(gmask, logits, -jnp.inf)
    mg = rmax(lg)
    p_grp = 1.0 / rsum(jnp.exp(lg - mg))
    g_idx = rmin(jnp.where(lg == mg, lane, far))
    e_lo = N_GROUPS + EXPERTS_PER_GROUP * g_idx
    emask = (lane >= e_lo) & (lane < e_lo + EXPERTS_PER_GROUP)
    le = jnp.where(emask, logits, -jnp.inf)
    ee = jnp.exp(le - rmax(le))
    pin = jnp.where(emask, ee / rsum(ee), -1.0)
    p1 = rmax(pin)
    i1 = rmin(jnp.where(pin == p1, lane, far))
    pin2 = jnp.where(lane == i1, -1.0, pin)
    p2 = rmax(pin2)
    i2 = rmin(jnp.where(pin2 == p2, lane, far))
    den = p1 + p2
    route_ref[...] = jnp.where(lane == 0, i1 - N_GROUPS,
                     jnp.where(lane == 1, i2 - N_GROUPS,
                     jnp.where(lane == 2, p_grp * p1 / den,
                     jnp.where(lane == 3, p_grp * p2 / den, 0.0))))


def _out_proj_router(x, mix0, mix1, w_out16, g, b, rw16, rb, alpha):
    T, D = x.shape
    tm = _row_tile(mix1[0].shape[0], 256)
    n0 = mix0[0].shape[0] // tm
    n1 = mix1[0].shape[0] // tm
    assert mix0[0].shape[0] % tm == 0 and (n0 + n1) * tm == T

    def rows(a):
        return pl.BlockSpec((tm, a.shape[1]), lambda i: (i, 0))

    def rows0(a):
        return pl.BlockSpec((tm, a.shape[1]), lambda i: (jnp.minimum(i, n0 - 1), 0))

    def rows1(a):
        return pl.BlockSpec((tm, a.shape[1]), lambda i: (jnp.maximum(i - n0, 0), 0))

    def full(a):
        return pl.BlockSpec(a.shape, lambda i: (0, 0))

    g2, b2 = g.reshape(1, D), b.reshape(1, D)
    return pl.pallas_call(
        functools.partial(_out_router_kernel, alpha, n0),
        grid=(T // tm,),
        in_specs=[rows(x)] + [rows0(a) for a in mix0] + [rows1(a) for a in mix1]
                 + [full(w_out16), full(g2), full(b2), full(rw16), full(rb)],
        out_specs=[pl.BlockSpec((tm, D), lambda i: (i, 0)), pl.BlockSpec((tm, LANES), lambda i: (i, 0))],
        out_shape=[jax.ShapeDtypeStruct((T, D), F32), jax.ShapeDtypeStruct((T, LANES), F32)],
        compiler_params=_cparams("parallel"),
        name="out_proj_router",
    )(x, *mix0, *mix1, w_out16, g2, b2, rw16, rb)


def _moe_kernel(be_ref, nused_ref, idx_ref, idx_next_ref, x_hbm, gate_ref, wg_ref, wu_ref, wd_ref, y_ref,
                wg16, wu16, wd16, xbuf, sem):
    blk = pl.program_id(0)
    bm = xbuf.shape[1]
    n_used = nused_ref[0]
    used = blk < n_used
    slot = blk % 2
    prev_expert = be_ref[jnp.maximum(blk - 1, 0)]

    def start_gather(rows_ref, to_slot):
        for r in range(bm):
            pltpu.make_async_copy(x_hbm.at[pl.ds(rows_ref[0, r], 1)], xbuf.at[to_slot, pl.ds(r, 1)],
                                  sem.at[to_slot]).start()

    def wait_gather():
        pltpu.make_async_copy(x_hbm.at[pl.ds(0, bm)], xbuf.at[slot], sem.at[slot]).wait()

    @pl.when(blk == 0)
    def _():
        start_gather(idx_ref, 0)

    @pl.when(used & ((blk == 0) | (be_ref[blk] != prev_expert)))
    def _():
        wg16[...] = wg_ref[0].astype(BF16)
        wu16[...] = wu_ref[0].astype(BF16)
        wd16[...] = wd_ref[0].astype(BF16)

    @pl.when(blk == n_used)
    def _():
        wait_gather()

    @pl.when(used)
    def _():
        wait_gather()
        start_gather(idx_next_ref, 1 - slot)
        x = xbuf[slot].astype(BF16)
        a = jnp.dot(x, wg16[...], preferred_element_type=F32)
        u = jnp.dot(x, wu16[...], preferred_element_type=F32)
        h = (a * jax.nn.sigmoid(a) * u).astype(BF16)
        y_ref[...] = jnp.dot(h, wd16[...], preferred_element_type=F32) * gate_ref[...]

    @pl.when(jnp.logical_not(used))
    def _():
        y_ref[...] = jnp.zeros_like(y_ref)


def _moe_experts(x, row_token, row_gate, block_expert, n_used, wg, wu, wd, layer):
    T, D = x.shape
    R = row_token.shape[0]
    DE = wg.shape[3]
    bm = MOE_BLOCK_ROWS
    n_blk = R // bm
    rows2d = row_token.reshape(n_blk, 1, bm)
    grid_spec = pltpu.PrefetchScalarGridSpec(
        num_scalar_prefetch=2,
        grid=(n_blk,),
        in_specs=[pl.BlockSpec((None, 1, bm), lambda i, be, nu: (i, 0, 0), memory_space=pltpu.SMEM),
                  pl.BlockSpec((None, 1, bm), lambda i, be, nu: (jnp.minimum(i + 1, n_blk - 1), 0, 0),
                               memory_space=pltpu.SMEM),
                  pl.BlockSpec(memory_space=pl.ANY),
                  pl.BlockSpec((bm, 1), lambda i, be, nu: (i, 0)),
                  pl.BlockSpec((None, 1, D, DE), lambda i, be, nu: (layer, be[i], 0, 0)),
                  pl.BlockSpec((None, 1, D, DE), lambda i, be, nu: (layer, be[i], 0, 0)),
                  pl.BlockSpec((None, 1, DE, D), lambda i, be, nu: (layer, be[i], 0, 0))],
        out_specs=pl.BlockSpec((bm, D), lambda i, be, nu: (i, 0)),
        scratch_shapes=[pltpu.VMEM((D, DE), BF16), pltpu.VMEM((D, DE), BF16), pltpu.VMEM((DE, D), BF16),
                        pltpu.VMEM((2, bm, D), F32), pltpu.SemaphoreType.DMA((2,))],
    )
    return pl.pallas_call(
        _moe_kernel,
        grid_spec=grid_spec,
        out_shape=jax.ShapeDtypeStruct((R, D), F32),
        compiler_params=_cparams("arbitrary"),
        name="moe_experts",
    )(block_expert, n_used, rows2d, rows2d, x, row_gate, wg, wu, wd)


def _moe_dispatch(route):
    T = route.shape[0]
    bm = MOE_BLOCK_ROWS
    eid = route[:, 0:2].astype(jnp.int32).reshape(-1)
    gate = route[:, 2:4].reshape(-1)
    n_assign = 2 * T
    n_rows = n_assign + N_EXPERTS * bm
    n_blocks = n_rows // bm
    eid_s, a_s, gate_s = lax.sort((eid, jnp.arange(n_assign, dtype=jnp.int32), gate), num_keys=1, is_stable=True)
    onehot_s = (eid_s[:, None] == jnp.arange(N_EXPERTS)[None, :]).astype(F32)
    exact = lax.Precision.HIGHEST
    counts = jnp.dot(jnp.ones((1, n_assign), F32), onehot_s, precision=exact)[0].astype(jnp.int32)
    padded = ((counts + bm - 1) // bm) * bm
    ends = jnp.cumsum(padded)
    starts = ends - padded
    first = jnp.cumsum(counts) - counts
    shift = (starts - first).astype(F32)
    pos_s = jnp.arange(n_assign, dtype=jnp.int32) + jnp.dot(onehot_s, shift, precision=exact).astype(jnp.int32)
    _, pos = lax.sort((a_s, pos_s), num_keys=1)
    n_used = ends[-1] // bm
    blk_start = jnp.arange(n_blocks, dtype=jnp.int32) * bm
    block_expert = jnp.sum((ends[None, :] <= blk_start[:, None]).astype(jnp.int32), axis=1)
    block_expert = jnp.minimum(block_expert, N_EXPERTS - 1)
    last_used = block_expert[jnp.maximum(n_used - 1, 0)]
    block_expert = jnp.where(jnp.arange(n_blocks) < n_used, block_expert, last_used)
    base = blk_start - starts[block_expert] + first[block_expert]
    n_valid = jnp.clip(first[block_expert] + counts[block_expert] - base, 0, bm)
    n_valid = jnp.where(jnp.arange(n_blocks) < n_used, n_valid, 0)
    base = jnp.clip(base, 0, n_assign)

    def runs(t):
        t = jnp.concatenate([t, jnp.zeros((bm,), t.dtype)])
        return jax.vmap(lambda b0: lax.dynamic_slice(t, (b0,), (bm,)))(base)

    valid = jnp.arange(bm)[None, :] < n_valid[:, None]
    row_token = jnp.where(valid, runs(a_s // 2), 0).reshape(n_rows)
    row_gate = jnp.where(valid, runs(gate_s), 0.0)
    return row_token, row_gate.reshape(n_rows, 1), block_expert, n_used.reshape(1).astype(jnp.int32), pos.reshape(T, 2)


def kernel(x_prompt, x_sample, cache_fox_k, cache_fox_v, cache_fox_logf, state_ret, state_conv, ln_in_g, ln_in_b, w_in, b_forget, conv_w, conv_b, conv_ln_g, conv_ln_b, w_out, ln1_g, ln1_b, router_group_w, router_group_b, router_expert_w, router_expert_b, w_gate, w_up, w_down, ln2_g, ln2_b):
    BP, LP, D = x_prompt.shape
    BS, LS, _ = x_sample.shape
    depth = w_in.shape[0]
    past = cache_fox_k.shape[2]
    fox_heads = cache_fox_k.shape[3]
    ret_heads = state_ret.shape[2]
    conv_ch = state_conv.shape[3]
    ret_w = ret_heads * HEAD_DIM
    fox_w = fox_heads * HEAD_DIM
    off_fox = 4 * ret_w
    off_fgt = off_fox + 3 * fox_w
    off_conv = off_fgt + fox_heads
    alpha = (2 * depth) ** 0.25
    TP, TS = BP * LP, BS * LS
    assert BP == 1 and LS == CHUNK and past % 512 == 0

    x32, x16 = _layer_norm_in(x_prompt.reshape(TP, D), x_sample.reshape(TS, D), ln_in_g, ln_in_b)

    perm = _ret_perm(ret_heads)
    pos_all = jnp.concatenate([jnp.arange(LP), past + jnp.tile(jnp.arange(LS), BS)])
    cos_t, sin_t = _rotary_tables(pos_all)
    fox_bq = _row_tile(LP, 512)
    n_pairs = fox_heads // FOX_HEADS_PER_STEP
    cache_k = cache_fox_k.reshape(depth * BS, past, fox_w)
    cache_v = cache_fox_v.reshape(depth * BS, past, fox_w)
    qk_scale = HEAD_DIM ** -0.5
    zero_state = jnp.zeros((BP, ret_heads // RET_GROUP_HEADS, RET_GROUP_LANES, RET_GROUP_LANES), F32)
    zero_conv = jnp.zeros((BP, CONV_WIDTH - 1, conv_ch), F32)

    ks, vs, lfs, rs_p, rs_s, cs_p, cs_s = [], [], [], [], [], [], []
    for l in range(depth):
        wl = w_in[l]
        w_ret = jnp.concatenate([wl[:, 0:ret_w][:, perm],
                                 (wl[:, ret_w:2 * ret_w] * qk_scale)[:, perm],
                                 wl[:, 2 * ret_w:4 * ret_w]], axis=1).astype(BF16)
        w_fox = jnp.concatenate([wl[:, off_fox:off_fox + fox_w] * (qk_scale * LOG2E),
                                 wl[:, off_fox + fox_w:off_fgt]], axis=1).astype(BF16)
        w_cf = jnp.concatenate([wl[:, off_conv:], wl[:, off_fgt:off_conv],
                                jnp.zeros((D, LANES - fox_heads), F32)], axis=1).astype(BF16)

        xr = _matmul(x16, w_ret, F32, "ret_proj")
        fq16, fk32, fv32, fk16, fv16, qn2, kn2 = _fox_proj(x16, w_fox)
        u, logf = _conv_fgt_proj(x16, w_cf, b_forget[l], conv_ch, fox_heads)

        o_r_p, sg_p = _retention(xr, cos_t, sin_t, zero_state, 0, TP, LP, ret_heads)
        o_r_s, sg_s = _retention(xr, cos_t, sin_t, _ret_state_to_groups(state_ret[l]), TP, TS, LS, ret_heads)
        rs_p.append(_ret_state_from_groups(sg_p))
        rs_s.append(_ret_state_from_groups(sg_s))

        cw, cb, cg, cbeta = conv_w[l], conv_b[l], conv_ln_g[l], conv_ln_b[l]
        o_c_p = _conformer_conv(u, zero_conv, cw, cb, cg, cbeta, 0, LP)
        o_c_s = _conformer_conv(u, state_conv[l], cw, cb, cg, cbeta, TP, LS)
        u_p = u[:TP].reshape(BP, LP, conv_ch)
        u_s = u[TP:].reshape(BS, LS, conv_ch)
        cs_p.append(u_p[:, LP - (CONV_WIDTH - 1):])
        cs_s.append(jnp.concatenate([state_conv[l], u_s], axis=1)[:, -(CONV_WIDTH - 1):])

        ct_p = _cumsum_time(logf[:TP].reshape(BP, LP, fox_heads))[0] * LOG2E
        o_f_p = _fox_prompt_attention(fq16, fk16, fv16, qn2, kn2, ct_p, LP, fox_bq)

        lf_all = jnp.concatenate([cache_fox_logf[l].astype(F32), logf[TP:].reshape(BS, LS, fox_heads),
                                  jnp.zeros((BS, LANES - LS, fox_heads), F32)], axis=1)
        ct_s = (_cumsum_time(lf_all) * LOG2E).transpose(0, 2, 1)
        ck_s = ct_s.reshape(BS * n_pairs, FOX_HEADS_PER_STEP, past + LANES)
        cq_s = ct_s[:, :, past:past + LS].reshape(BS * n_pairs, FOX_HEADS_PER_STEP * LS, 1)
        o_f_s = _fox_sample_attention(fq16, fk16, fv16, cache_k, cache_v, l, cq_s, ck_s, TP, LS)

        rw = jnp.concatenate([router_group_w[l], router_expert_w[l],
                              jnp.zeros((D, LANES - N_GROUPS - N_EXPERTS), F32)], axis=1).astype(BF16)
        rb = jnp.concatenate([router_group_b[l], router_expert_b[l],
                              jnp.zeros((LANES - N_GROUPS - N_EXPERTS,), F32)]).reshape(1, LANES)
        x1_32, route = _out_proj_router(x32, (o_r_p, o_f_p, o_c_p), (o_r_s, o_f_s, o_c_s),
                                        w_out[l].astype(BF16), ln1_g[l], ln1_b[l], rw, rb, alpha)

        row_token, row_gate, block_expert, n_used, pos = _moe_dispatch(route)
        ys = _moe_experts(x1_32, row_token, row_gate, block_expert, n_used, w_gate, w_up, w_down, l)
        f0 = jnp.take(ys, pos[:, 0], axis=0)
        f1 = jnp.take(ys, pos[:, 1], axis=0)
        x32, x16 = _residual_ln(x1_32, f0, f1, ln2_g[l], ln2_b[l], alpha)

        ks.append(fk32)
        vs.append(fv32)
        lfs.append(logf)

    def split(ts, tail):
        a = jnp.stack(ts)
        return (a[:, :TP].reshape((depth, BP, LP) + tail), a[:, TP:].reshape((depth, BS, LS) + tail))

    pk, sk = split(ks, (fox_heads, HEAD_DIM))
    pv, sv = split(vs, (fox_heads, HEAD_DIM))
    plf, slf = split(lfs, (fox_heads,))
    return (x32[:TP].reshape(BP, LP, D), x32[TP:].reshape(BS, LS, D),
            pk, pv, plf, jnp.stack(rs_p), jnp.stack(cs_p),
            sk, sv, slf, jnp.stack(rs_s), jnp.stack(cs_s))
```

```python
import functools

import numpy as np
import jax
import jax.numpy as jnp
from jax import lax
from jax.experimental import pallas as pl
from jax.experimental.pallas import tpu as pltpu

F32 = jnp.float32
BF16 = jnp.bfloat16

HEAD_DIM = 64
CHUNK = 64
CONV_WIDTH = 31
N_GROUPS = 4
EXPERTS_PER_GROUP = 8
N_EXPERTS = N_GROUPS * EXPERTS_PER_GROUP
ROPE_BASE = 10000.0
LN_EPS = 1e-5
NEG_BIG = -1e30

LANES = 128
MOE_BLOCK_ROWS = 256
VMEM_LIMIT = 56 * 1024 * 1024


def _cparams(*sem):
    return pltpu.CompilerParams(dimension_semantics=sem, vmem_limit_bytes=VMEM_LIMIT)


def _row_tile(n, pref):
    t = min(n, pref)
    while n % t:
        t //= 2
    return t


def _ln_rows(y, g, b):
    mu = jnp.mean(y, axis=-1, keepdims=True)
    yc = y - mu
    var = jnp.mean(yc * yc, axis=-1, keepdims=True)
    return yc * lax.rsqrt(var + LN_EPS) * g + b


def _ln_kernel(n_first, x0_ref, x1_ref, g_ref, b_ref, o32_ref, o16_ref):
    x = jnp.where(pl.program_id(0) < n_first, x0_ref[...], x1_ref[...])
    y = _ln_rows(x, g_ref[...], b_ref[...])
    o32_ref[...] = y
    o16_ref[...] = y.astype(BF16)


def _layer_norm_in(x0, x1, g, b):
    D = x0.shape[1]
    tm = _row_tile(x1.shape[0], 256)
    n0, n1 = x0.shape[0] // tm, x1.shape[0] // tm
    assert x0.shape[0] % tm == 0
    T = (n0 + n1) * tm
    row = pl.BlockSpec((tm, D), lambda i: (i, 0))
    vec = pl.BlockSpec((1, D), lambda i: (0, 0))
    return pl.pallas_call(
        functools.partial(_ln_kernel, n0),
        grid=(n0 + n1,),
        in_specs=[pl.BlockSpec((tm, D), lambda i: (jnp.minimum(i, n0 - 1), 0)),
                  pl.BlockSpec((tm, D), lambda i: (jnp.maximum(i - n0, 0), 0)), vec, vec],
        out_specs=[row, row],
        out_shape=[jax.ShapeDtypeStruct((T, D), F32), jax.ShapeDtypeStruct((T, D), BF16)],
        compiler_params=_cparams("parallel"),
        name="ln_in",
    )(x0, x1, g.reshape(1, D), b.reshape(1, D))


ROUTE_GATE_LANE = 2


def _res_ln_kernel(alpha, x_ref, f0_ref, f1_ref, route_ref, g_ref, b_ref, o32_ref, o16_ref):
    route = route_ref[...]
    f = (route[:, ROUTE_GATE_LANE:ROUTE_GATE_LANE + 1] * f0_ref[...]
         + route[:, ROUTE_GATE_LANE + 1:ROUTE_GATE_LANE + 2] * f1_ref[...])
    y = _ln_rows(alpha * x_ref[...] + f, g_ref[...], b_ref[...])
    o32_ref[...] = y
    o16_ref[...] = y.astype(BF16)


def _residual_ln(x, f0, f1, route, g, b, alpha):
    T, D = x.shape
    tm = _row_tile(T, 256)
    row = pl.BlockSpec((tm, D), lambda i: (i, 0))
    vec = pl.BlockSpec((1, D), lambda i: (0, 0))
    return pl.pallas_call(
        functools.partial(_res_ln_kernel, alpha),
        grid=(T // tm,),
        in_specs=[row, row, row, pl.BlockSpec((tm, LANES), lambda i: (i, 0)), vec, vec],
        out_specs=[row, row],
        out_shape=[jax.ShapeDtypeStruct((T, D), F32), jax.ShapeDtypeStruct((T, D), BF16)],
        compiler_params=_cparams("parallel"),
        name="res_ln",
    )(x, f0, f1, route, g.reshape(1, D), b.reshape(1, D))


def _mm_kernel(x_ref, w_ref, o_ref):
    o_ref[...] = jnp.dot(x_ref[...], w_ref[...], preferred_element_type=F32).astype(o_ref.dtype)


def _matmul(x, w, out_dtype, name):
    T, K = x.shape
    N = w.shape[1]
    tm = _row_tile(T, 1024)
    tn = _row_tile(N, 1024)
    return pl.pallas_call(
        _mm_kernel,
        grid=(T // tm, N // tn),
        in_specs=[pl.BlockSpec((tm, K), lambda i, j: (i, 0)),
                  pl.BlockSpec((K, tn), lambda i, j: (0, j))],
        out_specs=pl.BlockSpec((tm, tn), lambda i, j: (i, j)),
        out_shape=jax.ShapeDtypeStruct((T, N), out_dtype),
        compiler_params=_cparams("parallel", "parallel"),
        name=name,
    )(x, w)


def _fox_proj_kernel(W, x_ref, w_ref, q16_ref, k32_ref, v32_ref, k16_ref, v16_ref, qn_ref, kn_ref):
    x = x_ref[...]
    q16 = jnp.dot(x, w_ref[:, 0:W], preferred_element_type=F32).astype(BF16)
    q16_ref[...] = q16
    k = jnp.dot(x, w_ref[:, W:2 * W], preferred_element_type=F32)
    k16 = k.astype(BF16)
    k32_ref[...] = k
    k16_ref[...] = k16
    v = jnp.dot(x, w_ref[:, 2 * W:3 * W], preferred_element_type=F32)
    v32_ref[...] = v
    v16_ref[...] = v.astype(BF16)

    col_head = lax.broadcasted_iota(jnp.int32, (W, LANES), 0) // HEAD_DIM
    head = lax.broadcasted_iota(jnp.int32, (W, LANES), 1)
    head_sum = (col_head == head).astype(BF16)

    def max_sq_norm(t16):
        t = t16.astype(F32)
        n2 = jnp.dot((t * t).astype(BF16), head_sum, preferred_element_type=F32)
        return jnp.broadcast_to(jnp.max(n2, axis=0, keepdims=True), (8, LANES))

    qn_ref[0] = max_sq_norm(q16)
    kn_ref[0] = max_sq_norm(k16)


def _fox_proj(x16, w_fox):
    T, D = x16.shape
    W = w_fox.shape[1] // 3
    tm = _row_tile(T, 512)
    out = pl.BlockSpec((tm, W), lambda i: (i, 0))
    nrm = pl.BlockSpec((1, 8, LANES), lambda i: (i, 0, 0))
    return pl.pallas_call(
        functools.partial(_fox_proj_kernel, W),
        grid=(T // tm,),
        in_specs=[pl.BlockSpec((tm, D), lambda i: (i, 0)),
                  pl.BlockSpec((D, 3 * W), lambda i: (0, 0))],
        out_specs=[out, out, out, out, out, nrm, nrm],
        out_shape=[jax.ShapeDtypeStruct((T, W), BF16), jax.ShapeDtypeStruct((T, W), F32),
                   jax.ShapeDtypeStruct((T, W), F32), jax.ShapeDtypeStruct((T, W), BF16),
                   jax.ShapeDtypeStruct((T, W), BF16),
                   jax.ShapeDtypeStruct((T // tm, 8, LANES), F32),
                   jax.ShapeDtypeStruct((T // tm, 8, LANES), F32)],
        compiler_params=_cparams("parallel"),
        name="fox_proj",
    )(x16, w_fox)


def _log_sigmoid(z):
    return -(jnp.maximum(-z, 0.0) + jnp.log1p(jnp.exp(-jnp.abs(z))))


def _conv_fgt_proj_kernel(cc, nh, x_ref, w_ref, bf_ref, u_ref, lf_ref):
    x = x_ref[...]
    a = jnp.dot(x, w_ref[:, 0:cc], preferred_element_type=F32)
    gt = jnp.dot(x, w_ref[:, cc:2 * cc], preferred_element_type=F32)
    u_ref[...] = a * jax.nn.sigmoid(gt)
    z = jnp.dot(x, w_ref[:, 2 * cc:2 * cc + LANES], preferred_element_type=F32) + bf_ref[...]
    lf_ref[...] = _log_sigmoid(z)[:, :nh]


def _conv_fgt_proj(x16, w_cf, b_forget, cc, nh):
    T, D = x16.shape
    tm = _row_tile(T, 512)
    nw = w_cf.shape[1]
    bf = jnp.zeros((1, LANES), F32).at[0, :nh].set(b_forget)
    return pl.pallas_call(
        functools.partial(_conv_fgt_proj_kernel, cc, nh),
        grid=(T // tm,),
        in_specs=[pl.BlockSpec((tm, D), lambda i: (i, 0)),
                  pl.BlockSpec((D, nw), lambda i: (0, 0)),
                  pl.BlockSpec((1, LANES), lambda i: (0, 0))],
        out_specs=[pl.BlockSpec((tm, cc), lambda i: (i, 0)),
                   pl.BlockSpec((tm, nh), lambda i: (i, 0))],
        out_shape=[jax.ShapeDtypeStruct((T, cc), F32), jax.ShapeDtypeStruct((T, nh), F32)],
        compiler_params=_cparams("parallel"),
        name="conv_fgt_proj",
    )(x16, w_cf, bf)


def _trunc16(x):
    bits = lax.bitcast_convert_type(x, jnp.uint32) & jnp.uint32(0xFFFF0000)
    return lax.bitcast_convert_type(bits, F32)


def _split3(x):
    hi = _trunc16(x)
    r1 = x - hi
    mid = _trunc16(r1)
    lo = r1 - mid
    return hi.astype(BF16), mid.astype(BF16), lo.astype(BF16)


def _cumsum_kernel(nblk, x_ref, o_ref):
    r = lax.broadcasted_iota(jnp.int32, (LANES, LANES), 0)
    c = lax.broadcasted_iota(jnp.int32, (LANES, LANES), 1)
    tri = (r <= c).astype(BF16)

    def body(j, carry):
        hi, mid, lo = _split3(x_ref[0, j])
        s = (jnp.dot(hi, tri, preferred_element_type=F32)
             + jnp.dot(mid, tri, preferred_element_type=F32)
             + jnp.dot(lo, tri, preferred_element_type=F32)) + carry
        o_ref[0, j] = s
        return s[:, LANES - 1:LANES]

    lax.fori_loop(0, nblk, body, jnp.zeros((x_ref.shape[2], 1), F32))


def _cumsum_time(x):
    n, L, H = x.shape
    assert L % LANES == 0
    nblk = L // LANES
    xt = x.reshape(n, nblk, LANES, H).transpose(0, 1, 3, 2)
    blk = pl.BlockSpec((1, nblk, H, LANES), lambda i: (i, 0, 0, 0))
    out = pl.pallas_call(
        functools.partial(_cumsum_kernel, nblk),
        grid=(n,),
        in_specs=[blk],
        out_specs=blk,
        out_shape=jax.ShapeDtypeStruct((n, nblk, H, LANES), F32),
        compiler_params=_cparams("parallel"),
        name="cumsum_logf",
    )(xt)
    return out.transpose(0, 1, 3, 2).reshape(n, L, H)


RET_GROUP_HEADS = 4
RET_GROUP_LANES = RET_GROUP_HEADS * HEAD_DIM
RET_CHUNKS_PER_STEP = 8


def _ret_perm(n_heads):
    half = HEAD_DIM // 2
    perm = np.zeros(n_heads * HEAD_DIM, np.int32)
    for h in range(n_heads):
        g, hh = divmod(h, RET_GROUP_HEADS)
        for hf in range(2):
            for j in range(half):
                perm[g * RET_GROUP_LANES + hf * LANES + hh * half + j] = h * HEAD_DIM + hf * half + j
    return perm


def _retention_kernel(n_groups, cps, x_ref, cos_ref, sin_ref, inner_ref, qdec_ref, kdec_ref, cdec_ref,
                      s0_ref, o_ref, sout_ref, s_scr, o_scr):
    W = n_groups * RET_GROUP_LANES
    c = CHUNK
    step = pl.program_id(0)
    n_chunks = x_ref.shape[0] // c

    k_lane = lax.broadcasted_iota(jnp.int32, (c, RET_GROUP_LANES), 1)
    k_head = (k_lane % LANES) // (HEAD_DIM // 2)
    v_head = k_lane // HEAD_DIM
    sr = lax.broadcasted_iota(jnp.int32, (RET_GROUP_LANES, RET_GROUP_LANES), 0)
    sc = lax.broadcasted_iota(jnp.int32, (RET_GROUP_LANES, RET_GROUP_LANES), 1)
    s_mask = ((sr % LANES) // (HEAD_DIM // 2)) == (sc // HEAD_DIM)

    for ci in range(n_chunks):
        rows = slice(ci * c, (ci + 1) * c)
        cos = cos_ref[rows, :]
        sin = sin_ref[rows, :]
        if cps == 1:
            seq_local, first, last = ci, True, True
        else:
            seq_local = 0
            first = (step * n_chunks + ci) % cps == 0
            last = (step * n_chunks + ci) % cps == cps - 1
        for g in range(n_groups):
            lo = g * RET_GROUP_LANES

            def rot(t):
                t1, t2 = t[:, :LANES], t[:, LANES:]
                return jnp.concatenate([t1 * cos - t2 * sin, t1 * sin + t2 * cos], axis=1)

            q = rot(x_ref[rows, lo:lo + RET_GROUP_LANES])
            k = rot(x_ref[rows, W + lo:W + lo + RET_GROUP_LANES])
            v = x_ref[rows, 2 * W + lo:2 * W + lo + RET_GROUP_LANES]
            q16 = q.astype(BF16)
            k16 = k.astype(BF16)
            v16 = v.astype(BF16)
            kd16 = (k * kdec_ref[g]).astype(BF16)

            if cps == 1:
                s_prev = s0_ref[seq_local, g]
            else:
                @pl.when(first)
                def _():
                    s_scr[g] = s0_ref[0, g]
                s_prev = s_scr[g]

            zero = jnp.zeros_like(k16)
            k_bd = jnp.concatenate([jnp.where(k_head == hh, k16, zero) for hh in range(RET_GROUP_HEADS)], axis=0)
            v_bd = jnp.concatenate([jnp.where(v_head == hh, v16, zero) for hh in range(RET_GROUP_HEADS)], axis=0)
            s_all = lax.dot_general(q16, k_bd, (((1,), (1,)), ((), ())), preferred_element_type=F32)
            att = (s_all * inner_ref[g]).astype(BF16)
            o_intra = jnp.dot(att, v_bd, preferred_element_type=F32)
            o_cross = jnp.dot(q16, s_prev.astype(BF16), preferred_element_type=F32) * qdec_ref[g]
            o_scr[rows, lo:lo + RET_GROUP_LANES] = o_intra + o_cross

            kv = lax.dot_general(kd16, v16, (((0,), (0,)), ((), ())), preferred_element_type=F32)
            s_new = s_prev * cdec_ref[g] + jnp.where(s_mask, kv, 0.0)
            if cps == 1:
                sout_ref[seq_local, g] = s_new
            else:
                s_scr[g] = s_new

                @pl.when(last)
                def _():
                    sout_ref[0, g] = s_new

    pr = lax.broadcasted_iota(jnp.int32, (W, W), 0) // HEAD_DIM
    pc = lax.broadcasted_iota(jnp.int32, (W, W), 1) // HEAD_DIM
    avg = jnp.where(pr == pc, 1.0 / HEAD_DIM, 0.0).astype(BF16)

    def head_mean(t):
        hi, mid, lo3 = _split3(t)
        return (jnp.dot(hi, avg, preferred_element_type=F32)
                + jnp.dot(mid, avg, preferred_element_type=F32)
                + jnp.dot(lo3, avg, preferred_element_type=F32))

    o = o_scr[...]
    oc = o - head_mean(o)
    var = head_mean(oc * oc)
    gate = x_ref[:, 3 * W:4 * W]
    o_ref[...] = (oc * lax.rsqrt(var + LN_EPS) * (gate * jax.nn.sigmoid(gate))).astype(BF16)


def _retention_tables(n_heads):
    c = CHUNK
    n_groups = n_heads // RET_GROUP_HEADS
    lg = jnp.log1p(-jnp.exp2(-5.0 - jnp.arange(n_heads, dtype=F32)))
    idx = jnp.arange(c, dtype=F32)
    diff = idx[:, None] - idx[None, :]
    causal = diff >= 0
    inner = jnp.where(causal[None], jnp.exp(jnp.where(causal, diff, 0.0)[None] * lg[:, None, None]), 0.0)
    q_dec = jnp.exp((idx + 1.0)[:, None] * lg[None, :])
    k_dec = jnp.exp((c - 1.0 - idx)[:, None] * lg[None, :])
    c_dec = jnp.exp(c * lg)
    inner_t = inner.reshape(n_groups, RET_GROUP_HEADS, c, c).transpose(0, 2, 1, 3).reshape(n_groups, c, RET_GROUP_HEADS * c)
    out_head = np.arange(RET_GROUP_LANES) // HEAD_DIM
    k_head = (np.arange(RET_GROUP_LANES) % LANES) // (HEAD_DIM // 2)
    gh = np.arange(n_groups)[:, None] * RET_GROUP_HEADS
    qdec_t = q_dec[:, gh + out_head[None, :]].transpose(1, 0, 2)
    kdec_t = k_dec[:, gh + k_head[None, :]].transpose(1, 0, 2)
    cdec_t = c_dec[gh + out_head[None, :]][:, None, :]
    return inner_t, qdec_t, kdec_t, cdec_t


def _rotary_tables(pos):
    half = HEAD_DIM // 2
    inv = ROPE_BASE ** (-jnp.arange(half, dtype=F32) / half)
    ang = pos.astype(F32)[:, None] * inv[None, :]
    return (jnp.tile(jnp.cos(ang), (1, LANES // half)), jnp.tile(jnp.sin(ang), (1, LANES // half)))


def _retention(xr, cos, sin, s0g, row0, n_rows, seq_len, n_heads):
    W = n_heads * HEAD_DIM
    n_groups = n_heads // RET_GROUP_HEADS
    nseq = n_rows // seq_len
    cps = seq_len // CHUNK
    tl = RET_CHUNKS_PER_STEP * CHUNK
    assert n_rows % tl == 0 and row0 % tl == 0 and (cps == 1 or cps % RET_CHUNKS_PER_STEP == 0)
    off = row0 // tl
    if cps == 1:
        sblk = pl.BlockSpec((RET_CHUNKS_PER_STEP, n_groups, RET_GROUP_LANES, RET_GROUP_LANES),
                            lambda i: (i, 0, 0, 0))
    else:
        spc = cps // RET_CHUNKS_PER_STEP
        sblk = pl.BlockSpec((1, n_groups, RET_GROUP_LANES, RET_GROUP_LANES), lambda i: (i // spc, 0, 0, 0))
    inner_t, qdec_t, kdec_t, cdec_t = _retention_tables(n_heads)

    def full(a):
        return pl.BlockSpec(a.shape, lambda i: (0,) * a.ndim)

    return pl.pallas_call(
        functools.partial(_retention_kernel, n_groups, cps),
        grid=(n_rows // tl,),
        in_specs=[pl.BlockSpec((tl, 4 * W), lambda i: (i + off, 0)),
                  pl.BlockSpec((tl, LANES), lambda i: (i + off, 0)),
                  pl.BlockSpec((tl, LANES), lambda i: (i + off, 0)),
                  full(inner_t), full(qdec_t), full(kdec_t), full(cdec_t), sblk],
        out_specs=[pl.BlockSpec((tl, W), lambda i: (i, 0)), sblk],
        out_shape=[jax.ShapeDtypeStruct((n_rows, W), BF16),
                   jax.ShapeDtypeStruct((nseq, n_groups, RET_GROUP_LANES, RET_GROUP_LANES), F32)],
        scratch_shapes=[pltpu.VMEM((n_groups, RET_GROUP_LANES, RET_GROUP_LANES), F32),
                        pltpu.VMEM((tl, W), F32)],
        compiler_params=_cparams("arbitrary"),
        name="retention",
    )(xr, cos, sin, inner_t, qdec_t, kdec_t, cdec_t, s0g)


def _ret_state_to_groups(s):
    n, H = s.shape[:2]
    G = H // RET_GROUP_HEADS
    half = HEAD_DIM // 2
    t = s.reshape(n, G, RET_GROUP_HEADS, 2, half, HEAD_DIM).transpose(0, 1, 3, 2, 4, 5)
    eye = jnp.eye(RET_GROUP_HEADS, dtype=s.dtype)
    t = t[:, :, :, :, :, None, :] * eye[None, None, None, :, None, :, None]
    return t.reshape(n, G, RET_GROUP_LANES, RET_GROUP_LANES)


def _ret_state_from_groups(sg):
    n, G = sg.shape[:2]
    half = HEAD_DIM // 2
    t = sg.reshape(n, G, 2, RET_GROUP_HEADS, half, RET_GROUP_HEADS, HEAD_DIM)
    t = jnp.stack([t[:, :, :, hh, :, hh, :] for hh in range(RET_GROUP_HEADS)], axis=2)
    return t.reshape(n, G * RET_GROUP_HEADS, HEAD_DIM, HEAD_DIM)


CONV_HALO = 32


def _conv_kernel(tl, u_ref, prev_ref, w_ref, cb_ref, g_ref, b_ref, o_ref, buf):
    j = pl.program_id(1)

    @pl.when(j == 0)
    def _():
        buf[0:CONV_HALO, :] = prev_ref[0]

    @pl.when(j > 0)
    def _():
        buf[0:CONV_HALO, :] = buf[tl:tl + CONV_HALO, :]

    buf[CONV_HALO:CONV_HALO + tl, :] = u_ref[...]
    off = CONV_HALO - (CONV_WIDTH - 1)
    acc = buf[off:off + tl, :] * w_ref[0:1, :]
    for w in range(1, CONV_WIDTH):
        acc = acc + buf[off + w:off + w + tl, :] * w_ref[w:w + 1, :]
    y = _ln_rows(acc + cb_ref[...], g_ref[...], b_ref[...])
    o_ref[...] = (y * jax.nn.sigmoid(y)).astype(BF16)


def _conformer_conv(u, prev, w, cb, g, b, row0, seq_len):
    C = u.shape[1]
    nseq = prev.shape[0]
    tl = _row_tile(seq_len, 512)
    assert tl >= CONV_HALO and row0 % tl == 0
    off = row0 // tl
    nt = seq_len // tl
    prev_p = jnp.pad(prev.astype(F32), ((0, 0), (CONV_HALO - (CONV_WIDTH - 1), 0), (0, 0)))
    w_p = jnp.pad(w, ((0, CONV_HALO - CONV_WIDTH), (0, 0)))
    vec = pl.BlockSpec((1, C), lambda s, j: (0, 0))
    return pl.pallas_call(
        functools.partial(_conv_kernel, tl),
        grid=(nseq, nt),
        in_specs=[pl.BlockSpec((tl, C), lambda s, j: (off + s * nt + j, 0)),
                  pl.BlockSpec((1, CONV_HALO, C), lambda s, j: (s, 0, 0)),
                  pl.BlockSpec((CONV_HALO, C), lambda s, j: (0, 0)),
                  vec, vec, vec],
        out_specs=pl.BlockSpec((tl, C), lambda s, j: (s * nt + j, 0)),
        out_shape=jax.ShapeDtypeStruct((nseq * seq_len, C), BF16),
        scratch_shapes=[pltpu.VMEM((tl + CONV_HALO, C), F32)],
        compiler_params=_cparams("arbitrary", "arbitrary"),
        name="conformer_conv",
    )(u, prev_p, w_p, cb.reshape(1, C), g.reshape(1, C), b.reshape(1, C))


FOX_HEADS_PER_STEP = 2
LOG2E = 1.4426950408889634
FOX_SKIP_LOG2 = -160.0


def _split3_f32(x):
    hi = _trunc16(x)
    r1 = x - hi
    mid = _trunc16(r1)
    return hi, mid, r1 - mid


def _fox_prompt_kernel(bq, nkb, jlo_ref, q_ref, k_ref, v_ref, cq_ref, ck_ref, o_ref,
                       kt_scr, m_scr, l_scr, acc_scr):
    pair = pl.program_id(0)
    i = pl.program_id(1)
    heads = range(FOX_HEADS_PER_STEP)

    def key_rows(j):
        return pl.ds(pl.multiple_of(j * bq, bq), bq)

    @pl.when(i == 0)
    def _():
        row = lax.broadcasted_iota(jnp.int32, (LANES, bq), 0)
        ones = ((row < 3) | ((row >= 6) & (row < 9))).astype(F32)

        def build(j, carry):
            kt_scr[j, 0:LANES, :] = k_ref[key_rows(j), :].T
            ck = ck_ref[j]
            bias = ones
            for h in heads:
                terms = _split3_f32(-ck[h:h + 1, :])
                for t in range(3):
                    bias = jnp.where(row == 6 * h + 3 + t, terms[t], bias)
            kt_scr[j, LANES:2 * LANES, :] = bias.astype(BF16)
            return carry

        lax.fori_loop(0, nkb, build, 0)

    q = q_ref[...]
    lane = lax.broadcasted_iota(jnp.int32, (bq, LANES), 1)
    cq = cq_ref[...]
    qs = []
    for h in heads:
        terms = _split3_f32(cq[:, h:h + 1])
        bias = ((lane >= 6 * h + 3) & (lane < 6 * h + 6)).astype(F32)
        for t in range(3):
            bias = jnp.where(lane == 6 * h + t, terms[t], bias)
        in_head = (lane < HEAD_DIM) if h == 0 else (lane >= HEAD_DIM)
        qs.append(jnp.concatenate([jnp.where(in_head, q, jnp.zeros_like(q)), bias.astype(BF16)], axis=1))
        m_scr[h] = jnp.full((bq, LANES), NEG_BIG, F32)
        l_scr[h] = jnp.zeros((bq, LANES), F32)
        acc_scr[h] = jnp.zeros((bq, LANES), F32)

    def update(h, s, v):
        m_prev = m_scr[h]
        m_new = jnp.maximum(m_prev, jnp.max(s, axis=1, keepdims=True))
        p = jnp.exp2(s - jnp.concatenate([m_new] * (bq // LANES), axis=1))
        alpha = jnp.exp2(m_prev - m_new)
        l_scr[h] = alpha * l_scr[h] + jnp.sum(p, axis=1, keepdims=True)
        acc_scr[h] = alpha * acc_scr[h] + jnp.dot(p.astype(BF16), v, preferred_element_type=F32)
        m_scr[h] = m_new

    def scores(h, j):
        return jnp.dot(qs[h], kt_scr[j], preferred_element_type=F32)

    def body(j, carry):
        s = [scores(h, j) for h in heads]
        v = v_ref[key_rows(j), :]
        for h in heads:
            update(h, s[h], v)
        return carry

    lax.fori_loop(jlo_ref[pair * nkb + i], i, body, 0)

    r = lax.broadcasted_iota(jnp.int32, (bq, bq), 0)
    c = lax.broadcasted_iota(jnp.int32, (bq, bq), 1)
    s = [scores(h, i) for h in heads]
    v = v_ref[key_rows(i), :]
    for h in heads:
        update(h, jnp.where(c <= r, s[h], NEG_BIG), v)

    o_ref[...] = jnp.where(lane < HEAD_DIM, acc_scr[0] / l_scr[0], acc_scr[1] / l_scr[1]).astype(BF16)


def _fox_skip(qn2, kn2, ct, n_heads, bq):
    L = ct.shape[0]
    nb = L // bq
    qn = jnp.sqrt(qn2[:nb, 0, :n_heads])
    kn = jnp.sqrt(kn2[:nb, 0, :n_heads])
    kmax = jnp.max(kn, axis=0, keepdims=True)
    ctb = ct.reshape(nb, bq, n_heads)
    bound = (1.02 * qn * (kmax + kn) + ctb[:, 0, :])[:, None, :] - ctb[None, :, bq - 1, :]
    skip = (bound < FOX_SKIP_LOG2) & (jnp.arange(nb)[None, :, None] < jnp.arange(nb)[:, None, None])
    jlo = jnp.sum(jnp.cumprod(skip.astype(jnp.int32), axis=1), axis=1)
    jlo = jnp.min(jlo.reshape(nb, n_heads // FOX_HEADS_PER_STEP, FOX_HEADS_PER_STEP), axis=2)
    return jlo.T.reshape(-1).astype(jnp.int32)


def _fox_prompt_attention(q16, k16, v16, qn2, kn2, ct, L, bq):
    W = q16.shape[1]
    n_heads = W // HEAD_DIM
    hp = FOX_HEADS_PER_STEP
    n_pairs = n_heads // hp
    nkb = L // bq
    jlo = _fox_skip(qn2, kn2, ct, n_heads, bq)
    cq = ct.reshape(L, n_pairs, hp).transpose(1, 0, 2)
    ck = ct.reshape(nkb, bq, n_pairs, hp).transpose(2, 0, 3, 1)
    grid_spec = pltpu.PrefetchScalarGridSpec(
        num_scalar_prefetch=1,
        grid=(n_pairs, nkb),
        in_specs=[pl.BlockSpec((bq, LANES), lambda p, i, jl: (i, p)),
                  pl.BlockSpec((L, LANES), lambda p, i, jl: (0, p)),
                  pl.BlockSpec((L, LANES), lambda p, i, jl: (0, p)),
                  pl.BlockSpec((None, bq, hp), lambda p, i, jl: (p, i, 0)),
                  pl.BlockSpec((None, nkb, hp, bq), lambda p, i, jl: (p, 0, 0, 0))],
        out_specs=pl.BlockSpec((bq, LANES), lambda p, i, jl: (i, p)),
        scratch_shapes=[pltpu.VMEM((nkb, 2 * LANES, bq), BF16),
                        pltpu.VMEM((hp, bq, LANES), F32), pltpu.VMEM((hp, bq, LANES), F32),
                        pltpu.VMEM((hp, bq, LANES), F32)],
    )
    return pl.pallas_call(
        functools.partial(_fox_prompt_kernel, bq, nkb),
        grid_spec=grid_spec,
        out_shape=jax.ShapeDtypeStruct((L, W), BF16),
        compiler_params=_cparams("arbitrary", "arbitrary"),
        name="fox_prompt",
    )(jlo, q16, k16, v16, cq, ck)


def _fox_sample_kernel(past, ls, q_ref, kn_ref, vn_ref, kc_ref, vc_ref, cq_ref, ck_ref, o_ref):
    q = q_ref[...]
    lane = lax.broadcasted_iota(jnp.int32, q.shape, 1)
    zero = jnp.zeros_like(q)
    q2 = jnp.concatenate([jnp.where(lane < HEAD_DIM, q, zero), jnp.where(lane >= HEAD_DIM, q, zero)], axis=0)
    pad = jnp.zeros((LANES - ls, LANES), BF16)
    k_all = jnp.concatenate([kc_ref[0].astype(BF16), kn_ref[...], pad], axis=0)
    v_all = jnp.concatenate([vc_ref[0].astype(BF16), vn_ref[...], pad], axis=0)
    nk = past + LANES
    s = lax.dot_general(q2, k_all, (((1,), (1,)), ((), ())), preferred_element_type=F32)
    row = lax.broadcasted_iota(jnp.int32, (2 * ls, nk), 0)
    col = lax.broadcasted_iota(jnp.int32, (2 * ls, nk), 1)
    ck = jnp.where(row < ls, ck_ref[0, 0:1, :], ck_ref[0, 1:2, :])
    visible = (col < past) | (col - past <= row % ls)
    s = jnp.where(visible, s + cq_ref[0] - ck, NEG_BIG)
    p = jnp.exp2(s - jnp.max(s, axis=1, keepdims=True))
    den = jnp.sum(p, axis=1, keepdims=True)
    o2 = jnp.dot(p.astype(BF16), v_all, preferred_element_type=F32) / den
    o_ref[...] = jnp.where(lane < HEAD_DIM, o2[0:ls], o2[ls:2 * ls]).astype(BF16)


def _fox_sample_attention(q16, k16, v16, cache_k, cache_v, layer, cq, ck, row0, ls):
    W = q16.shape[1]
    past = cache_k.shape[1]
    n_pairs = W // LANES
    B = cq.shape[0] // n_pairs
    assert ls <= LANES and row0 % ls == 0
    off = row0 // ls
    rows = pl.BlockSpec((ls, LANES), lambda b, p: (off + b, p))
    cache = pl.BlockSpec((1, past, LANES), lambda b, p: (layer * B + b, 0, p))
    return pl.pallas_call(
        functools.partial(_fox_sample_kernel, past, ls),
        grid=(B, n_pairs),
        in_specs=[rows, rows, rows, cache, cache,
                  pl.BlockSpec((1, 2 * ls, 1), lambda b, p: (b * n_pairs + p, 0, 0)),
                  pl.BlockSpec((1, 2, past + LANES), lambda b, p: (b * n_pairs + p, 0, 0))],
        out_specs=pl.BlockSpec((ls, LANES), lambda b, p: (b, p)),
        out_shape=jax.ShapeDtypeStruct((B * ls, W), BF16),
        compiler_params=_cparams("parallel", "parallel"),
        name="fox_sample",
    )(q16, k16, v16, cache_k, cache_v, cq, ck)


def _out_router_kernel(alpha, n_first, x_ref, r0_ref, f0_ref, c0_ref, r1_ref, f1_ref, c1_ref,
                       w_ref, g_ref, b_ref, rw_ref, rb_ref, o32_ref, otok_ref, route_ref):
    first = pl.program_id(0) < n_first
    mixed = jnp.concatenate([jnp.where(first, r0_ref[...], r1_ref[...]),
                             jnp.where(first, f0_ref[...], f1_ref[...]),
                             jnp.where(first, c0_ref[...], c1_ref[...])], axis=1)
    m = jnp.dot(mixed, w_ref[...], preferred_element_type=F32)
    y = _ln_rows(alpha * x_ref[...] + m, g_ref[...], b_ref[...])
    o32_ref[...] = y
    tm, d = y.shape
    rpt = d // LANES
    for c in range(rpt):
        otok_ref[pl.ds(c, tm, stride=rpt), :] = y[:, c * LANES:(c + 1) * LANES]

    logits = jnp.dot(y.astype(BF16), rw_ref[...], preferred_element_type=F32) + rb_ref[...]
    lane = lax.broadcasted_iota(jnp.int32, logits.shape, 1).astype(F32)
    far = float(LANES)

    def rmax(t):
        return jnp.max(t, axis=1, keepdims=True)

    def rmin(t):
        return jnp.min(t, axis=1, keepdims=True)

    def rsum(t):
        return jnp.sum(t, axis=1, keepdims=True)

    gmask = lane < N_GROUPS
    lg = jnp.where(gmask, logits, -jnp.inf)
    mg = rmax(lg)
    p_grp = 1.0 / rsum(jnp.exp(lg - mg))
    g_idx = rmin(jnp.where(lg == mg, lane, far))
    e_lo = N_GROUPS + EXPERTS_PER_GROUP * g_idx
    emask = (lane >= e_lo) & (lane < e_lo + EXPERTS_PER_GROUP)
    le = jnp.where(emask, logits, -jnp.inf)
    ee = jnp.exp(le - rmax(le))
    pin = jnp.where(emask, ee / rsum(ee), -1.0)
    p1 = rmax(pin)
    i1 = rmin(jnp.where(pin == p1, lane, far))
    pin2 = jnp.where(lane == i1, -1.0, pin)
    p2 = rmax(pin2)
    i2 = rmin(jnp.where(pin2 == p2, lane, far))
    den = p1 + p2
    route_ref[...] = jnp.where(lane == 0, i1 - N_GROUPS,
                     jnp.where(lane == 1, i2 - N_GROUPS,
                     jnp.where(lane == 2, p_grp * p1 / den,
                     jnp.where(lane == 3, p_grp * p2 / den, 0.0))))


def _out_proj_router(x, mix0, mix1, w_out16, g, b, rw16, rb, alpha):
    T, D = x.shape
    tm = _row_tile(mix1[0].shape[0], 256)
    n0 = mix0[0].shape[0] // tm
    n1 = mix1[0].shape[0] // tm
    assert mix0[0].shape[0] % tm == 0 and (n0 + n1) * tm == T

    def rows(a):
        return pl.BlockSpec((tm, a.shape[1]), lambda i: (i, 0))

    def rows0(a):
        return pl.BlockSpec((tm, a.shape[1]), lambda i: (jnp.minimum(i, n0 - 1), 0))

    def rows1(a):
        return pl.BlockSpec((tm, a.shape[1]), lambda i: (jnp.maximum(i - n0, 0), 0))

    def full(a):
        return pl.BlockSpec(a.shape, lambda i: (0, 0))

    g2, b2 = g.reshape(1, D), b.reshape(1, D)
    return pl.pallas_call(
        functools.partial(_out_router_kernel, alpha, n0),
        grid=(T // tm,),
        in_specs=[rows(x)] + [rows0(a) for a in mix0] + [rows1(a) for a in mix1]
                 + [full(w_out16), full(g2), full(b2), full(rw16), full(rb)],
        out_specs=[pl.BlockSpec((tm, D), lambda i: (i, 0)),
                   pl.BlockSpec((tm * (D // LANES), LANES), lambda i: (i, 0)),
                   pl.BlockSpec((tm, LANES), lambda i: (i, 0))],
        out_shape=[jax.ShapeDtypeStruct((T, D), F32), jax.ShapeDtypeStruct((T * (D // LANES), LANES), F32),
                   jax.ShapeDtypeStruct((T, LANES), F32)],
        compiler_params=_cparams("parallel"),
        name="out_proj_router",
    )(x, *mix0, *mix1, w_out16, g2, b2, rw16, rb)


def _moe_kernel(bm, rpt, be_ref, nused_ref, base_ref, tok_ref, x_hbm, wg_ref, wu_ref, wd_ref, y_ref,
                wg16, wu16, wd16, xbuf, sem):
    blk = pl.program_id(0)
    n_used = nused_ref[0]
    used = blk < n_used
    slot = blk % 2
    prev_expert = be_ref[jnp.maximum(blk - 1, 0)]

    def start_gather(of_blk, to_slot):
        base = base_ref[of_blk]
        for r in range(bm):
            src = pl.ds(pl.multiple_of(tok_ref[base + r] * rpt, rpt), rpt)
            pltpu.make_async_copy(x_hbm.at[src], xbuf.at[to_slot, pl.ds(r * rpt, rpt)],
                                  sem.at[to_slot]).start()

    def wait_gather():
        pltpu.make_async_copy(x_hbm.at[pl.ds(0, bm * rpt)], xbuf.at[slot], sem.at[slot]).wait()

    @pl.when(blk == 0)
    def _():
        start_gather(0, 0)

    @pl.when(used & ((blk == 0) | (be_ref[blk] != prev_expert)))
    def _():
        wg16[...] = wg_ref[0].astype(BF16)
        wu16[...] = wu_ref[0].astype(BF16)
        wd16[...] = wd_ref[0].astype(BF16)

    @pl.when(blk == n_used)
    def _():
        wait_gather()

    @pl.when(used)
    def _():
        wait_gather()
        start_gather(blk + 1, 1 - slot)
        xs = xbuf.at[slot]
        x = jnp.concatenate([xs[pl.ds(c, bm, stride=rpt), :] for c in range(rpt)], axis=1).astype(BF16)
        a = jnp.dot(x, wg16[...], preferred_element_type=F32)
        u = jnp.dot(x, wu16[...], preferred_element_type=F32)
        h = (a * jax.nn.sigmoid(a) * u).astype(BF16)
        y_ref[...] = jnp.dot(h, wd16[...], preferred_element_type=F32)

    @pl.when(jnp.logical_not(used))
    def _():
        y_ref[...] = jnp.zeros_like(y_ref)


def _moe_experts(xt, tok_sorted, base, block_expert, n_used, wg, wu, wd, layer, n_rows):
    D, DE = wg.shape[2], wg.shape[3]
    rpt = D // LANES
    bm = MOE_BLOCK_ROWS
    n_blk = n_rows // bm

    def weights(rows, cols):
        return pl.BlockSpec((None, 1, rows, cols), lambda i, be, nu, ba, tk: (layer, be[i], 0, 0))

    grid_spec = pltpu.PrefetchScalarGridSpec(
        num_scalar_prefetch=4,
        grid=(n_blk,),
        in_specs=[pl.BlockSpec(memory_space=pl.ANY), weights(D, DE), weights(D, DE), weights(DE, D)],
        out_specs=pl.BlockSpec((bm, D), lambda i, be, nu, ba, tk: (i, 0)),
        scratch_shapes=[pltpu.VMEM((D, DE), BF16), pltpu.VMEM((D, DE), BF16), pltpu.VMEM((DE, D), BF16),
                        pltpu.VMEM((2, bm * rpt, LANES), F32), pltpu.SemaphoreType.DMA((2,))],
    )
    return pl.pallas_call(
        functools.partial(_moe_kernel, bm, rpt),
        grid_spec=grid_spec,
        out_shape=jax.ShapeDtypeStruct((n_rows, D), F32),
        compiler_params=_cparams("arbitrary"),
        name="moe_experts",
    )(block_expert, n_used, base, tok_sorted, xt, wg, wu, wd)


def _moe_dispatch(route):
    T = route.shape[0]
    bm = MOE_BLOCK_ROWS
    eid = route[:, 0:2].astype(jnp.int32).reshape(-1)
    n_assign = 2 * T
    n_rows = n_assign + N_EXPERTS * bm
    n_blocks = n_rows // bm
    eid_s, a_s = lax.sort((eid, jnp.arange(n_assign, dtype=jnp.int32)), num_keys=1, is_stable=True)
    onehot_s = (eid_s[:, None] == jnp.arange(N_EXPERTS)[None, :]).astype(F32)
    exact = lax.Precision.HIGHEST
    counts = jnp.dot(jnp.ones((1, n_assign), F32), onehot_s, precision=exact)[0].astype(jnp.int32)
    padded = ((counts + bm - 1) // bm) * bm
    ends = jnp.cumsum(padded)
    starts = ends - padded
    first = jnp.cumsum(counts) - counts
    shift = (starts - first).astype(F32)
    pos_s = jnp.arange(n_assign, dtype=jnp.int32) + jnp.dot(onehot_s, shift, precision=exact).astype(jnp.int32)
    _, pos = lax.sort((a_s, pos_s), num_keys=1)
    n_used = ends[-1] // bm
    blk_start = jnp.arange(n_blocks, dtype=jnp.int32) * bm
    block_expert = jnp.sum((ends[None, :] <= blk_start[:, None]).astype(jnp.int32), axis=1)
    block_expert = jnp.minimum(block_expert, N_EXPERTS - 1)
    last_used = block_expert[jnp.maximum(n_used - 1, 0)]
    block_expert = jnp.where(jnp.arange(n_blocks) < n_used, block_expert, last_used)
    base = jnp.clip(blk_start - starts[block_expert] + first[block_expert], 0, n_assign)
    tok_sorted = jnp.concatenate([a_s // 2, jnp.zeros((bm,), jnp.int32)])
    return tok_sorted, base, block_expert, n_used.reshape(1).astype(jnp.int32), pos.reshape(T, 2), n_rows


def kernel(x_prompt, x_sample, cache_fox_k, cache_fox_v, cache_fox_logf, state_ret, state_conv, ln_in_g, ln_in_b, w_in, b_forget, conv_w, conv_b, conv_ln_g, conv_ln_b, w_out, ln1_g, ln1_b, router_group_w, router_group_b, router_expert_w, router_expert_b, w_gate, w_up, w_down, ln2_g, ln2_b):
    BP, LP, D = x_prompt.shape
    BS, LS, _ = x_sample.shape
    depth = w_in.shape[0]
    past = cache_fox_k.shape[2]
    fox_heads = cache_fox_k.shape[3]
    ret_heads = state_ret.shape[2]
    conv_ch = state_conv.shape[3]
    ret_w = ret_heads * HEAD_DIM
    fox_w = fox_heads * HEAD_DIM
    off_fox = 4 * ret_w
    off_fgt = off_fox + 3 * fox_w
    off_conv = off_fgt + fox_heads
    alpha = (2 * depth) ** 0.25
    TP, TS = BP * LP, BS * LS
    assert BP == 1 and LS == CHUNK and past % 512 == 0

    x32, x16 = _layer_norm_in(x_prompt.reshape(TP, D), x_sample.reshape(TS, D), ln_in_g, ln_in_b)

    perm = _ret_perm(ret_heads)
    pos_all = jnp.concatenate([jnp.arange(LP), past + jnp.tile(jnp.arange(LS), BS)])
    cos_t, sin_t = _rotary_tables(pos_all)
    fox_bq = _row_tile(LP, 512)
    n_pairs = fox_heads // FOX_HEADS_PER_STEP
    cache_k = cache_fox_k.reshape(depth * BS, past, fox_w)
    cache_v = cache_fox_v.reshape(depth * BS, past, fox_w)
    qk_scale = HEAD_DIM ** -0.5
    zero_state = jnp.zeros((BP, ret_heads // RET_GROUP_HEADS, RET_GROUP_LANES, RET_GROUP_LANES), F32)
    zero_conv = jnp.zeros((BP, CONV_WIDTH - 1, conv_ch), F32)

    ks, vs, lfs, rs_p, rs_s, cs_p, cs_s = [], [], [], [], [], [], []
    for l in range(depth):
        wl = w_in[l]
        w_ret = jnp.concatenate([wl[:, 0:ret_w][:, perm],
                                 (wl[:, ret_w:2 * ret_w] * qk_scale)[:, perm],
                                 wl[:, 2 * ret_w:4 * ret_w]], axis=1).astype(BF16)
        w_fox = jnp.concatenate([wl[:, off_fox:off_fox + fox_w] * (qk_scale * LOG2E),
                                 wl[:, off_fox + fox_w:off_fgt]], axis=1).astype(BF16)
        w_cf = jnp.concatenate([wl[:, off_conv:], wl[:, off_fgt:off_conv],
                                jnp.zeros((D, LANES - fox_heads), F32)], axis=1).astype(BF16)

        xr = _matmul(x16, w_ret, F32, "ret_proj")
        fq16, fk32, fv32, fk16, fv16, qn2, kn2 = _fox_proj(x16, w_fox)
        u, logf = _conv_fgt_proj(x16, w_cf, b_forget[l], conv_ch, fox_heads)

        o_r_p, sg_p = _retention(xr, cos_t, sin_t, zero_state, 0, TP, LP, ret_heads)
        o_r_s, sg_s = _retention(xr, cos_t, sin_t, _ret_state_to_groups(state_ret[l]), TP, TS, LS, ret_heads)
        rs_p.append(_ret_state_from_groups(sg_p))
        rs_s.append(_ret_state_from_groups(sg_s))

        cw, cb, cg, cbeta = conv_w[l], conv_b[l], conv_ln_g[l], conv_ln_b[l]
        o_c_p = _conformer_conv(u, zero_conv, cw, cb, cg, cbeta, 0, LP)
        o_c_s = _conformer_conv(u, state_conv[l], cw, cb, cg, cbeta, TP, LS)
        u_p = u[:TP].reshape(BP, LP, conv_ch)
        u_s = u[TP:].reshape(BS, LS, conv_ch)
        cs_p.append(u_p[:, LP - (CONV_WIDTH - 1):])
        cs_s.append(jnp.concatenate([state_conv[l], u_s], axis=1)[:, -(CONV_WIDTH - 1):])

        ct_p = _cumsum_time(logf[:TP].reshape(BP, LP, fox_heads))[0] * LOG2E
        o_f_p = _fox_prompt_attention(fq16, fk16, fv16, qn2, kn2, ct_p, LP, fox_bq)

        lf_all = jnp.concatenate([cache_fox_logf[l].astype(F32), logf[TP:].reshape(BS, LS, fox_heads),
                                  jnp.zeros((BS, LANES - LS, fox_heads), F32)], axis=1)
        ct_s = (_cumsum_time(lf_all) * LOG2E).transpose(0, 2, 1)
        ck_s = ct_s.reshape(BS * n_pairs, FOX_HEADS_PER_STEP, past + LANES)
        cq_s = ct_s[:, :, past:past + LS].reshape(BS * n_pairs, FOX_HEADS_PER_STEP * LS, 1)
        o_f_s = _fox_sample_attention(fq16, fk16, fv16, cache_k, cache_v, l, cq_s, ck_s, TP, LS)

        rw = jnp.concatenate([router_group_w[l], router_expert_w[l],
                              jnp.zeros((D, LANES - N_GROUPS - N_EXPERTS), F32)], axis=1).astype(BF16)
        rb = jnp.concatenate([router_group_b[l], router_expert_b[l],
                              jnp.zeros((LANES - N_GROUPS - N_EXPERTS,), F32)]).reshape(1, LANES)
        x1_32, x1_tok, route = _out_proj_router(x32, (o_r_p, o_f_p, o_c_p), (o_r_s, o_f_s, o_c_s),
                                                w_out[l].astype(BF16), ln1_g[l], ln1_b[l], rw, rb, alpha)

        tok_sorted, base, block_expert, n_used, pos, n_rows = _moe_dispatch(route)
        ys = _moe_experts(x1_tok, tok_sorted, base, block_expert, n_used, w_gate, w_up, w_down, l, n_rows)
        f0 = jnp.take(ys, pos[:, 0], axis=0)
        f1 = jnp.take(ys, pos[:, 1], axis=0)
        x32, x16 = _residual_ln(x1_32, f0, f1, route, ln2_g[l], ln2_b[l], alpha)

        ks.append(fk32)
        vs.append(fv32)
        lfs.append(logf)

    def split(ts, tail):
        a = jnp.stack(ts)
        return (a[:, :TP].reshape((depth, BP, LP) + tail), a[:, TP:].reshape((depth, BS, LS) + tail))

    pk, sk = split(ks, (fox_heads, HEAD_DIM))
    pv, sv = split(vs, (fox_heads, HEAD_DIM))
    plf, slf = split(lfs, (fox_heads,))
    return (x32[:TP].reshape(BP, LP, D), x32[TP:].reshape(BS, LS, D),
            pk, pv, plf, jnp.stack(rs_p), jnp.stack(cs_p),
            sk, sv, slf, jnp.stack(rs_s), jnp.stack(cs_s))
```

```python
import functools

import numpy as np
import jax
import jax.numpy as jnp
from jax import lax
from jax.experimental import pallas as pl
from jax.experimental.pallas import tpu as pltpu

F32 = jnp.float32
BF16 = jnp.bfloat16

HEAD_DIM = 64
CHUNK = 64
CONV_WIDTH = 31
N_GROUPS = 4
EXPERTS_PER_GROUP = 8
N_EXPERTS = N_GROUPS * EXPERTS_PER_GROUP
ROPE_BASE = 10000.0
LN_EPS = 1e-5
NEG_BIG = -1e30

LANES = 128
MOE_BLOCK_ROWS = 256
MOE_GATHER_SLOTS = 3
VMEM_LIMIT = 56 * 1024 * 1024


def _cparams(*sem):
    return pltpu.CompilerParams(dimension_semantics=sem, vmem_limit_bytes=VMEM_LIMIT)


def _row_tile(n, pref):
    t = min(n, pref)
    while n % t:
        t //= 2
    return t


def _ln_rows(y, g, b):
    mu = jnp.mean(y, axis=-1, keepdims=True)
    yc = y - mu
    var = jnp.mean(yc * yc, axis=-1, keepdims=True)
    return yc * lax.rsqrt(var + LN_EPS) * g + b


def _ln_kernel(n_first, x0_ref, x1_ref, g_ref, b_ref, o32_ref, o16_ref):
    x = jnp.where(pl.program_id(0) < n_first, x0_ref[...], x1_ref[...])
    y = _ln_rows(x, g_ref[...], b_ref[...])
    o32_ref[...] = y
    o16_ref[...] = y.astype(BF16)


def _layer_norm_in(x0, x1, g, b):
    D = x0.shape[1]
    tm = _row_tile(x1.shape[0], 256)
    n0, n1 = x0.shape[0] // tm, x1.shape[0] // tm
    assert x0.shape[0] % tm == 0
    T = (n0 + n1) * tm
    row = pl.BlockSpec((tm, D), lambda i: (i, 0))
    vec = pl.BlockSpec((1, D), lambda i: (0, 0))
    return pl.pallas_call(
        functools.partial(_ln_kernel, n0),
        grid=(n0 + n1,),
        in_specs=[pl.BlockSpec((tm, D), lambda i: (jnp.minimum(i, n0 - 1), 0)),
                  pl.BlockSpec((tm, D), lambda i: (jnp.maximum(i - n0, 0), 0)), vec, vec],
        out_specs=[row, row],
        out_shape=[jax.ShapeDtypeStruct((T, D), F32), jax.ShapeDtypeStruct((T, D), BF16)],
        compiler_params=_cparams("parallel"),
        name="ln_in",
    )(x0, x1, g.reshape(1, D), b.reshape(1, D))


ROUTE_GATE_LANE = 2


def _res_ln_kernel(alpha, x_ref, f0_ref, f1_ref, route_ref, g_ref, b_ref, o32_ref, o16_ref):
    route = route_ref[...]
    f = (route[:, ROUTE_GATE_LANE:ROUTE_GATE_LANE + 1] * f0_ref[...]
         + route[:, ROUTE_GATE_LANE + 1:ROUTE_GATE_LANE + 2] * f1_ref[...])
    y = _ln_rows(alpha * x_ref[...] + f, g_ref[...], b_ref[...])
    o32_ref[...] = y
    o16_ref[...] = y.astype(BF16)


def _residual_ln(x, f0, f1, route, g, b, alpha):
    T, D = x.shape
    tm = _row_tile(T, 256)
    row = pl.BlockSpec((tm, D), lambda i: (i, 0))
    vec = pl.BlockSpec((1, D), lambda i: (0, 0))
    return pl.pallas_call(
        functools.partial(_res_ln_kernel, alpha),
        grid=(T // tm,),
        in_specs=[row, row, row, pl.BlockSpec((tm, LANES), lambda i: (i, 0)), vec, vec],
        out_specs=[row, row],
        out_shape=[jax.ShapeDtypeStruct((T, D), F32), jax.ShapeDtypeStruct((T, D), BF16)],
        compiler_params=_cparams("parallel"),
        name="res_ln",
    )(x, f0, f1, route, g.reshape(1, D), b.reshape(1, D))


def _mm_kernel(x_ref, w_ref, o_ref):
    o_ref[...] = jnp.dot(x_ref[...], w_ref[...], preferred_element_type=F32).astype(o_ref.dtype)


def _matmul(x, w, out_dtype, name):
    T, K = x.shape
    N = w.shape[1]
    tm = _row_tile(T, 1024)
    tn = _row_tile(N, 1024)
    return pl.pallas_call(
        _mm_kernel,
        grid=(T // tm, N // tn),
        in_specs=[pl.BlockSpec((tm, K), lambda i, j: (i, 0)),
                  pl.BlockSpec((K, tn), lambda i, j: (0, j))],
        out_specs=pl.BlockSpec((tm, tn), lambda i, j: (i, j)),
        out_shape=jax.ShapeDtypeStruct((T, N), out_dtype),
        compiler_params=_cparams("parallel", "parallel"),
        name=name,
    )(x, w)


def _fox_proj_kernel(W, x_ref, w_ref, q16_ref, k32_ref, v32_ref, k16_ref, v16_ref, qn_ref, kn_ref):
    x = x_ref[...]
    q16 = jnp.dot(x, w_ref[:, 0:W], preferred_element_type=F32).astype(BF16)
    q16_ref[...] = q16
    k = jnp.dot(x, w_ref[:, W:2 * W], preferred_element_type=F32)
    k16 = k.astype(BF16)
    k32_ref[...] = k
    k16_ref[...] = k16
    v = jnp.dot(x, w_ref[:, 2 * W:3 * W], preferred_element_type=F32)
    v32_ref[...] = v
    v16_ref[...] = v.astype(BF16)

    col_head = lax.broadcasted_iota(jnp.int32, (W, LANES), 0) // HEAD_DIM
    head = lax.broadcasted_iota(jnp.int32, (W, LANES), 1)
    head_sum = (col_head == head).astype(BF16)

    def max_sq_norm(t16):
        t = t16.astype(F32)
        n2 = jnp.dot((t * t).astype(BF16), head_sum, preferred_element_type=F32)
        return jnp.broadcast_to(jnp.max(n2, axis=0, keepdims=True), (8, LANES))

    qn_ref[0] = max_sq_norm(q16)
    kn_ref[0] = max_sq_norm(k16)


def _fox_proj(x16, w_fox):
    T, D = x16.shape
    W = w_fox.shape[1] // 3
    tm = _row_tile(T, 512)
    out = pl.BlockSpec((tm, W), lambda i: (i, 0))
    nrm = pl.BlockSpec((1, 8, LANES), lambda i: (i, 0, 0))
    return pl.pallas_call(
        functools.partial(_fox_proj_kernel, W),
        grid=(T // tm,),
        in_specs=[pl.BlockSpec((tm, D), lambda i: (i, 0)),
                  pl.BlockSpec((D, 3 * W), lambda i: (0, 0))],
        out_specs=[out, out, out, out, out, nrm, nrm],
        out_shape=[jax.ShapeDtypeStruct((T, W), BF16), jax.ShapeDtypeStruct((T, W), F32),
                   jax.ShapeDtypeStruct((T, W), F32), jax.ShapeDtypeStruct((T, W), BF16),
                   jax.ShapeDtypeStruct((T, W), BF16),
                   jax.ShapeDtypeStruct((T // tm, 8, LANES), F32),
                   jax.ShapeDtypeStruct((T // tm, 8, LANES), F32)],
        compiler_params=_cparams("parallel"),
        name="fox_proj",
    )(x16, w_fox)


def _log_sigmoid(z):
    return -(jnp.maximum(-z, 0.0) + jnp.log1p(jnp.exp(-jnp.abs(z))))


def _conv_fgt_proj_kernel(cc, nh, x_ref, w_ref, bf_ref, u_ref, lf_ref):
    x = x_ref[...]
    a = jnp.dot(x, w_ref[:, 0:cc], preferred_element_type=F32)
    gt = jnp.dot(x, w_ref[:, cc:2 * cc], preferred_element_type=F32)
    u_ref[...] = a * jax.nn.sigmoid(gt)
    z = jnp.dot(x, w_ref[:, 2 * cc:2 * cc + LANES], preferred_element_type=F32) + bf_ref[...]
    lf_ref[...] = _log_sigmoid(z)[:, :nh]


def _conv_fgt_proj(x16, w_cf, b_forget, cc, nh):
    T, D = x16.shape
    tm = _row_tile(T, 512)
    nw = w_cf.shape[1]
    bf = jnp.zeros((1, LANES), F32).at[0, :nh].set(b_forget)
    return pl.pallas_call(
        functools.partial(_conv_fgt_proj_kernel, cc, nh),
        grid=(T // tm,),
        in_specs=[pl.BlockSpec((tm, D), lambda i: (i, 0)),
                  pl.BlockSpec((D, nw), lambda i: (0, 0)),
                  pl.BlockSpec((1, LANES), lambda i: (0, 0))],
        out_specs=[pl.BlockSpec((tm, cc), lambda i: (i, 0)),
                   pl.BlockSpec((tm, nh), lambda i: (i, 0))],
        out_shape=[jax.ShapeDtypeStruct((T, cc), F32), jax.ShapeDtypeStruct((T, nh), F32)],
        compiler_params=_cparams("parallel"),
        name="conv_fgt_proj",
    )(x16, w_cf, bf)


def _trunc16(x):
    bits = lax.bitcast_convert_type(x, jnp.uint32) & jnp.uint32(0xFFFF0000)
    return lax.bitcast_convert_type(bits, F32)


def _split3(x):
    hi = _trunc16(x)
    r1 = x - hi
    mid = _trunc16(r1)
    lo = r1 - mid
    return hi.astype(BF16), mid.astype(BF16), lo.astype(BF16)


def _cumsum_kernel(nblk, x_ref, o_ref):
    r = lax.broadcasted_iota(jnp.int32, (LANES, LANES), 0)
    c = lax.broadcasted_iota(jnp.int32, (LANES, LANES), 1)
    tri = (r <= c).astype(BF16)

    def body(j, carry):
        hi, mid, lo = _split3(x_ref[0, j])
        s = (jnp.dot(hi, tri, preferred_element_type=F32)
             + jnp.dot(mid, tri, preferred_element_type=F32)
             + jnp.dot(lo, tri, preferred_element_type=F32)) + carry
        o_ref[0, j] = s
        return s[:, LANES - 1:LANES]

    lax.fori_loop(0, nblk, body, jnp.zeros((x_ref.shape[2], 1), F32))


def _cumsum_time(x):
    n, L, H = x.shape
    assert L % LANES == 0
    nblk = L // LANES
    xt = x.reshape(n, nblk, LANES, H).transpose(0, 1, 3, 2)
    blk = pl.BlockSpec((1, nblk, H, LANES), lambda i: (i, 0, 0, 0))
    out = pl.pallas_call(
        functools.partial(_cumsum_kernel, nblk),
        grid=(n,),
        in_specs=[blk],
        out_specs=blk,
        out_shape=jax.ShapeDtypeStruct((n, nblk, H, LANES), F32),
        compiler_params=_cparams("parallel"),
        name="cumsum_logf",
    )(xt)
    return out.transpose(0, 1, 3, 2).reshape(n, L, H)


RET_GROUP_HEADS = 4
RET_GROUP_LANES = RET_GROUP_HEADS * HEAD_DIM
RET_CHUNKS_PER_STEP = 8


def _ret_half_major(w, n_heads):
    d = w.shape[0]
    half = HEAD_DIM // 2
    t = w.reshape(d, n_heads // RET_GROUP_HEADS, RET_GROUP_HEADS, 2, half)
    return t.transpose(0, 1, 3, 2, 4).reshape(d, n_heads * HEAD_DIM)


def _retention_kernel(n_groups, cps, x_ref, cos_ref, sin_ref, inner_ref, qdec_ref, kdec_ref, cdec_ref,
                      s0_ref, o_ref, sout_ref, s_scr, o_scr):
    W = n_groups * RET_GROUP_LANES
    c = CHUNK
    step = pl.program_id(0)
    n_chunks = x_ref.shape[0] // c

    k_lane = lax.broadcasted_iota(jnp.int32, (c, RET_GROUP_LANES), 1)
    k_head = (k_lane % LANES) // (HEAD_DIM // 2)
    v_head = k_lane // HEAD_DIM
    sr = lax.broadcasted_iota(jnp.int32, (RET_GROUP_LANES, RET_GROUP_LANES), 0)
    sc = lax.broadcasted_iota(jnp.int32, (RET_GROUP_LANES, RET_GROUP_LANES), 1)
    s_mask = ((sr % LANES) // (HEAD_DIM // 2)) == (sc // HEAD_DIM)

    for ci in range(n_chunks):
        rows = slice(ci * c, (ci + 1) * c)
        cos = cos_ref[rows, :]
        sin = sin_ref[rows, :]
        if cps == 1:
            seq_local, first, last = ci, True, True
        else:
            seq_local = 0
            first = (step * n_chunks + ci) % cps == 0
            last = (step * n_chunks + ci) % cps == cps - 1
        for g in range(n_groups):
            lo = g * RET_GROUP_LANES

            def rot(t):
                t1, t2 = t[:, :LANES], t[:, LANES:]
                return jnp.concatenate([t1 * cos - t2 * sin, t1 * sin + t2 * cos], axis=1)

            q = rot(x_ref[rows, lo:lo + RET_GROUP_LANES])
            k = rot(x_ref[rows, W + lo:W + lo + RET_GROUP_LANES])
            v = x_ref[rows, 2 * W + lo:2 * W + lo + RET_GROUP_LANES]
            q16 = q.astype(BF16)
            k16 = k.astype(BF16)
            v16 = v.astype(BF16)
            kd16 = (k * kdec_ref[g]).astype(BF16)

            if cps == 1:
                s_prev = s0_ref[seq_local, g]
            else:
                @pl.when(first)
                def _():
                    s_scr[g] = s0_ref[0, g]
                s_prev = s_scr[g]

            zero = jnp.zeros_like(k16)
            k_bd = jnp.concatenate([jnp.where(k_head == hh, k16, zero) for hh in range(RET_GROUP_HEADS)], axis=0)
            v_bd = jnp.concatenate([jnp.where(v_head == hh, v16, zero) for hh in range(RET_GROUP_HEADS)], axis=0)
            s_all = lax.dot_general(q16, k_bd, (((1,), (1,)), ((), ())), preferred_element_type=F32)
            att = (s_all * inner_ref[g]).astype(BF16)
            o_intra = jnp.dot(att, v_bd, preferred_element_type=F32)
            o_cross = jnp.dot(q16, s_prev.astype(BF16), preferred_element_type=F32) * qdec_ref[g]
            o_scr[rows, lo:lo + RET_GROUP_LANES] = o_intra + o_cross

            kv = lax.dot_general(kd16, v16, (((0,), (0,)), ((), ())), preferred_element_type=F32)
            s_new = s_prev * cdec_ref[g] + jnp.where(s_mask, kv, 0.0)
            if cps == 1:
                sout_ref[seq_local, g] = s_new
            else:
                s_scr[g] = s_new

                @pl.when(last)
                def _():
                    sout_ref[0, g] = s_new

    pr = lax.broadcasted_iota(jnp.int32, (W, W), 0) // HEAD_DIM
    pc = lax.broadcasted_iota(jnp.int32, (W, W), 1) // HEAD_DIM
    avg = jnp.where(pr == pc, 1.0 / HEAD_DIM, 0.0).astype(BF16)

    def head_mean(t):
        hi, mid, lo3 = _split3(t)
        return (jnp.dot(hi, avg, preferred_element_type=F32)
                + jnp.dot(mid, avg, preferred_element_type=F32)
                + jnp.dot(lo3, avg, preferred_element_type=F32))

    o = o_scr[...]
    oc = o - head_mean(o)
    var = head_mean(oc * oc)
    gate = x_ref[:, 3 * W:4 * W]
    o_ref[...] = (oc * lax.rsqrt(var + LN_EPS) * (gate * jax.nn.sigmoid(gate))).astype(BF16)


def _retention_tables(n_heads):
    c = CHUNK
    n_groups = n_heads // RET_GROUP_HEADS
    lg = jnp.log1p(-jnp.exp2(-5.0 - jnp.arange(n_heads, dtype=F32)))
    idx = jnp.arange(c, dtype=F32)
    diff = idx[:, None] - idx[None, :]
    causal = diff >= 0
    inner = jnp.where(causal[None], jnp.exp(jnp.where(causal, diff, 0.0)[None] * lg[:, None, None]), 0.0)
    q_dec = jnp.exp((idx + 1.0)[:, None] * lg[None, :])
    k_dec = jnp.exp((c - 1.0 - idx)[:, None] * lg[None, :])
    c_dec = jnp.exp(c * lg)
    inner_t = inner.reshape(n_groups, RET_GROUP_HEADS, c, c).transpose(0, 2, 1, 3).reshape(n_groups, c, RET_GROUP_HEADS * c)
    out_head = np.arange(RET_GROUP_LANES) // HEAD_DIM
    k_head = (np.arange(RET_GROUP_LANES) % LANES) // (HEAD_DIM // 2)
    gh = np.arange(n_groups)[:, None] * RET_GROUP_HEADS
    qdec_t = q_dec[:, gh + out_head[None, :]].transpose(1, 0, 2)
    kdec_t = k_dec[:, gh + k_head[None, :]].transpose(1, 0, 2)
    cdec_t = c_dec[gh + out_head[None, :]][:, None, :]
    return inner_t, qdec_t, kdec_t, cdec_t


def _rotary_tables(pos):
    half = HEAD_DIM // 2
    inv = ROPE_BASE ** (-jnp.arange(half, dtype=F32) / half)
    ang = pos.astype(F32)[:, None] * inv[None, :]
    return (jnp.tile(jnp.cos(ang), (1, LANES // half)), jnp.tile(jnp.sin(ang), (1, LANES // half)))


def _retention(xr, cos, sin, s0g, row0, n_rows, seq_len, n_heads):
    W = n_heads * HEAD_DIM
    n_groups = n_heads // RET_GROUP_HEADS
    nseq = n_rows // seq_len
    cps = seq_len // CHUNK
    tl = RET_CHUNKS_PER_STEP * CHUNK
    assert n_rows % tl == 0 and row0 % tl == 0 and (cps == 1 or cps % RET_CHUNKS_PER_STEP == 0)
    off = row0 // tl
    if cps == 1:
        sblk = pl.BlockSpec((RET_CHUNKS_PER_STEP, n_groups, RET_GROUP_LANES, RET_GROUP_LANES),
                            lambda i: (i, 0, 0, 0))
    else:
        spc = cps // RET_CHUNKS_PER_STEP
        sblk = pl.BlockSpec((1, n_groups, RET_GROUP_LANES, RET_GROUP_LANES), lambda i: (i // spc, 0, 0, 0))
    inner_t, qdec_t, kdec_t, cdec_t = _retention_tables(n_heads)

    def full(a):
        return pl.BlockSpec(a.shape, lambda i: (0,) * a.ndim)

    return pl.pallas_call(
        functools.partial(_retention_kernel, n_groups, cps),
        grid=(n_rows // tl,),
        in_specs=[pl.BlockSpec((tl, 4 * W), lambda i: (i + off, 0)),
                  pl.BlockSpec((tl, LANES), lambda i: (i + off, 0)),
                  pl.BlockSpec((tl, LANES), lambda i: (i + off, 0)),
                  full(inner_t), full(qdec_t), full(kdec_t), full(cdec_t), sblk],
        out_specs=[pl.BlockSpec((tl, W), lambda i: (i, 0)), sblk],
        out_shape=[jax.ShapeDtypeStruct((n_rows, W), BF16),
                   jax.ShapeDtypeStruct((nseq, n_groups, RET_GROUP_LANES, RET_GROUP_LANES), F32)],
        scratch_shapes=[pltpu.VMEM((n_groups, RET_GROUP_LANES, RET_GROUP_LANES), F32),
                        pltpu.VMEM((tl, W), F32)],
        compiler_params=_cparams("arbitrary"),
        name="retention",
    )(xr, cos, sin, inner_t, qdec_t, kdec_t, cdec_t, s0g)


def _ret_state_to_groups(s):
    n, H = s.shape[:2]
    G = H // RET_GROUP_HEADS
    half = HEAD_DIM // 2
    t = s.reshape(n, G, RET_GROUP_HEADS, 2, half, HEAD_DIM).transpose(0, 1, 3, 2, 4, 5)
    eye = jnp.eye(RET_GROUP_HEADS, dtype=s.dtype)
    t = t[:, :, :, :, :, None, :] * eye[None, None, None, :, None, :, None]
    return t.reshape(n, G, RET_GROUP_LANES, RET_GROUP_LANES)


def _ret_state_from_groups(sg):
    n, G = sg.shape[:2]
    half = HEAD_DIM // 2
    t = sg.reshape(n, G, 2, RET_GROUP_HEADS, half, RET_GROUP_HEADS, HEAD_DIM)
    t = jnp.stack([t[:, :, :, hh, :, hh, :] for hh in range(RET_GROUP_HEADS)], axis=2)
    return t.reshape(n, G * RET_GROUP_HEADS, HEAD_DIM, HEAD_DIM)


CONV_HALO = 32


def _conv_kernel(tl, u_ref, prev_ref, w_ref, cb_ref, g_ref, b_ref, o_ref, buf):
    j = pl.program_id(1)

    @pl.when(j == 0)
    def _():
        buf[0:CONV_HALO, :] = prev_ref[0]

    @pl.when(j > 0)
    def _():
        buf[0:CONV_HALO, :] = buf[tl:tl + CONV_HALO, :]

    buf[CONV_HALO:CONV_HALO + tl, :] = u_ref[...]
    off = CONV_HALO - (CONV_WIDTH - 1)
    acc = buf[off:off + tl, :] * w_ref[0:1, :]
    for w in range(1, CONV_WIDTH):
        acc = acc + buf[off + w:off + w + tl, :] * w_ref[w:w + 1, :]
    y = _ln_rows(acc + cb_ref[...], g_ref[...], b_ref[...])
    o_ref[...] = (y * jax.nn.sigmoid(y)).astype(BF16)


def _conformer_conv(u, prev, w, cb, g, b, row0, seq_len):
    C = u.shape[1]
    nseq = prev.shape[0]
    tl = _row_tile(seq_len, 512)
    assert tl >= CONV_HALO and row0 % tl == 0
    off = row0 // tl
    nt = seq_len // tl
    prev_p = jnp.pad(prev.astype(F32), ((0, 0), (CONV_HALO - (CONV_WIDTH - 1), 0), (0, 0)))
    w_p = jnp.pad(w, ((0, CONV_HALO - CONV_WIDTH), (0, 0)))
    vec = pl.BlockSpec((1, C), lambda s, j: (0, 0))
    return pl.pallas_call(
        functools.partial(_conv_kernel, tl),
        grid=(nseq, nt),
        in_specs=[pl.BlockSpec((tl, C), lambda s, j: (off + s * nt + j, 0)),
                  pl.BlockSpec((1, CONV_HALO, C), lambda s, j: (s, 0, 0)),
                  pl.BlockSpec((CONV_HALO, C), lambda s, j: (0, 0)),
                  vec, vec, vec],
        out_specs=pl.BlockSpec((tl, C), lambda s, j: (s * nt + j, 0)),
        out_shape=jax.ShapeDtypeStruct((nseq * seq_len, C), BF16),
        scratch_shapes=[pltpu.VMEM((tl + CONV_HALO, C), F32)],
        compiler_params=_cparams("arbitrary", "arbitrary"),
        name="conformer_conv",
    )(u, prev_p, w_p, cb.reshape(1, C), g.reshape(1, C), b.reshape(1, C))


FOX_HEADS_PER_STEP = 2
LOG2E = 1.4426950408889634
FOX_SKIP_LOG2 = -160.0


def _split3_f32(x):
    hi = _trunc16(x)
    r1 = x - hi
    mid = _trunc16(r1)
    return hi, mid, r1 - mid


def _fox_prompt_kernel(bq, nkb, jlo_ref, q_ref, k_ref, v_ref, cq_ref, ck_ref, o_ref,
                       kt_scr, m_scr, l_scr, acc_scr):
    pair = pl.program_id(0)
    i = pl.program_id(1)
    heads = range(FOX_HEADS_PER_STEP)

    def key_rows(j):
        return pl.ds(pl.multiple_of(j * bq, bq), bq)

    @pl.when(i == 0)
    def _():
        row = lax.broadcasted_iota(jnp.int32, (LANES, bq), 0)
        ones = ((row < 3) | ((row >= 6) & (row < 9))).astype(F32)

        def build(j, carry):
            kt_scr[j, 0:LANES, :] = k_ref[key_rows(j), :].T
            ck = ck_ref[j]
            bias = ones
            for h in heads:
                terms = _split3_f32(-ck[h:h + 1, :])
                for t in range(3):
                    bias = jnp.where(row == 6 * h + 3 + t, terms[t], bias)
            kt_scr[j, LANES:2 * LANES, :] = bias.astype(BF16)
            return carry

        lax.fori_loop(0, nkb, build, 0)

    q = q_ref[...]
    lane = lax.broadcasted_iota(jnp.int32, (bq, LANES), 1)
    cq = cq_ref[...]
    qs = []
    for h in heads:
        terms = _split3_f32(cq[:, h:h + 1])
        bias = ((lane >= 6 * h + 3) & (lane < 6 * h + 6)).astype(F32)
        for t in range(3):
            bias = jnp.where(lane == 6 * h + t, terms[t], bias)
        in_head = (lane < HEAD_DIM) if h == 0 else (lane >= HEAD_DIM)
        qs.append(jnp.concatenate([jnp.where(in_head, q, jnp.zeros_like(q)), bias.astype(BF16)], axis=1))
        m_scr[h] = jnp.full((bq, LANES), NEG_BIG, F32)
        l_scr[h] = jnp.zeros((bq, LANES), F32)
        acc_scr[h] = jnp.zeros((bq, LANES), F32)

    def update(h, s, v):
        m_prev = m_scr[h]
        m_new = jnp.maximum(m_prev, jnp.max(s, axis=1, keepdims=True))
        p = jnp.exp2(s - jnp.concatenate([m_new] * (bq // LANES), axis=1))
        alpha = jnp.exp2(m_prev - m_new)
        l_scr[h] = alpha * l_scr[h] + jnp.sum(p, axis=1, keepdims=True)
        acc_scr[h] = alpha * acc_scr[h] + jnp.dot(p.astype(BF16), v, preferred_element_type=F32)
        m_scr[h] = m_new

    def scores(h, j):
        return jnp.dot(qs[h], kt_scr[j], preferred_element_type=F32)

    def body(j, carry):
        s = [scores(h, j) for h in heads]
        v = v_ref[key_rows(j), :]
        for h in heads:
            update(h, s[h], v)
        return carry

    lax.fori_loop(jlo_ref[pair * nkb + i], i, body, 0)

    r = lax.broadcasted_iota(jnp.int32, (bq, bq), 0)
    c = lax.broadcasted_iota(jnp.int32, (bq, bq), 1)
    s = [scores(h, i) for h in heads]
    v = v_ref[key_rows(i), :]
    for h in heads:
        update(h, jnp.where(c <= r, s[h], NEG_BIG), v)

    o_ref[...] = jnp.where(lane < HEAD_DIM, acc_scr[0] / l_scr[0], acc_scr[1] / l_scr[1]).astype(BF16)


def _fox_skip(qn2, kn2, ct, n_heads, bq):
    L = ct.shape[0]
    nb = L // bq
    qn = jnp.sqrt(qn2[:nb, 0, :n_heads])
    kn = jnp.sqrt(kn2[:nb, 0, :n_heads])
    kmax = jnp.max(kn, axis=0, keepdims=True)
    ctb = ct.reshape(nb, bq, n_heads)
    bound = (1.02 * qn * (kmax + kn) + ctb[:, 0, :])[:, None, :] - ctb[None, :, bq - 1, :]
    skip = (bound < FOX_SKIP_LOG2) & (jnp.arange(nb)[None, :, None] < jnp.arange(nb)[:, None, None])
    jlo = jnp.sum(jnp.cumprod(skip.astype(jnp.int32), axis=1), axis=1)
    jlo = jnp.min(jlo.reshape(nb, n_heads // FOX_HEADS_PER_STEP, FOX_HEADS_PER_STEP), axis=2)
    return jlo.T.reshape(-1).astype(jnp.int32)


def _fox_prompt_attention(q16, k16, v16, qn2, kn2, ct, L, bq):
    W = q16.shape[1]
    n_heads = W // HEAD_DIM
    hp = FOX_HEADS_PER_STEP
    n_pairs = n_heads // hp
    nkb = L // bq
    jlo = _fox_skip(qn2, kn2, ct, n_heads, bq)
    cq = ct.reshape(L, n_pairs, hp).transpose(1, 0, 2)
    ck = ct.reshape(nkb, bq, n_pairs, hp).transpose(2, 0, 3, 1)
    grid_spec = pltpu.PrefetchScalarGridSpec(
        num_scalar_prefetch=1,
        grid=(n_pairs, nkb),
        in_specs=[pl.BlockSpec((bq, LANES), lambda p, i, jl: (i, p)),
                  pl.BlockSpec((L, LANES), lambda p, i, jl: (0, p)),
                  pl.BlockSpec((L, LANES), lambda p, i, jl: (0, p)),
                  pl.BlockSpec((None, bq, hp), lambda p, i, jl: (p, i, 0)),
                  pl.BlockSpec((None, nkb, hp, bq), lambda p, i, jl: (p, 0, 0, 0))],
        out_specs=pl.BlockSpec((bq, LANES), lambda p, i, jl: (i, p)),
        scratch_shapes=[pltpu.VMEM((nkb, 2 * LANES, bq), BF16),
                        pltpu.VMEM((hp, bq, LANES), F32), pltpu.VMEM((hp, bq, LANES), F32),
                        pltpu.VMEM((hp, bq, LANES), F32)],
    )
    return pl.pallas_call(
        functools.partial(_fox_prompt_kernel, bq, nkb),
        grid_spec=grid_spec,
        out_shape=jax.ShapeDtypeStruct((L, W), BF16),
        compiler_params=_cparams("arbitrary", "arbitrary"),
        name="fox_prompt",
    )(jlo, q16, k16, v16, cq, ck)


def _fox_sample_kernel(past, ls, q_ref, kn_ref, vn_ref, kc_ref, vc_ref, cq_ref, ck_ref, o_ref):
    q = q_ref[...]
    lane = lax.broadcasted_iota(jnp.int32, q.shape, 1)
    zero = jnp.zeros_like(q)
    q2 = jnp.concatenate([jnp.where(lane < HEAD_DIM, q, zero), jnp.where(lane >= HEAD_DIM, q, zero)], axis=0)
    pad = jnp.zeros((LANES - ls, LANES), BF16)
    k_all = jnp.concatenate([kc_ref[0].astype(BF16), kn_ref[...], pad], axis=0)
    v_all = jnp.concatenate([vc_ref[0].astype(BF16), vn_ref[...], pad], axis=0)
    nk = past + LANES
    s = lax.dot_general(q2, k_all, (((1,), (1,)), ((), ())), preferred_element_type=F32)
    row = lax.broadcasted_iota(jnp.int32, (2 * ls, nk), 0)
    col = lax.broadcasted_iota(jnp.int32, (2 * ls, nk), 1)
    ck = jnp.where(row < ls, ck_ref[0, 0:1, :], ck_ref[0, 1:2, :])
    visible = (col < past) | (col - past <= row % ls)
    s = jnp.where(visible, s + cq_ref[0] - ck, NEG_BIG)
    p = jnp.exp2(s - jnp.max(s, axis=1, keepdims=True))
    den = jnp.sum(p, axis=1, keepdims=True)
    o2 = jnp.dot(p.astype(BF16), v_all, preferred_element_type=F32) / den
    o_ref[...] = jnp.where(lane < HEAD_DIM, o2[0:ls], o2[ls:2 * ls]).astype(BF16)


def _fox_sample_attention(q16, k16, v16, cache_k, cache_v, layer, cq, ck, row0, ls):
    W = q16.shape[1]
    past = cache_k.shape[1]
    n_pairs = W // LANES
    B = cq.shape[0] // n_pairs
    assert ls <= LANES and row0 % ls == 0
    off = row0 // ls
    rows = pl.BlockSpec((ls, LANES), lambda b, p: (off + b, p))
    cache = pl.BlockSpec((1, past, LANES), lambda b, p: (layer * B + b, 0, p))
    return pl.pallas_call(
        functools.partial(_fox_sample_kernel, past, ls),
        grid=(B, n_pairs),
        in_specs=[rows, rows, rows, cache, cache,
                  pl.BlockSpec((1, 2 * ls, 1), lambda b, p: (b * n_pairs + p, 0, 0)),
                  pl.BlockSpec((1, 2, past + LANES), lambda b, p: (b * n_pairs + p, 0, 0))],
        out_specs=pl.BlockSpec((ls, LANES), lambda b, p: (b, p)),
        out_shape=jax.ShapeDtypeStruct((B * ls, W), BF16),
        compiler_params=_cparams("parallel", "parallel"),
        name="fox_sample",
    )(q16, k16, v16, cache_k, cache_v, cq, ck)


def _out_router_kernel(alpha, n_first, x_ref, r0_ref, f0_ref, c0_ref, r1_ref, f1_ref, c1_ref,
                       w_ref, g_ref, b_ref, rw_ref, rb_ref, o32_ref, otok_ref, route_ref):
    first = pl.program_id(0) < n_first
    mixed = jnp.concatenate([jnp.where(first, r0_ref[...], r1_ref[...]),
                             jnp.where(first, f0_ref[...], f1_ref[...]),
                             jnp.where(first, c0_ref[...], c1_ref[...])], axis=1)
    m = jnp.dot(mixed, w_ref[...], preferred_element_type=F32)
    y = _ln_rows(alpha * x_ref[...] + m, g_ref[...], b_ref[...])
    o32_ref[...] = y
    tm, d = y.shape
    rpt = d // LANES
    for c in range(rpt):
        otok_ref[pl.ds(c, tm, stride=rpt), :] = y[:, c * LANES:(c + 1) * LANES]

    logits = jnp.dot(y.astype(BF16), rw_ref[...], preferred_element_type=F32) + rb_ref[...]
    lane = lax.broadcasted_iota(jnp.int32, logits.shape, 1).astype(F32)
    far = float(LANES)

    def rmax(t):
        return jnp.max(t, axis=1, keepdims=True)

    def rmin(t):
        return jnp.min(t, axis=1, keepdims=True)

    def rsum(t):
        return jnp.sum(t, axis=1, keepdims=True)

    gmask = lane < N_GROUPS
    lg = jnp.where(gmask, logits, -jnp.inf)
    mg = rmax(lg)
    p_grp = 1.0 / rsum(jnp.exp(lg - mg))
    g_idx = rmin(jnp.where(lg == mg, lane, far))
    e_lo = N_GROUPS + EXPERTS_PER_GROUP * g_idx
    emask = (lane >= e_lo) & (lane < e_lo + EXPERTS_PER_GROUP)
    le = jnp.where(emask, logits, -jnp.inf)
    ee = jnp.exp(le - rmax(le))
    pin = jnp.where(emask, ee / rsum(ee), -1.0)
    p1 = rmax(pin)
    i1 = rmin(jnp.where(pin == p1, lane, far))
    pin2 = jnp.where(lane == i1, -1.0, pin)
    p2 = rmax(pin2)
    i2 = rmin(jnp.where(pin2 == p2, lane, far))
    den = p1 + p2
    route_ref[...] = jnp.where(lane == 0, i1 - N_GROUPS,
                     jnp.where(lane == 1, i2 - N_GROUPS,
                     jnp.where(lane == 2, p_grp * p1 / den,
                     jnp.where(lane == 3, p_grp * p2 / den, 0.0))))


def _out_proj_router(x, mix0, mix1, w_out16, g, b, rw16, rb, alpha):
    T, D = x.shape
    tm = _row_tile(mix1[0].shape[0], 256)
    n0 = mix0[0].shape[0] // tm
    n1 = mix1[0].shape[0] // tm
    assert mix0[0].shape[0] % tm == 0 and (n0 + n1) * tm == T

    def rows(a):
        return pl.BlockSpec((tm, a.shape[1]), lambda i: (i, 0))

    def rows0(a):
        return pl.BlockSpec((tm, a.shape[1]), lambda i: (jnp.minimum(i, n0 - 1), 0))

    def rows1(a):
        return pl.BlockSpec((tm, a.shape[1]), lambda i: (jnp.maximum(i - n0, 0), 0))

    def full(a):
        return pl.BlockSpec(a.shape, lambda i: (0, 0))

    g2, b2 = g.reshape(1, D), b.reshape(1, D)
    return pl.pallas_call(
        functools.partial(_out_router_kernel, alpha, n0),
        grid=(T // tm,),
        in_specs=[rows(x)] + [rows0(a) for a in mix0] + [rows1(a) for a in mix1]
                 + [full(w_out16), full(g2), full(b2), full(rw16), full(rb)],
        out_specs=[pl.BlockSpec((tm, D), lambda i: (i, 0)),
                   pl.BlockSpec((tm * (D // LANES), LANES), lambda i: (i, 0)),
                   pl.BlockSpec((tm, LANES), lambda i: (i, 0))],
        out_shape=[jax.ShapeDtypeStruct((T, D), F32), jax.ShapeDtypeStruct((T * (D // LANES), LANES), F32),
                   jax.ShapeDtypeStruct((T, LANES), F32)],
        compiler_params=_cparams("parallel"),
        name="out_proj_router",
    )(x, *mix0, *mix1, w_out16, g2, b2, rw16, rb)


def _moe_kernel(bm, rpt, be_ref, nused_ref, base_ref, tok_ref, x_hbm, wg_ref, wu_ref, wd_ref, y_ref,
                wg16, wu16, wd16, xbuf, sem):
    blk = pl.program_id(0)
    n_used = nused_ref[0]
    used = blk < n_used
    slot = blk % MOE_GATHER_SLOTS
    ahead = MOE_GATHER_SLOTS - 1
    prev_expert = be_ref[jnp.maximum(blk - 1, 0)]

    def start_gather(of_blk):
        base = base_ref[of_blk]
        to_slot = of_blk % MOE_GATHER_SLOTS
        for r in range(bm):
            src = pl.ds(pl.multiple_of(tok_ref[base + r] * rpt, rpt), rpt)
            pltpu.make_async_copy(x_hbm.at[src], xbuf.at[to_slot, pl.ds(r * rpt, rpt)],
                                  sem.at[to_slot]).start()

    def wait_gather():
        pltpu.make_async_copy(x_hbm.at[pl.ds(0, bm * rpt)], xbuf.at[slot], sem.at[slot]).wait()

    @pl.when(blk == 0)
    def _():
        for b in range(ahead):
            start_gather(b)

    @pl.when(used & ((blk == 0) | (be_ref[blk] != prev_expert)))
    def _():
        wg16[...] = wg_ref[0].astype(BF16)
        wu16[...] = wu_ref[0].astype(BF16)
        wd16[...] = wd_ref[0].astype(BF16)

    @pl.when(jnp.logical_not(used) & (blk < n_used + ahead))
    def _():
        wait_gather()

    @pl.when(used)
    def _():
        wait_gather()
        start_gather(blk + ahead)
        xs = xbuf.at[slot]
        x = jnp.concatenate([xs[pl.ds(c, bm, stride=rpt), :] for c in range(rpt)], axis=1).astype(BF16)
        a = jnp.dot(x, wg16[...], preferred_element_type=F32)
        u = jnp.dot(x, wu16[...], preferred_element_type=F32)
        h = (a * jax.nn.sigmoid(a) * u).astype(BF16)
        y_ref[...] = jnp.dot(h, wd16[...], preferred_element_type=F32)

    @pl.when(jnp.logical_not(used))
    def _():
        y_ref[...] = jnp.zeros_like(y_ref)


def _moe_experts(xt, tok_sorted, base, block_expert, n_used, wg, wu, wd, layer, n_rows):
    D, DE = wg.shape[2], wg.shape[3]
    rpt = D // LANES
    bm = MOE_BLOCK_ROWS
    n_blk = n_rows // bm

    def weights(rows, cols):
        return pl.BlockSpec((None, 1, rows, cols), lambda i, be, nu, ba, tk: (layer, be[i], 0, 0))

    grid_spec = pltpu.PrefetchScalarGridSpec(
        num_scalar_prefetch=4,
        grid=(n_blk,),
        in_specs=[pl.BlockSpec(memory_space=pl.ANY), weights(D, DE), weights(D, DE), weights(DE, D)],
        out_specs=pl.BlockSpec((bm, D), lambda i, be, nu, ba, tk: (i, 0)),
        scratch_shapes=[pltpu.VMEM((D, DE), BF16), pltpu.VMEM((D, DE), BF16), pltpu.VMEM((DE, D), BF16),
                        pltpu.VMEM((MOE_GATHER_SLOTS, bm * rpt, LANES), F32),
                        pltpu.SemaphoreType.DMA((MOE_GATHER_SLOTS,))],
    )
    return pl.pallas_call(
        functools.partial(_moe_kernel, bm, rpt),
        grid_spec=grid_spec,
        out_shape=jax.ShapeDtypeStruct((n_rows, D), F32),
        compiler_params=_cparams("arbitrary"),
        name="moe_experts",
    )(block_expert, n_used, base, tok_sorted, xt, wg, wu, wd)


def _moe_dispatch(route):
    T = route.shape[0]
    bm = MOE_BLOCK_ROWS
    eid = route[:, 0:2].astype(jnp.int32).reshape(-1)
    n_assign = 2 * T
    n_rows = n_assign + (N_EXPERTS + MOE_GATHER_SLOTS - 2) * bm
    n_blocks = n_rows // bm
    eid_s, a_s = lax.sort((eid, jnp.arange(n_assign, dtype=jnp.int32)), num_keys=1, is_stable=True)
    onehot_s = (eid_s[:, None] == jnp.arange(N_EXPERTS)[None, :]).astype(F32)
    exact = lax.Precision.HIGHEST
    counts = jnp.dot(jnp.ones((1, n_assign), F32), onehot_s, precision=exact)[0].astype(jnp.int32)
    padded = ((counts + bm - 1) // bm) * bm
    ends = jnp.cumsum(padded)
    starts = ends - padded
    first = jnp.cumsum(counts) - counts
    shift = (starts - first).astype(F32)
    pos_s = jnp.arange(n_assign, dtype=jnp.int32) + jnp.dot(onehot_s, shift, precision=exact).astype(jnp.int32)
    _, pos = lax.sort((a_s, pos_s), num_keys=1)
    n_used = ends[-1] // bm
    blk_start = jnp.arange(n_blocks, dtype=jnp.int32) * bm
    block_expert = jnp.sum((ends[None, :] <= blk_start[:, None]).astype(jnp.int32), axis=1)
    block_expert = jnp.minimum(block_expert, N_EXPERTS - 1)
    last_used = block_expert[jnp.maximum(n_used - 1, 0)]
    block_expert = jnp.where(jnp.arange(n_blocks) < n_used, block_expert, last_used)
    base = jnp.clip(blk_start - starts[block_expert] + first[block_expert], 0, n_assign)
    tok_sorted = jnp.concatenate([a_s // 2, jnp.zeros((bm,), jnp.int32)])
    return tok_sorted, base, block_expert, n_used.reshape(1).astype(jnp.int32), pos.reshape(T, 2), n_rows


def kernel(x_prompt, x_sample, cache_fox_k, cache_fox_v, cache_fox_logf, state_ret, state_conv, ln_in_g, ln_in_b, w_in, b_forget, conv_w, conv_b, conv_ln_g, conv_ln_b, w_out, ln1_g, ln1_b, router_group_w, router_group_b, router_expert_w, router_expert_b, w_gate, w_up, w_down, ln2_g, ln2_b):
    BP, LP, D = x_prompt.shape
    BS, LS, _ = x_sample.shape
    depth = w_in.shape[0]
    past = cache_fox_k.shape[2]
    fox_heads = cache_fox_k.shape[3]
    ret_heads = state_ret.shape[2]
    conv_ch = state_conv.shape[3]
    ret_w = ret_heads * HEAD_DIM
    fox_w = fox_heads * HEAD_DIM
    off_fox = 4 * ret_w
    off_fgt = off_fox + 3 * fox_w
    off_conv = off_fgt + fox_heads
    alpha = (2 * depth) ** 0.25
    TP, TS = BP * LP, BS * LS
    assert BP == 1 and LS == CHUNK and past % 512 == 0

    x32, x16 = _layer_norm_in(x_prompt.reshape(TP, D), x_sample.reshape(TS, D), ln_in_g, ln_in_b)

    pos_all =jnp.concatenate([jnp.arange(LP), past + jnp.tile(jnp.arange(LS), BS)])
    cos_t, sin_t = _rotary_tables(pos_all)
    fox_bq = _row_tile(LP, 512)
    n_pairs = fox_heads // FOX_HEADS_PER_STEP
    cache_k = cache_fox_k.reshape(depth * BS, past, fox_w)
    cache_v = cache_fox_v.reshape(depth * BS, past, fox_w)
    qk_scale = HEAD_DIM ** -0.5
    zero_state = jnp.zeros((BP, ret_heads // RET_GROUP_HEADS, RET_GROUP_LANES, RET_GROUP_LANES), F32)
    zero_conv = jnp.zeros((BP, CONV_WIDTH - 1, conv_ch), F32)

    ks, vs, lfs, rs_p, rs_s, cs_p, cs_s = [], [], [], [], [], [], []
    for l in range(depth):
        wl = w_in[l]
        w_ret = jnp.concatenate([_ret_half_major(wl[:, 0:ret_w], ret_heads),
                                 _ret_half_major(wl[:, ret_w:2 * ret_w] * qk_scale, ret_heads),
                                 wl[:, 2 * ret_w:4 * ret_w]], axis=1).astype(BF16)
        w_fox = jnp.concatenate([wl[:, off_fox:off_fox + fox_w] * (qk_scale * LOG2E),
                                 wl[:, off_fox + fox_w:off_fgt]], axis=1).astype(BF16)
        w_cf = jnp.concatenate([wl[:, off_conv:], wl[:, off_fgt:off_conv],
                                jnp.zeros((D, LANES - fox_heads), F32)], axis=1).astype(BF16)

        xr = _matmul(x16, w_ret, F32, "ret_proj")
        fq16, fk32, fv32, fk16, fv16, qn2, kn2 = _fox_proj(x16, w_fox)
        u, logf = _conv_fgt_proj(x16, w_cf, b_forget[l], conv_ch, fox_heads)

        o_r_p, sg_p = _retention(xr, cos_t, sin_t, zero_state, 0, TP, LP, ret_heads)
        o_r_s, sg_s = _retention(xr, cos_t, sin_t, _ret_state_to_groups(state_ret[l]), TP, TS, LS, ret_heads)
        rs_p.append(_ret_state_from_groups(sg_p))
        rs_s.append(_ret_state_from_groups(sg_s))

        cw, cb, cg, cbeta = conv_w[l], conv_b[l], conv_ln_g[l], conv_ln_b[l]
        o_c_p = _conformer_conv(u, zero_conv, cw, cb, cg, cbeta, 0, LP)
        o_c_s = _conformer_conv(u, state_conv[l], cw, cb, cg, cbeta, TP, LS)
        u_p = u[:TP].reshape(BP, LP, conv_ch)
        u_s = u[TP:].reshape(BS, LS, conv_ch)
        cs_p.append(u_p[:, LP - (CONV_WIDTH - 1):])
        cs_s.append(jnp.concatenate([state_conv[l], u_s], axis=1)[:, -(CONV_WIDTH - 1):])

        ct_p = _cumsum_time(logf[:TP].reshape(BP, LP, fox_heads))[0] * LOG2E
        o_f_p = _fox_prompt_attention(fq16, fk16, fv16, qn2, kn2, ct_p, LP, fox_bq)

        lf_all = jnp.concatenate([cache_fox_logf[l].astype(F32), logf[TP:].reshape(BS, LS, fox_heads),
                                  jnp.zeros((BS, LANES - LS, fox_heads), F32)], axis=1)
        ct_s = (_cumsum_time(lf_all) * LOG2E).transpose(0, 2, 1)
        ck_s = ct_s.reshape(BS * n_pairs, FOX_HEADS_PER_STEP, past + LANES)
        cq_s = ct_s[:, :, past:past + LS].reshape(BS * n_pairs, FOX_HEADS_PER_STEP * LS, 1)
        o_f_s = _fox_sample_attention(fq16, fk16, fv16, cache_k, cache_v, l, cq_s, ck_s, TP, LS)

        rw = jnp.concatenate([router_group_w[l], router_expert_w[l],
                              jnp.zeros((D, LANES - N_GROUPS - N_EXPERTS), F32)], axis=1).astype(BF16)
        rb = jnp.concatenate([router_group_b[l], router_expert_b[l],
                              jnp.zeros((LANES - N_GROUPS - N_EXPERTS,), F32)]).reshape(1, LANES)
        x1_32, x1_tok, route = _out_proj_router(x32, (o_r_p, o_f_p, o_c_p), (o_r_s, o_f_s, o_c_s),
                                                w_out[l].astype(BF16), ln1_g[l], ln1_b[l], rw, rb, alpha)

        tok_sorted, base, block_expert, n_used, pos, n_rows = _moe_dispatch(route)
        ys = _moe_experts(x1_tok, tok_sorted, base, block_expert, n_used, w_gate, w_up, w_down, l, n_rows)
        f0 = jnp.take(ys, pos[:, 0], axis=0)
        f1 = jnp.take(ys, pos[:, 1], axis=0)
        x32, x16 = _residual_ln(x1_32, f0, f1, route, ln2_g[l], ln2_b[l], alpha)

        ks.append(fk32)
        vs.append(fv32)
        lfs.append(logf)

    def split(ts, tail):
        a = jnp.stack(ts)
        return (a[:, :TP].reshape((depth, BP, LP) + tail), a[:, TP:].reshape((depth, BS, LS) + tail))

    pk, sk = split(ks, (fox_heads, HEAD_DIM))
    pv, sv = split(vs, (fox_heads, HEAD_DIM))
    plf, slf = split(lfs, (fox_heads,))
    return (x32[:TP].reshape(BP, LP, D), x32[TP:].reshape(BS, LS, D),
            pk, pv, plf, jnp.stack(rs_p), jnp.stack(cs_p),
            sk, sv, slf, jnp.stack(rs_s), jnp.stack(cs_s))
```

```python
import functools

import numpy as np
import jax
import jax.numpy as jnp
from jax import lax
from jax.experimental import pallas as pl
from jax.experimental.pallas import tpu as pltpu

F32 = jnp.float32
BF16 = jnp.bfloat16

HEAD_DIM = 64
CHUNK = 64
CONV_WIDTH = 31
N_GROUPS = 4
EXPERTS_PER_GROUP = 8
N_EXPERTS = N_GROUPS * EXPERTS_PER_GROUP
ROPE_BASE = 10000.0
LN_EPS = 1e-5
NEG_BIG = -1e30

LANES = 128
MOE_BLOCK_ROWS = 256
VMEM_LIMIT = 56 * 1024 * 1024


def _cparams(*sem):
    return pltpu.CompilerParams(dimension_semantics=sem, vmem_limit_bytes=VMEM_LIMIT)


def _row_tile(n, pref):
    t = min(n, pref)
    while n % t:
        t //= 2
    return t


def _ln_rows(y, g, b):
    mu = jnp.mean(y, axis=-1, keepdims=True)
    yc = y - mu
    var = jnp.mean(yc * yc, axis=-1, keepdims=True)
    return yc * lax.rsqrt(var + LN_EPS) * g + b


def _ln_kernel(n_first, x0_ref, x1_ref, g_ref, b_ref, o32_ref, o16_ref):
    x = jnp.where(pl.program_id(0) < n_first, x0_ref[...], x1_ref[...])
    y = _ln_rows(x, g_ref[...], b_ref[...])
    o32_ref[...] = y
    o16_ref[...] = y.astype(BF16)


def _layer_norm_in(x0, x1, g, b):
    D = x0.shape[1]
    tm = _row_tile(x1.shape[0], 256)
    n0, n1 = x0.shape[0] // tm, x1.shape[0] // tm
    assert x0.shape[0] % tm == 0
    T = (n0 + n1) * tm
    row = pl.BlockSpec((tm, D), lambda i: (i, 0))
    vec = pl.BlockSpec((1, D), lambda i: (0, 0))
    return pl.pallas_call(
        functools.partial(_ln_kernel, n0),
        grid=(n0 + n1,),
        in_specs=[pl.BlockSpec((tm, D), lambda i: (jnp.minimum(i, n0 - 1), 0)),
                  pl.BlockSpec((tm, D), lambda i: (jnp.maximum(i - n0, 0), 0)), vec, vec],
        out_specs=[row, row],
        out_shape=[jax.ShapeDtypeStruct((T, D), F32), jax.ShapeDtypeStruct((T, D), BF16)],
        compiler_params=_cparams("parallel"),
        name="ln_in",
    )(x0, x1, g.reshape(1, D), b.reshape(1, D))


ROUTE_GATE_LANE = 2


def _res_ln_kernel(alpha, x_ref, f0_ref, f1_ref, route_ref, g_ref, b_ref, o32_ref, o16_ref):
    route = route_ref[...]
    f = (route[:, ROUTE_GATE_LANE:ROUTE_GATE_LANE + 1] * f0_ref[...]
         + route[:, ROUTE_GATE_LANE + 1:ROUTE_GATE_LANE + 2] * f1_ref[...])
    y = _ln_rows(alpha * x_ref[...] + f, g_ref[...], b_ref[...])
    o32_ref[...] = y
    o16_ref[...] = y.astype(BF16)


def _residual_ln(x, f0, f1, route, g, b, alpha):
    T, D = x.shape
    tm = _row_tile(T, 256)
    row = pl.BlockSpec((tm, D), lambda i: (i, 0))
    vec = pl.BlockSpec((1, D), lambda i: (0, 0))
    return pl.pallas_call(
        functools.partial(_res_ln_kernel, alpha),
        grid=(T // tm,),
        in_specs=[row, row, row, pl.BlockSpec((tm, LANES), lambda i: (i, 0)), vec, vec],
        out_specs=[row, row],
        out_shape=[jax.ShapeDtypeStruct((T, D), F32), jax.ShapeDtypeStruct((T, D), BF16)],
        compiler_params=_cparams("parallel"),
        name="res_ln",
    )(x, f0, f1, route, g.reshape(1, D), b.reshape(1, D))


def _mm_kernel(x_ref, w_ref, o_ref):
    o_ref[...] = jnp.dot(x_ref[...], w_ref[...], preferred_element_type=F32).astype(o_ref.dtype)


def _matmul(x, w, out_dtype, name):
    T, K = x.shape
    N = w.shape[1]
    tm = _row_tile(T, 1024)
    tn = _row_tile(N, 1024)
    return pl.pallas_call(
        _mm_kernel,
        grid=(T // tm, N // tn),
        in_specs=[pl.BlockSpec((tm, K), lambda i, j: (i, 0)),
                  pl.BlockSpec((K, tn), lambda i, j: (0, j))],
        out_specs=pl.BlockSpec((tm, tn), lambda i, j: (i, j)),
        out_shape=jax.ShapeDtypeStruct((T, N), out_dtype),
        compiler_params=_cparams("parallel", "parallel"),
        name=name,
    )(x, w)


def _fox_proj_kernel(W, x_ref, w_ref, q16_ref, k32_ref, v32_ref, k16_ref, v16_ref, qn_ref, kn_ref):
    x = x_ref[...]
    q16 = jnp.dot(x, w_ref[:, 0:W], preferred_element_type=F32).astype(BF16)
    q16_ref[...] = q16
    k = jnp.dot(x, w_ref[:, W:2 * W], preferred_element_type=F32)
    k16 = k.astype(BF16)
    k32_ref[...] = k
    k16_ref[...] = k16
    v = jnp.dot(x, w_ref[:, 2 * W:3 * W], preferred_element_type=F32)
    v32_ref[...] = v
    v16_ref[...] = v.astype(BF16)

    col_head = lax.broadcasted_iota(jnp.int32, (W, LANES), 0) // HEAD_DIM
    head = lax.broadcasted_iota(jnp.int32, (W, LANES), 1)
    head_sum = (col_head == head).astype(BF16)

    def max_sq_norm(t16):
        t = t16.astype(F32)
        n2 = jnp.dot((t * t).astype(BF16), head_sum, preferred_element_type=F32)
        return jnp.broadcast_to(jnp.max(n2, axis=0, keepdims=True), (8, LANES))

    qn_ref[0] = max_sq_norm(q16)
    kn_ref[0] = max_sq_norm(k16)


def _fox_proj(x16, w_fox):
    T, D = x16.shape
    W = w_fox.shape[1] // 3
    tm = _row_tile(T, 512)
    out = pl.BlockSpec((tm, W), lambda i: (i, 0))
    nrm = pl.BlockSpec((1, 8, LANES), lambda i: (i, 0, 0))
    return pl.pallas_call(
        functools.partial(_fox_proj_kernel, W),
        grid=(T // tm,),
        in_specs=[pl.BlockSpec((tm, D), lambda i: (i, 0)),
                  pl.BlockSpec((D, 3 * W), lambda i: (0, 0))],
        out_specs=[out, out, out, out, out, nrm, nrm],
        out_shape=[jax.ShapeDtypeStruct((T, W), BF16), jax.ShapeDtypeStruct((T, W), F32),
                   jax.ShapeDtypeStruct((T, W), F32), jax.ShapeDtypeStruct((T, W), BF16),
                   jax.ShapeDtypeStruct((T, W), BF16),
                   jax.ShapeDtypeStruct((T // tm, 8, LANES), F32),
                   jax.ShapeDtypeStruct((T // tm, 8, LANES), F32)],
        compiler_params=_cparams("parallel"),
        name="fox_proj",
    )(x16, w_fox)


def _log_sigmoid(z):
    return -(jnp.maximum(-z, 0.0) + jnp.log1p(jnp.exp(-jnp.abs(z))))


def _conv_fgt_proj_kernel(cc, nh, x_ref, w_ref, bf_ref, u_ref, lf_ref):
    x = x_ref[...]
    a = jnp.dot(x, w_ref[:, 0:cc], preferred_element_type=F32)
    gt = jnp.dot(x, w_ref[:, cc:2 * cc], preferred_element_type=F32)
    u_ref[...] = a * jax.nn.sigmoid(gt)
    z = jnp.dot(x, w_ref[:, 2 * cc:2 * cc + LANES], preferred_element_type=F32) + bf_ref[...]
    lf_ref[...] = _log_sigmoid(z)[:, :nh]


def _conv_fgt_proj(x16, w_cf, b_forget, cc, nh):
    T, D = x16.shape
    tm = _row_tile(T, 512)
    nw = w_cf.shape[1]
    bf = jnp.zeros((1, LANES), F32).at[0, :nh].set(b_forget)
    return pl.pallas_call(
        functools.partial(_conv_fgt_proj_kernel, cc, nh),
        grid=(T // tm,),
        in_specs=[pl.BlockSpec((tm, D), lambda i: (i, 0)),
                  pl.BlockSpec((D, nw), lambda i: (0, 0)),
                  pl.BlockSpec((1, LANES), lambda i: (0, 0))],
        out_specs=[pl.BlockSpec((tm, cc), lambda i: (i, 0)),
                   pl.BlockSpec((tm, nh), lambda i: (i, 0))],
        out_shape=[jax.ShapeDtypeStruct((T, cc), F32), jax.ShapeDtypeStruct((T, nh), F32)],
        compiler_params=_cparams("parallel"),
        name="conv_fgt_proj",
    )(x16, w_cf, bf)


def _trunc16(x):
    bits = lax.bitcast_convert_type(x, jnp.uint32) & jnp.uint32(0xFFFF0000)
    return lax.bitcast_convert_type(bits, F32)


def _split3(x):
    hi = _trunc16(x)
    r1 = x - hi
    mid = _trunc16(r1)
    lo = r1 - mid
    return hi.astype(BF16), mid.astype(BF16), lo.astype(BF16)


def _cumsum_kernel(nblk, x_ref, o_ref):
    r = lax.broadcasted_iota(jnp.int32, (LANES, LANES), 0)
    c = lax.broadcasted_iota(jnp.int32, (LANES, LANES), 1)
    tri = (r <= c).astype(BF16)

    def body(j, carry):
        hi, mid, lo = _split3(x_ref[0, j])
        s = (jnp.dot(hi, tri, preferred_element_type=F32)
             + jnp.dot(mid, tri, preferred_element_type=F32)
             + jnp.dot(lo, tri, preferred_element_type=F32)) + carry
        o_ref[0, j] = s
        return s[:, LANES - 1:LANES]

    lax.fori_loop(0, nblk, body, jnp.zeros((x_ref.shape[2], 1), F32))


def _cumsum_time(x):
    n, L, H = x.shape
    assert L % LANES == 0
    nblk = L // LANES
    xt = x.reshape(n, nblk, LANES, H).transpose(0, 1, 3, 2)
    blk = pl.BlockSpec((1, nblk, H, LANES), lambda i: (i, 0, 0, 0))
    out = pl.pallas_call(
        functools.partial(_cumsum_kernel, nblk),
        grid=(n,),
        in_specs=[blk],
        out_specs=blk,
        out_shape=jax.ShapeDtypeStruct((n, nblk, H, LANES), F32),
        compiler_params=_cparams("parallel"),
        name="cumsum_logf",
    )(xt)
    return out.transpose(0, 1, 3, 2).reshape(n, L, H)


RET_GROUP_HEADS = 4
RET_GROUP_LANES = RET_GROUP_HEADS * HEAD_DIM
RET_CHUNKS_PER_STEP = 8


def _ret_perm(n_heads):
    half = HEAD_DIM // 2
    perm = np.zeros(n_heads * HEAD_DIM, np.int32)
    for h in range(n_heads):
        g, hh = divmod(h, RET_GROUP_HEADS)
        for hf in range(2):
            for j in range(half):
                perm[g * RET_GROUP_LANES + hf * LANES + hh * half + j] = h * HEAD_DIM + hf * half + j
    return perm


def _retention_kernel(n_groups, cps, x_ref, cos_ref, sin_ref, inner_ref, qdec_ref, kdec_ref, cdec_ref,
                      s0_ref, o_ref, sout_ref, s_scr, o_scr):
    W = n_groups * RET_GROUP_LANES
    c = CHUNK
    step = pl.program_id(0)
    n_chunks = x_ref.shape[0] // c

    k_lane = lax.broadcasted_iota(jnp.int32, (c, RET_GROUP_LANES), 1)
    k_head = (k_lane % LANES) // (HEAD_DIM // 2)
    v_head = k_lane // HEAD_DIM
    sr = lax.broadcasted_iota(jnp.int32, (RET_GROUP_LANES, RET_GROUP_LANES), 0)
    sc = lax.broadcasted_iota(jnp.int32, (RET_GROUP_LANES, RET_GROUP_LANES), 1)
    s_mask = ((sr % LANES) // (HEAD_DIM // 2)) == (sc // HEAD_DIM)

    for ci in range(n_chunks):
        rows = slice(ci * c, (ci + 1) * c)
        cos = cos_ref[rows, :]
        sin = sin_ref[rows, :]
        if cps == 1:
            seq_local, first, last = ci, True, True
        else:
            seq_local = 0
            first = (step * n_chunks + ci) % cps == 0
            last = (step * n_chunks + ci) % cps == cps - 1
        for g in range(n_groups):
            lo = g * RET_GROUP_LANES

            def rot(t):
                t1, t2 = t[:, :LANES], t[:, LANES:]
                return jnp.concatenate([t1 * cos - t2 * sin, t1 * sin + t2 * cos], axis=1)

            q = rot(x_ref[rows, lo:lo + RET_GROUP_LANES])
            k = rot(x_ref[rows, W + lo:W + lo + RET_GROUP_LANES])
            v = x_ref[rows, 2 * W + lo:2 * W + lo + RET_GROUP_LANES]
            q16 = q.astype(BF16)
            k16 = k.astype(BF16)
            v16 = v.astype(BF16)
            kd16 = (k * kdec_ref[g]).astype(BF16)

            if cps == 1:
                s_prev = s0_ref[seq_local, g]
            else:
                @pl.when(first)
                def _():
                    s_scr[g] = s0_ref[0, g]
                s_prev = s_scr[g]

            zero = jnp.zeros_like(k16)
            k_bd = jnp.concatenate([jnp.where(k_head == hh, k16, zero) for hh in range(RET_GROUP_HEADS)], axis=0)
            v_bd = jnp.concatenate([jnp.where(v_head == hh, v16, zero) for hh in range(RET_GROUP_HEADS)], axis=0)
            s_all = lax.dot_general(q16, k_bd, (((1,), (1,)), ((), ())), preferred_element_type=F32)
            att = (s_all * inner_ref[g]).astype(BF16)
            o_intra = jnp.dot(att, v_bd, preferred_element_type=F32)
            o_cross = jnp.dot(q16, s_prev.astype(BF16), preferred_element_type=F32) * qdec_ref[g]
            o_scr[rows, lo:lo + RET_GROUP_LANES] = o_intra + o_cross

            kv = lax.dot_general(kd16, v16, (((0,), (0,)), ((), ())), preferred_element_type=F32)
            s_new = s_prev * cdec_ref[g] + jnp.where(s_mask, kv, 0.0)
            if cps == 1:
                sout_ref[seq_local, g] = s_new
            else:
                s_scr[g] = s_new

                @pl.when(last)
                def _():
                    sout_ref[0, g] = s_new

    pr = lax.broadcasted_iota(jnp.int32, (W, W), 0) // HEAD_DIM
    pc = lax.broadcasted_iota(jnp.int32, (W, W), 1) // HEAD_DIM
    avg = jnp.where(pr == pc, 1.0 / HEAD_DIM, 0.0).astype(BF16)

    def head_mean(t):
        hi, mid, lo3 = _split3(t)
        return (jnp.dot(hi, avg, preferred_element_type=F32)
                + jnp.dot(mid, avg, preferred_element_type=F32)
                + jnp.dot(lo3, avg, preferred_element_type=F32))

    o = o_scr[...]
    oc = o - head_mean(o)
    var = head_mean(oc * oc)
    gate = x_ref[:, 3 * W:4 * W]
    o_ref[...] = (oc * lax.rsqrt(var + LN_EPS) * (gate * jax.nn.sigmoid(gate))).astype(BF16)


def _retention_tables(n_heads):
    c = CHUNK
    n_groups = n_heads // RET_GROUP_HEADS
    lg = jnp.log1p(-jnp.exp2(-5.0 - jnp.arange(n_heads, dtype=F32)))
    idx = jnp.arange(c, dtype=F32)
    diff = idx[:, None] - idx[None, :]
    causal = diff >= 0
    inner = jnp.where(causal[None], jnp.exp(jnp.where(causal, diff, 0.0)[None] * lg[:, None, None]), 0.0)
    q_dec = jnp.exp((idx + 1.0)[:, None] * lg[None, :])
    k_dec = jnp.exp((c - 1.0 - idx)[:, None] * lg[None, :])
    c_dec = jnp.exp(c * lg)
    inner_t = inner.reshape(n_groups, RET_GROUP_HEADS, c, c).transpose(0, 2, 1, 3).reshape(n_groups, c, RET_GROUP_HEADS * c)
    out_head = np.arange(RET_GROUP_LANES) // HEAD_DIM
    k_head = (np.arange(RET_GROUP_LANES) % LANES) // (HEAD_DIM // 2)
    gh = np.arange(n_groups)[:, None] * RET_GROUP_HEADS
    qdec_t = q_dec[:, gh + out_head[None, :]].transpose(1, 0, 2)
    kdec_t = k_dec[:, gh + k_head[None, :]].transpose(1, 0, 2)
    cdec_t = c_dec[gh + out_head[None, :]][:, None, :]
    return inner_t, qdec_t, kdec_t, cdec_t


def _rotary_tables(pos):
    half = HEAD_DIM // 2
    inv = ROPE_BASE ** (-jnp.arange(half, dtype=F32) / half)
    ang = pos.astype(F32)[:, None] * inv[None, :]
    return (jnp.tile(jnp.cos(ang), (1, LANES // half)), jnp.tile(jnp.sin(ang), (1, LANES // half)))


def _retention(xr, cos, sin, s0g, row0, n_rows, seq_len, n_heads):
    W = n_heads * HEAD_DIM
    n_groups = n_heads // RET_GROUP_HEADS
    nseq = n_rows // seq_len
    cps = seq_len // CHUNK
    tl = RET_CHUNKS_PER_STEP * CHUNK
    assert n_rows % tl == 0 and row0 % tl == 0 and (cps == 1 or cps % RET_CHUNKS_PER_STEP == 0)
    off = row0 // tl
    if cps == 1:
        sblk = pl.BlockSpec((RET_CHUNKS_PER_STEP, n_groups, RET_GROUP_LANES, RET_GROUP_LANES),
                            lambda i: (i, 0, 0, 0))
    else:
        spc = cps // RET_CHUNKS_PER_STEP
        sblk = pl.BlockSpec((1, n_groups, RET_GROUP_LANES, RET_GROUP_LANES), lambda i: (i // spc, 0, 0, 0))
    inner_t, qdec_t, kdec_t, cdec_t = _retention_tables(n_heads)

    def full(a):
        return pl.BlockSpec(a.shape, lambda i: (0,) * a.ndim)

    return pl.pallas_call(
        functools.partial(_retention_kernel, n_groups, cps),
        grid=(n_rows // tl,),
        in_specs=[pl.BlockSpec((tl, 4 * W), lambda i: (i + off, 0)),
                  pl.BlockSpec((tl, LANES), lambda i: (i + off, 0)),
                  pl.BlockSpec((tl, LANES), lambda i: (i + off, 0)),
                  full(inner_t), full(qdec_t), full(kdec_t), full(cdec_t), sblk],
        out_specs=[pl.BlockSpec((tl, W), lambda i: (i, 0)), sblk],
        out_shape=[jax.ShapeDtypeStruct((n_rows, W), BF16),
                   jax.ShapeDtypeStruct((nseq, n_groups, RET_GROUP_LANES, RET_GROUP_LANES), F32)],
        scratch_shapes=[pltpu.VMEM((n_groups, RET_GROUP_LANES, RET_GROUP_LANES), F32),
                        pltpu.VMEM((tl, W), F32)],
        compiler_params=_cparams("arbitrary"),
        name="retention",
    )(xr, cos, sin, inner_t, qdec_t, kdec_t, cdec_t, s0g)


def _ret_state_to_groups(s):
    n, H = s.shape[:2]
    G = H // RET_GROUP_HEADS
    half = HEAD_DIM // 2
    t = s.reshape(n, G, RET_GROUP_HEADS, 2, half, HEAD_DIM).transpose(0, 1, 3, 2, 4, 5)
    eye = jnp.eye(RET_GROUP_HEADS, dtype=s.dtype)
    t = t[:, :, :, :, :, None, :] * eye[None, None, None, :, None, :, None]
    return t.reshape(n, G, RET_GROUP_LANES, RET_GROUP_LANES)


def _ret_state_from_groups(sg):
    n, G = sg.shape[:2]
    half = HEAD_DIM // 2
    t = sg.reshape(n, G, 2, RET_GROUP_HEADS, half, RET_GROUP_HEADS, HEAD_DIM)
    t = jnp.stack([t[:, :, :, hh, :, hh, :] for hh in range(RET_GROUP_HEADS)], axis=2)
    return t.reshape(n, G * RET_GROUP_HEADS, HEAD_DIM, HEAD_DIM)


CONV_HALO = 32


def _conv_kernel(tl, u_ref, prev_ref, w_ref, cb_ref, g_ref, b_ref, o_ref, buf):
    j = pl.program_id(1)

    @pl.when(j == 0)
    def _():
        buf[0:CONV_HALO, :] = prev_ref[0]

    @pl.when(j > 0)
    def _():
        buf[0:CONV_HALO, :] = buf[tl:tl + CONV_HALO, :]

    buf[CONV_HALO:CONV_HALO + tl, :] = u_ref[...]
    off = CONV_HALO - (CONV_WIDTH - 1)
    acc = buf[off:off + tl, :] * w_ref[0:1, :]
    for w in range(1, CONV_WIDTH):
        acc = acc + buf[off + w:off + w + tl, :] * w_ref[w:w + 1, :]
    y = _ln_rows(acc + cb_ref[...], g_ref[...], b_ref[...])
    o_ref[...] = (y * jax.nn.sigmoid(y)).astype(BF16)


def _conformer_conv(u, prev, w, cb, g, b, row0, seq_len):
    C = u.shape[1]
    nseq = prev.shape[0]
    tl = _row_tile(seq_len, 512)
    assert tl >= CONV_HALO and row0 % tl == 0
    off = row0 // tl
    nt = seq_len // tl
    prev_p = jnp.pad(prev.astype(F32), ((0, 0), (CONV_HALO - (CONV_WIDTH - 1), 0), (0, 0)))
    w_p = jnp.pad(w, ((0, CONV_HALO - CONV_WIDTH), (0, 0)))
    vec = pl.BlockSpec((1, C), lambda s, j: (0, 0))
    return pl.pallas_call(
        functools.partial(_conv_kernel, tl),
        grid=(nseq, nt),
        in_specs=[pl.BlockSpec((tl, C), lambda s, j: (off + s * nt + j, 0)),
                  pl.BlockSpec((1, CONV_HALO, C), lambda s, j: (s, 0, 0)),
                  pl.BlockSpec((CONV_HALO, C), lambda s, j: (0, 0)),
                  vec, vec, vec],
        out_specs=pl.BlockSpec((tl, C), lambda s, j: (s * nt + j, 0)),
        out_shape=jax.ShapeDtypeStruct((nseq * seq_len, C), BF16),
        scratch_shapes=[pltpu.VMEM((tl + CONV_HALO, C), F32)],
        compiler_params=_cparams("arbitrary", "arbitrary"),
        name="conformer_conv",
    )(u, prev_p, w_p, cb.reshape(1, C), g.reshape(1, C), b.reshape(1, C))


FOX_HEADS_PER_STEP = 2
LOG2E = 1.4426950408889634
FOX_SKIP_LOG2 = -152.0


def _split3_f32(x):
    hi = _trunc16(x)
    r1 = x - hi
    mid = _trunc16(r1)
    return hi, mid, r1 - mid


def _fox_prompt_kernel(bq, nkb, jlo_ref, q_ref, k_ref, v_ref, cq_ref, ck_ref, o_ref,
                       kt_scr, m_scr, l_scr, acc_scr):
    pair = pl.program_id(0)
    i = pl.program_id(1)
    heads = range(FOX_HEADS_PER_STEP)

    def key_rows(j):
        return pl.ds(pl.multiple_of(j * bq, bq), bq)

    @pl.when(i == 0)
    def _():
        row = lax.broadcasted_iota(jnp.int32, (LANES, bq), 0)
        ones = ((row < 3) | ((row >= 6) & (row < 9))).astype(F32)

        def build(j, carry):
            kt_scr[j, 0:LANES, :] = k_ref[key_rows(j), :].T
            ck = ck_ref[j]
            bias = ones
            for h in heads:
                terms = _split3_f32(-ck[h:h + 1, :])
                for t in range(3):
                    bias = jnp.where(row == 6 * h + 3 + t, terms[t], bias)
            kt_scr[j, LANES:2 * LANES, :] = bias.astype(BF16)
            return carry

        lax.fori_loop(0, nkb, build, 0)

    q = q_ref[...]
    lane = lax.broadcasted_iota(jnp.int32, (bq, LANES), 1)
    cq = cq_ref[...]
    qs = []
    for h in heads:
        terms = _split3_f32(cq[:, h:h + 1])
        bias = ((lane >= 6 * h + 3) & (lane < 6 * h + 6)).astype(F32)
        for t in range(3):
            bias = jnp.where(lane == 6 * h + t, terms[t], bias)
        in_head = (lane < HEAD_DIM) if h == 0 else (lane >= HEAD_DIM)
        qs.append(jnp.concatenate([jnp.where(in_head, q, jnp.zeros_like(q)), bias.astype(BF16)], axis=1))
        m_scr[h] = jnp.full((bq, LANES), NEG_BIG, F32)
        l_scr[h] = jnp.zeros((bq, LANES), F32)
        acc_scr[h] = jnp.zeros((bq, LANES), F32)

    def update(h, s, v):
        m_prev = m_scr[h]
        m_new = jnp.maximum(m_prev, jnp.max(s, axis=1, keepdims=True))
        p = jnp.exp2(s - jnp.concatenate([m_new] * (bq // LANES), axis=1))
        alpha = jnp.exp2(m_prev - m_new)
        l_scr[h] = alpha * l_scr[h] + jnp.sum(p, axis=1, keepdims=True)
        acc_scr[h] = alpha * acc_scr[h] + jnp.dot(p.astype(BF16), v, preferred_element_type=F32)
        m_scr[h] = m_new

    def scores(h, j):
        return jnp.dot(qs[h], kt_scr[j], preferred_element_type=F32)

    def body(j, carry):
        s = [scores(h, j) for h in heads]
        v = v_ref[key_rows(j), :]
        for h in heads:
            update(h, s[h], v)
        return carry

    lax.fori_loop(jlo_ref[pair * nkb + i], i, body, 0)

    r = lax.broadcasted_iota(jnp.int32, (bq, bq), 0)
    c = lax.broadcasted_iota(jnp.int32, (bq, bq), 1)
    s = [scores(h, i) for h in heads]
    v = v_ref[key_rows(i), :]
    for h in heads:
        update(h, jnp.where(c <= r, s[h], NEG_BIG), v)

    o_ref[...] = jnp.where(lane < HEAD_DIM, acc_scr[0] / l_scr[0], acc_scr[1] / l_scr[1]).astype(BF16)


def _fox_skip(qn2, kn2, ct, n_heads, bq):
    L = ct.shape[0]
    nb = L // bq
    qn = jnp.sqrt(qn2[:nb, 0, :n_heads])
    kn = jnp.sqrt(kn2[:nb, 0, :n_heads])
    ctb = ct.reshape(nb, bq, n_heads)
    bound = (1.02 * qn[:, None, :] * (kn[None, :, :] + kn[:, None, :])
             + ctb[:, None, 0, :] - ctb[None, :, bq - 1, :])
    skip = (bound < FOX_SKIP_LOG2) & (jnp.arange(nb)[None, :, None] < jnp.arange(nb)[:, None, None])
    jlo = jnp.sum(jnp.cumprod(skip.astype(jnp.int32), axis=1), axis=1)
    jlo = jnp.min(jlo.reshape(nb, n_heads // FOX_HEADS_PER_STEP, FOX_HEADS_PER_STEP), axis=2)
    return jlo.T.reshape(-1).astype(jnp.int32)


def _fox_prompt_attention(q16, k16, v16, qn2, kn2, ct, L, bq):
    W = q16.shape[1]
    n_heads = W // HEAD_DIM
    hp = FOX_HEADS_PER_STEP
    n_pairs = n_heads // hp
    nkb = L // bq
    jlo = _fox_skip(qn2, kn2, ct, n_heads, bq)
    cq = ct.reshape(L, n_pairs, hp).transpose(1, 0, 2)
    ck = ct.reshape(nkb, bq, n_pairs, hp).transpose(2, 0, 3, 1)
    grid_spec = pltpu.PrefetchScalarGridSpec(
        num_scalar_prefetch=1,
        grid=(n_pairs, nkb),
        in_specs=[pl.BlockSpec((bq, LANES), lambda p, i, jl: (i, p)),
                  pl.BlockSpec((L, LANES), lambda p, i, jl: (0, p)),
                  pl.BlockSpec((L, LANES), lambda p, i, jl: (0, p)),
                  pl.BlockSpec((None, bq, hp), lambda p, i, jl: (p, i, 0)),
                  pl.BlockSpec((None, nkb, hp, bq), lambda p, i, jl: (p, 0, 0, 0))],
        out_specs=pl.BlockSpec((bq, LANES), lambda p, i, jl: (i, p)),
        scratch_shapes=[pltpu.VMEM((nkb, 2 * LANES, bq), BF16),
                        pltpu.VMEM((hp, bq, LANES), F32), pltpu.VMEM((hp, bq, LANES), F32),
                        pltpu.VMEM((hp, bq, LANES), F32)],
    )
    return pl.pallas_call(
        functools.partial(_fox_prompt_kernel, bq, nkb),
        grid_spec=grid_spec,
        out_shape=jax.ShapeDtypeStruct((L, W), BF16),
        compiler_params=_cparams("arbitrary", "arbitrary"),
        name="fox_prompt",
    )(jlo, q16, k16, v16, cq, ck)


def _fox_sample_kernel(past, ls, q_ref, kn_ref, vn_ref, kc_ref, vc_ref, cq_ref, ck_ref, o_ref):
    q = q_ref[...]
    lane = lax.broadcasted_iota(jnp.int32, q.shape, 1)
    zero = jnp.zeros_like(q)
    q2 = jnp.concatenate([jnp.where(lane < HEAD_DIM, q, zero), jnp.where(lane >= HEAD_DIM, q, zero)], axis=0)
    pad = jnp.zeros((LANES - ls, LANES), BF16)
    k_all = jnp.concatenate([kc_ref[0].astype(BF16), kn_ref[...], pad], axis=0)
    v_all = jnp.concatenate([vc_ref[0].astype(BF16), vn_ref[...], pad], axis=0)
    nk = past + LANES
    s = lax.dot_general(q2, k_all, (((1,), (1,)), ((), ())), preferred_element_type=F32)
    row = lax.broadcasted_iota(jnp.int32, (2 * ls, nk), 0)
    col = lax.broadcasted_iota(jnp.int32, (2 * ls, nk), 1)
    ck = jnp.where(row < ls, ck_ref[0, 0:1, :], ck_ref[0, 1:2, :])
    visible = (col < past) | (col - past <= row % ls)
    s = jnp.where(visible, s + cq_ref[0] - ck, NEG_BIG)
    p = jnp.exp2(s - jnp.max(s, axis=1, keepdims=True))
    den = jnp.sum(p, axis=1, keepdims=True)
    o2 = jnp.dot(p.astype(BF16), v_all, preferred_element_type=F32) / den
    o_ref[...] = jnp.where(lane < HEAD_DIM, o2[0:ls], o2[ls:2 * ls]).astype(BF16)


def _fox_sample_attention(q16, k16, v16, cache_k, cache_v, layer, cq, ck, row0, ls):
    W = q16.shape[1]
    past = cache_k.shape[1]
    n_pairs = W // LANES
    B = cq.shape[0] // n_pairs
    assert ls <= LANES and row0 % ls == 0
    off = row0 // ls
    rows = pl.BlockSpec((ls, LANES), lambda b, p: (off + b, p))
    cache = pl.BlockSpec((1, past, LANES), lambda b, p: (layer * B + b, 0, p))
    return pl.pallas_call(
        functools.partial(_fox_sample_kernel, past, ls),
        grid=(B, n_pairs),
        in_specs=[rows, rows, rows, cache, cache,
                  pl.BlockSpec((1, 2 * ls, 1), lambda b, p: (b * n_pairs + p, 0, 0)),
                  pl.BlockSpec((1, 2, past + LANES), lambda b, p: (b * n_pairs + p, 0, 0))],
        out_specs=pl.BlockSpec((ls, LANES), lambda b, p: (b, p)),
        out_shape=jax.ShapeDtypeStruct((B * ls, W), BF16),
        compiler_params=_cparams("parallel", "parallel"),
        name="fox_sample",
    )(q16, k16, v16, cache_k, cache_v, cq, ck)


def _out_router_kernel(alpha, n_first, x_ref, r0_ref, f0_ref, c0_ref, r1_ref, f1_ref, c1_ref,
                       w_ref, g_ref, b_ref, rw_ref, rb_ref, o32_ref, otok_ref, route_ref):
    first = pl.program_id(0) < n_first
    mixed = jnp.concatenate([jnp.where(first, r0_ref[...], r1_ref[...]),
                             jnp.where(first, f0_ref[...], f1_ref[...]),
                             jnp.where(first, c0_ref[...], c1_ref[...])], axis=1)
    m = jnp.dot(mixed, w_ref[...], preferred_element_type=F32)
    y = _ln_rows(alpha * x_ref[...] + m, g_ref[...], b_ref[...])
    o32_ref[...] = y
    tm, d = y.shape
    rpt = d // LANES
    for c in range(rpt):
        otok_ref[pl.ds(c, tm, stride=rpt), :] = y[:, c * LANES:(c + 1) * LANES]

    logits = jnp.dot(y.astype(BF16), rw_ref[...], preferred_element_type=F32) + rb_ref[...]
    lane = lax.broadcasted_iota(jnp.int32, logits.shape, 1).astype(F32)
    far = float(LANES)

    def rmax(t):
        return jnp.max(t, axis=1, keepdims=True)

    def rmin(t):
        return jnp.min(t, axis=1, keepdims=True)

    def rsum(t):
        return jnp.sum(t, axis=1, keepdims=True)

    gmask = lane < N_GROUPS
    lg = jnp.where(gmask, logits, -jnp.inf)
    mg = rmax(lg)
    p_grp = 1.0 / rsum(jnp.exp(lg - mg))
    g_idx = rmin(jnp.where(lg == mg, lane, far))
    e_lo = N_GROUPS + EXPERTS_PER_GROUP * g_idx
    emask = (lane >= e_lo) & (lane < e_lo + EXPERTS_PER_GROUP)
    le = jnp.where(emask, logits, -jnp.inf)
    ee = jnp.exp(le - rmax(le))
    pin = jnp.where(emask, ee / rsum(ee), -1.0)
    p1 = rmax(pin)
    i1 = rmin(jnp.where(pin == p1, lane, far))
    pin2 = jnp.where(lane == i1, -1.0, pin)
    p2 = rmax(pin2)
    i2 = rmin(jnp.where(pin2 == p2, lane, far))
    den = p1 + p2
    route_ref[...] = jnp.where(lane == 0, i1 - N_GROUPS,
                     jnp.where(lane == 1, i2 - N_GROUPS,
                     jnp.where(lane == 2, p_grp * p1 / den,
                     jnp.where(lane == 3, p_grp * p2 / den, 0.0))))


def _out_proj_router(x, mix0, mix1, w_out16, g, b, rw16, rb, alpha):
    T, D = x.shape
    tm = _row_tile(mix1[0].shape[0], 256)
    n0 = mix0[0].shape[0] // tm
    n1 = mix1[0].shape[0] // tm
    assert mix0[0].shape[0] % tm == 0 and (n0 + n1) * tm == T

    def rows(a):
        return pl.BlockSpec((tm, a.shape[1]), lambda i: (i, 0))

    def rows0(a):
        return pl.BlockSpec((tm, a.shape[1]), lambda i: (jnp.minimum(i, n0 - 1), 0))

    def rows1(a):
        return pl.BlockSpec((tm, a.shape[1]), lambda i: (jnp.maximum(i - n0, 0), 0))

    def full(a):
        return pl.BlockSpec(a.shape, lambda i: (0, 0))

    g2, b2 = g.reshape(1, D), b.reshape(1, D)
    return pl.pallas_call(
        functools.partial(_out_router_kernel, alpha, n0),
        grid=(T // tm,),
        in_specs=[rows(x)] + [rows0(a) for a in mix0] + [rows1(a) for a in mix1]
                 + [full(w_out16), full(g2), full(b2), full(rw16), full(rb)],
        out_specs=[pl.BlockSpec((tm, D), lambda i: (i, 0)),
                   pl.BlockSpec((tm * (D // LANES), LANES), lambda i: (i, 0)),
                   pl.BlockSpec((tm, LANES), lambda i: (i, 0))],
        out_shape=[jax.ShapeDtypeStruct((T, D), F32), jax.ShapeDtypeStruct((T * (D // LANES), LANES), F32),
                   jax.ShapeDtypeStruct((T, LANES), F32)],
        compiler_params=_cparams("parallel"),
        name="out_proj_router",
    )(x, *mix0, *mix1, w_out16, g2, b2, rw16, rb)


def _moe_kernel(bm, rpt, be_ref, nused_ref, base_ref, tok_ref, x_hbm, wg_ref, wu_ref, wd_ref, y_ref,
                wg16, wu16, wd16, xbuf, sem):
    blk = pl.program_id(0)
    n_used = nused_ref[0]
    used = blk < n_used
    slot = blk % 2
    prev_expert = be_ref[jnp.maximum(blk - 1, 0)]

    def start_gather(of_blk, to_slot):
        base = base_ref[of_blk]
        for r in range(bm):
            src = pl.ds(pl.multiple_of(tok_ref[base + r] * rpt, rpt), rpt)
            pltpu.make_async_copy(x_hbm.at[src], xbuf.at[to_slot, pl.ds(r * rpt, rpt)],
                                  sem.at[to_slot]).start()

    def wait_gather():
        pltpu.make_async_copy(x_hbm.at[pl.ds(0, bm * rpt)], xbuf.at[slot], sem.at[slot]).wait()

    @pl.when(blk == 0)
    def _():
        start_gather(0, 0)

    @pl.when(used & ((blk == 0) | (be_ref[blk] != prev_expert)))
    def _():
        wg16[...] = wg_ref[0].astype(BF16)
        wu16[...] = wu_ref[0].astype(BF16)
        wd16[...] = wd_ref[0].astype(BF16)

    @pl.when(blk == n_used)
    def _():
        wait_gather()

    @pl.when(used)
    def _():
        wait_gather()
        start_gather(blk + 1, 1 - slot)
        xs = xbuf.at[slot]
        x = jnp.concatenate([xs[pl.ds(c, bm, stride=rpt), :] for c in range(rpt)], axis=1).astype(BF16)
        a = jnp.dot(x, wg16[...], preferred_element_type=F32)
        u = jnp.dot(x, wu16[...], preferred_element_type=F32)
        h = (a * jax.nn.sigmoid(a) * u).astype(BF16)
        y_ref[...] = jnp.dot(h, wd16[...], preferred_element_type=F32).astype(y_ref.dtype)

    @pl.when(jnp.logical_not(used))
    def _():
        y_ref[...] = jnp.zeros_like(y_ref)


def _moe_experts(xt, tok_sorted, base, block_expert, n_used, wg, wu, wd, layer, n_rows):
    D, DE = wg.shape[2], wg.shape[3]
    rpt = D // LANES
    bm = MOE_BLOCK_ROWS
    n_blk = n_rows // bm

    def weights(rows, cols):
        return pl.BlockSpec((None, 1, rows, cols), lambda i, be, nu, ba, tk: (layer, be[i], 0, 0))

    grid_spec = pltpu.PrefetchScalarGridSpec(
        num_scalar_prefetch=4,
        grid=(n_blk,),
        in_specs=[pl.BlockSpec(memory_space=pl.ANY), weights(D, DE), weights(D, DE), weights(DE, D)],
        out_specs=pl.BlockSpec((bm, D), lambda i, be, nu, ba, tk: (i, 0)),
        scratch_shapes=[pltpu.VMEM((D, DE), BF16), pltpu.VMEM((D, DE), BF16), pltpu.VMEM((DE, D), BF16),
                        pltpu.VMEM((2, bm * rpt, LANES), F32), pltpu.SemaphoreType.DMA((2,))],
    )
    return pl.pallas_call(
        functools.partial(_moe_kernel, bm, rpt),
        grid_spec=grid_spec,
        out_shape=jax.ShapeDtypeStruct((n_rows, D), BF16),
        compiler_params=_cparams("arbitrary"),
        name="moe_experts",
    )(block_expert, n_used, base, tok_sorted, xt, wg, wu, wd)


def _moe_dispatch(route):
    T = route.shape[0]
    bm = MOE_BLOCK_ROWS
    eid = route[:, 0:2].astype(jnp.int32).reshape(-1)
    n_assign = 2 * T
    n_rows = n_assign + N_EXPERTS * bm
    n_blocks = n_rows // bm
    eid_s, a_s = lax.sort((eid, jnp.arange(n_assign, dtype=jnp.int32)), num_keys=1, is_stable=True)
    onehot_s = (eid_s[:, None] == jnp.arange(N_EXPERTS)[None, :]).astype(F32)
    exact = lax.Precision.HIGHEST
    counts = jnp.dot(jnp.ones((1, n_assign), F32), onehot_s, precision=exact)[0].astype(jnp.int32)
    padded = ((counts + bm - 1) // bm) * bm
    ends = jnp.cumsum(padded)
    starts = ends - padded
    first = jnp.cumsum(counts) - counts
    shift = (starts - first).astype(F32)
    pos_s = jnp.arange(n_assign, dtype=jnp.int32) + jnp.dot(onehot_s, shift, precision=exact).astype(jnp.int32)
    _, pos = lax.sort((a_s, pos_s), num_keys=1)
    n_used = ends[-1] // bm
    blk_start = jnp.arange(n_blocks, dtype=jnp.int32) * bm
    block_expert = jnp.sum((ends[None, :] <= blk_start[:, None]).astype(jnp.int32), axis=1)
    block_expert = jnp.minimum(block_expert, N_EXPERTS - 1)
    last_used = block_expert[jnp.maximum(n_used - 1, 0)]
    block_expert = jnp.where(jnp.arange(n_blocks) < n_used, block_expert, last_used)
    base = jnp.clip(blk_start - starts[block_expert] + first[block_expert], 0, n_assign)
    tok_sorted = jnp.concatenate([a_s // 2, jnp.zeros((bm,), jnp.int32)])
    return tok_sorted, base, block_expert, n_used.reshape(1).astype(jnp.int32), pos.reshape(T, 2), n_rows


def kernel(x_prompt, x_sample, cache_fox_k, cache_fox_v, cache_fox_logf, state_ret, state_conv, ln_in_g, ln_in_b, w_in, b_forget, conv_w, conv_b, conv_ln_g, conv_ln_b, w_out, ln1_g, ln1_b, router_group_w, router_group_b, router_expert_w, router_expert_b, w_gate, w_up, w_down, ln2_g, ln2_b):
    BP, LP, D = x_prompt.shape
    BS, LS, _ = x_sample.shape
    depth = w_in.shape[0]
    past = cache_fox_k.shape[2]
    fox_heads = cache_fox_k.shape[3]
    ret_heads = state_ret.shape[2]
    conv_ch = state_conv.shape[3]
    ret_w = ret_heads * HEAD_DIM
    fox_w = fox_heads * HEAD_DIM
    off_fox = 4 * ret_w
    off_fgt = off_fox + 3 * fox_w
    off_conv = off_fgt + fox_heads
    alpha = (2 * depth) ** 0.25
    TP, TS = BP * LP, BS * LS
    assert BP == 1 and LS == CHUNK and past % 512 == 0

    x32, x16 = _layer_norm_in(x_prompt.reshape(TP, D), x_sample.reshape(TS, D), ln_in_g, ln_in_b)

    perm = _ret_perm(ret_heads)
    pos_all = jnp.concatenate([jnp.arange(LP), past + jnp.tile(jnp.arange(LS), BS)])
    cos_t, sin_t = _rotary_tables(pos_all)
    fox_bq = _row_tile(LP, 512)
    n_pairs = fox_heads // FOX_HEADS_PER_STEP
    cache_k = cache_fox_k.reshape(depth * BS, past, fox_w)
    cache_v = cache_fox_v.reshape(depth * BS, past, fox_w)
    qk_scale = HEAD_DIM ** -0.5
    zero_state = jnp.zeros((BP, ret_heads // RET_GROUP_HEADS, RET_GROUP_LANES, RET_GROUP_LANES), F32)
    zero_conv = jnp.zeros((BP, CONV_WIDTH - 1, conv_ch), F32)

    ks, vs, lfs, rs_p, rs_s, cs_p, cs_s = [], [], [], [], [], [], []
    for l in range(depth):
        wl = w_in[l]
        w_ret = jnp.concatenate([wl[:, 0:ret_w][:, perm],
                                 (wl[:, ret_w:2 * ret_w] * qk_scale)[:, perm],
                                 wl[:, 2 * ret_w:4 * ret_w]], axis=1).astype(BF16)
        w_fox = jnp.concatenate([wl[:, off_fox:off_fox + fox_w] * (qk_scale * LOG2E),
                                 wl[:, off_fox + fox_w:off_fgt]], axis=1).astype(BF16)
        w_cf = jnp.concatenate([wl[:, off_conv:], wl[:, off_fgt:off_conv],
                                jnp.zeros((D, LANES - fox_heads), F32)], axis=1).astype(BF16)

        xr = _matmul(x16, w_ret, F32, "ret_proj")
        fq16, fk32, fv32, fk16, fv16, qn2, kn2 = _fox_proj(x16, w_fox)
        u, logf = _conv_fgt_proj(x16, w_cf, b_forget[l], conv_ch, fox_heads)

        o_r_p, sg_p = _retention(xr, cos_t, sin_t, zero_state, 0, TP, LP, ret_heads)
        o_r_s, sg_s = _retention(xr, cos_t, sin_t, _ret_state_to_groups(state_ret[l]), TP, TS, LS, ret_heads)
        rs_p.append(_ret_state_from_groups(sg_p))
        rs_s.append(_ret_state_from_groups(sg_s))

        cw, cb, cg, cbeta = conv_w[l], conv_b[l], conv_ln_g[l], conv_ln_b[l]
        o_c_p = _conformer_conv(u, zero_conv, cw, cb, cg, cbeta, 0, LP)
        o_c_s = _conformer_conv(u, state_conv[l], cw, cb, cg, cbeta, TP, LS)
        u_p = u[:TP].reshape(BP, LP, conv_ch)
        u_s = u[TP:].reshape(BS, LS, conv_ch)
        cs_p.append(u_p[:, LP - (CONV_WIDTH - 1):])
        cs_s.append(jnp.concatenate([state_conv[l], u_s], axis=1)[:, -(CONV_WIDTH - 1):])

        ct_p = _cumsum_time(logf[:TP].reshape(BP, LP, fox_heads))[0] * LOG2E
        o_f_p = _fox_prompt_attention(fq16, fk16, fv16, qn2, kn2, ct_p, LP, fox_bq)

        lf_all = jnp.concatenate([cache_fox_logf[l].astype(F32), logf[TP:].reshape(BS, LS, fox_heads),
                                  jnp.zeros((BS, LANES - LS, fox_heads), F32)], axis=1)
        ct_s = (_cumsum_time(lf_all) * LOG2E).transpose(0, 2, 1)
        ck_s = ct_s.reshape(BS * n_pairs, FOX_HEADS_PER_STEP, past + LANES)
        cq_s = ct_s[:, :, past:past + LS].reshape(BS * n_pairs, FOX_HEADS_PER_STEP * LS, 1)
        o_f_s = _fox_sample_attention(fq16, fk16, fv16, cache_k, cache_v, l, cq_s, ck_s, TP, LS)

        rw = jnp.concatenate([router_group_w[l], router_expert_w[l],
                              jnp.zeros((D, LANES - N_GROUPS - N_EXPERTS), F32)], axis=1).astype(BF16)
        rb = jnp.concatenate([router_group_b[l], router_expert_b[l],
                              jnp.zeros((LANES - N_GROUPS - N_EXPERTS,), F32)]).reshape(1, LANES)
        x1_32, x1_tok, route = _out_proj_router(x32, (o_r_p, o_f_p, o_c_p), (o_r_s, o_f_s, o_c_s),
                                                w_out[l].astype(BF16), ln1_g[l], ln1_b[l], rw, rb, alpha)

        tok_sorted, base, block_expert, n_used, pos, n_rows = _moe_dispatch(route)
        ys = _moe_experts(x1_tok, tok_sorted, base, block_expert, n_used, w_gate, w_up, w_down, l, n_rows)
        f0 = jnp.take(ys, pos[:, 0], axis=0)
        f1 = jnp.take(ys, pos[:, 1], axis=0)
        x32, x16 = _residual_ln(x1_32, f0, f1, route, ln2_g[l], ln2_b[l], alpha)

        ks.append(fk32)
        vs.append(fv32)
        lfs.append(logf)

    def split(ts, tail):
        a = jnp.stack(ts)
        return (a[:, :TP].reshape((depth, BP, LP) + tail), a[:, TP:].reshape((depth, BS, LS) + tail))

    pk, sk = split(ks, (fox_heads, HEAD_DIM))
    pv, sv = split(vs, (fox_heads, HEAD_DIM))
    plf, slf = split(lfs, (fox_heads,))
    return (x32[:TP].reshape(BP, LP, D), x32[TP:].reshape(BS, LS, D),
            pk, pv, plf, jnp.stack(rs_p), jnp.stack(cs_p),
            sk, sv, slf, jnp.stack(rs_s), jnp.stack(cs_s))
```
